```python
import math
import jax, jax.numpy as jnp
from jax import lax
import numpy as np

D_MODEL = 2048
BATCH = 8
SEQ = 8192
DEPTH = 4

D_MIX = D_MODEL
POOL_WIDTH = D_MIX // 4
POOL_WINDOWS = (2, 4, 8, 16)
POOL_GROUP = POOL_WIDTH // len(POOL_WINDOWS)
SSD_WIDTH = D_MIX // 2
SSD_HEAD_DIM = 64
SSD_HEADS = SSD_WIDTH // SSD_HEAD_DIM
SSD_GROUPS = 2
SSD_STATE = 128
SSD_CONV = 4
SSD_CHUNK = 256
ATTN_WIDTH = D_MIX - POOL_WIDTH - SSD_WIDTH
ATTN_HEAD_DIM = 64
ATTN_HEADS = ATTN_WIDTH // ATTN_HEAD_DIM
ATTN_BLOCK = 128
XBC_WIDTH = SSD_WIDTH + 2 * SSD_GROUPS * SSD_STATE
IN_WIDTH = POOL_WIDTH + SSD_WIDTH + XBC_WIDTH + SSD_HEADS + 3 * ATTN_WIDTH
D_FF = ((8 * D_MODEL // 3 + 255) // 256) * 256
RMS_EPS = 1e-6

kernel_name = "hymba_pool_ssd_stickbreak_macaron"

F32 = jnp.float32


def rms_norm(x, g):
    xf = x.astype(F32)
    y = xf * lax.rsqrt(jnp.mean(xf * xf, axis=-1, keepdims=True) + RMS_EPS)
    return (y * g.astype(F32)).astype(x.dtype)


def swiglu(u, w_gate, w_up, w_down):
    a = jnp.einsum('bsd,df->bsf', u, w_gate)
    b = jnp.einsum('bsd,df->bsf', u, w_up)
    return jnp.einsum('bsf,fd->bsd', jax.nn.silu(a) * b, w_down)


def multiscale_pool(v, w_pool, scale):
    s_len = v.shape[1]
    vf = v.astype(F32)
    cs = jnp.cumsum(vf, axis=1)
    pos = jnp.arange(1, s_len + 1, dtype=F32)
    outs = []
    for i, w in enumerate(POOL_WINDOWS):
        sl = slice(i * POOL_GROUP, (i + 1) * POOL_GROUP)
        c = cs[..., sl]
        prev = jnp.pad(c, ((0, 0), (w, 0), (0, 0)))[:, :s_len]
        mean = (c - prev) / jnp.minimum(pos, float(w))[None, :, None]
        outs.append(jnp.einsum('bsc,cd->bsd', mean - vf[..., sl], w_pool[i].astype(F32)))
    return (jnp.concatenate(outs, axis=-1) * scale.astype(F32)).astype(v.dtype)


def causal_dwconv(x, w, b):
    k_len = w.shape[0]
    s_len = x.shape[1]
    xp = jnp.pad(x, ((0, 0), (k_len - 1, 0), (0, 0)))
    y = xp[:, 0:s_len] * w[0]
    for k in range(1, k_len):
        y = y + xp[:, k:k + s_len] * w[k]
    return y + b


def segsum_exp(a):
    t = a.shape[-1]
    strict = jnp.tril(jnp.ones((t, t), dtype=bool), -1)
    incl = jnp.tril(jnp.ones((t, t), dtype=bool))
    rep = jnp.where(strict, a[..., :, None], 0.0)
    ss = jnp.cumsum(rep, axis=-2)
    return jnp.where(incl, jnp.exp(ss), 0.0)


def ssd_chunked(X, A, Bm, Cm):
    b, s_len, h, p = X.shape
    pad = (-s_len) % SSD_CHUNK
    if pad:
        X = jnp.pad(X, ((0, 0), (0, pad), (0, 0), (0, 0)))
        A = jnp.pad(A, ((0, 0), (0, pad), (0, 0)))
        Bm = jnp.pad(Bm, ((0, 0), (0, pad), (0, 0), (0, 0)))
        Cm = jnp.pad(Cm, ((0, 0), (0, pad), (0, 0), (0, 0)))
    t_len = s_len + pad
    nc, L, g = t_len // SSD_CHUNK, SSD_CHUNK, SSD_GROUPS
    e = h // g
    n = Bm.shape[-1]
    X = X.reshape(b, nc, L, g, e, p)
    A = A.reshape(b, nc, L, g, e).transpose(0, 3, 4, 1, 2)
    Bm = Bm.reshape(b, nc, L, g, n)
    Cm = Cm.reshape(b, nc, L, g, n)
    A_cs = jnp.cumsum(A, axis=-1)
    decay_in = segsum_exp(A)
    CB = jnp.einsum('bclgn,bcsgn->bgcls', Cm, Bm)
    Y_diag = jnp.einsum('bgcls,bgecls,bcsgep->bclgep', CB, decay_in, X)
    decay_states = jnp.exp(A_cs[..., -1:] - A_cs)
    states = jnp.einsum('bclgn,bgecl,bclgep->bcgepn', Bm, decay_states, X)
    chunk_decay = jnp.moveaxis(jnp.exp(A_cs[..., -1]), -1, 0)
    states_c = jnp.moveaxis(states, 1, 0)

    def step(carry, inp):
        s_c, d_c = inp
        return carry * d_c[..., None, None] + s_c, carry

    init = jnp.zeros(states_c.shape[1:], dtype=states_c.dtype)
    _, states_in = lax.scan(step, init, (states_c, chunk_decay))
    Y_off = jnp.einsum('bclgn,cbgepn,bgecl->bclgep', Cm, states_in, jnp.exp(A_cs))
    Y = (Y_diag + Y_off).reshape(b, t_len, h, p)
    return Y[:, :s_len]


def ssd_mixer(z, xbc, dt_raw, conv_w, conv_b, dt_bias, a_log, d_skip, norm_g):
    bsz, s_len, _ = z.shape
    xbc = jax.nn.silu(causal_dwconv(xbc, conv_w, conv_b)).astype(F32)
    gn = SSD_GROUPS * SSD_STATE
    xs, Bm, Cm = jnp.split(xbc, [SSD_WIDTH, SSD_WIDTH + gn], axis=-1)
    xs = xs.reshape(bsz, s_len, SSD_HEADS, SSD_HEAD_DIM)
    Bm = Bm.reshape(bsz, s_len, SSD_GROUPS, SSD_STATE)
    Cm = Cm.reshape(bsz, s_len, SSD_GROUPS, SSD_STATE)
    dt = jax.nn.softplus(dt_raw.astype(F32) + dt_bias.astype(F32))
    A = -jnp.exp(a_log.astype(F32))
    y = ssd_chunked(xs * dt[..., None], A * dt, Bm, Cm)
    y = y + d_skip.astype(F32)[:, None] * xs
    y = y.reshape(bsz, s_len, SSD_WIDTH) * jax.nn.silu(z.astype(F32))
    yg = y.reshape(bsz, s_len, SSD_GROUPS, SSD_WIDTH // SSD_GROUPS)
    yg = yg * lax.rsqrt(jnp.mean(yg * yg, axis=-1, keepdims=True) + RMS_EPS)
    y = yg.reshape(bsz, s_len, SSD_WIDTH) * norm_g.astype(F32)
    return y.astype(z.dtype)


def stick_breaking_attention(q, k, v):
    bsz, s_len, nh, dh = q.shape
    nb = s_len // ATTN_BLOCK
    qf = (q.astype(F32) * (dh ** -0.5)).reshape(bsz, nb, ATTN_BLOCK, nh, dh).transpose(1, 0, 3, 2, 4)
    kf = k.astype(F32).transpose(0, 2, 1, 3)
    vf = v.astype(F32).transpose(0, 2, 1, 3)
    key_pos = jnp.arange(s_len, dtype=jnp.int32)
    starts = jnp.arange(nb, dtype=jnp.int32) * ATTN_BLOCK
    q_off = jnp.arange(ATTN_BLOCK, dtype=jnp.int32)

    def one_block(args):
        qb, q0 = args
        logits = jnp.einsum('bhqd,bhkd->bhqk', qb, kf)
        mask = key_pos[None, :] < (q0 + q_off)[:, None]
        log_1m = jnp.where(mask, jax.nn.log_sigmoid(-logits), 0.0)
        suffix = lax.cumsum(log_1m, axis=3, reverse=True) - log_1m
        weights = jnp.where(mask, jnp.exp(jax.nn.log_sigmoid(logits) + suffix), 0.0)
        return jnp.einsum('bhqk,bhkd->bhqd', weights, vf)

    out = lax.map(one_block, (qf, starts))
    return out.transpose(1, 0, 3, 2, 4).reshape(bsz, s_len, nh * dh).astype(q.dtype)


def hybrid_mixer(u, w_in, pool_w, pool_scale, conv_w, conv_b, dt_bias, a_log, d_skip, ssd_norm, w_out):
    bsz, s_len, _ = u.shape
    proj = jnp.einsum('bsd,dn->bsn', u, w_in)
    idx = np.cumsum([POOL_WIDTH, SSD_WIDTH, XBC_WIDTH, SSD_HEADS, ATTN_WIDTH, ATTN_WIDTH]).tolist()
    pool_in, z, xbc, dt_raw, q, k, v = jnp.split(proj, idx, axis=-1)
    pool_out = multiscale_pool(pool_in, pool_w, pool_scale)
    ssd_out = ssd_mixer(z, xbc, dt_raw, conv_w, conv_b, dt_bias, a_log, d_skip, ssd_norm)
    shp = (bsz, s_len, ATTN_HEADS, ATTN_HEAD_DIM)
    attn_out = stick_breaking_attention(q.reshape(shp), k.reshape(shp), v.reshape(shp))
    mixed = jnp.concatenate([pool_out, ssd_out, attn_out], axis=-1)
    return jnp.einsum('bsm,md->bsd', mixed, w_out)


def _fwd_setup_inputs(seed: int = 0) -> dict:
    key = jax.random.key(seed)
    ks = jax.random.split(key, 24)

    def nrm(k, shape, scale):
        return jax.random.normal(k, shape, dtype=F32) * scale

    def gain(k, shape):
        return 1.0 + 0.02 * jax.random.normal(k, shape, dtype=F32)

    dt0 = jnp.exp(jax.random.uniform(ks[13], (DEPTH, SSD_HEADS), dtype=F32)
                  * (math.log(0.1) - math.log(0.001)) + math.log(0.001))
    return {
        "x": jax.random.normal(ks[0], (BATCH, SEQ, D_MODEL), dtype=F32),
        "ffn1_norm": gain(ks[1], (DEPTH, D_MODEL)),
        "ffn1_w_gate": nrm(ks[2], (DEPTH, D_MODEL, D_FF), D_MODEL ** -0.5),
        "ffn1_w_up": nrm(ks[3], (DEPTH, D_MODEL, D_FF), D_MODEL ** -0.5),
        "ffn1_w_down": nrm(ks[4], (DEPTH, D_FF, D_MODEL), D_FF ** -0.5),
        "mix_norm": gain(ks[5], (DEPTH, D_MODEL)),
        "w_in": nrm(ks[6], (DEPTH, D_MODEL, IN_WIDTH), D_MODEL ** -0.5),
        "pool_w": nrm(ks[7], (DEPTH, len(POOL_WINDOWS), POOL_GROUP, POOL_GROUP), POOL_GROUP ** -0.5),
        "pool_scale": gain(ks[8], (DEPTH, POOL_WIDTH)),
        "conv_w": nrm(ks[9], (DEPTH, SSD_CONV, XBC_WIDTH), SSD_CONV ** -0.5),
        "conv_b": nrm(ks[10], (DEPTH, XBC_WIDTH), 0.01),
        "dt_bias": dt0 + jnp.log(-jnp.expm1(-dt0)),
        "a_log": jnp.log(jax.random.uniform(ks[11], (DEPTH, SSD_HEADS), dtype=F32, minval=1.0, maxval=16.0)),
        "d_skip": 1.0 + 0.1 * jax.random.normal(ks[12], (DEPTH, SSD_HEADS), dtype=F32),
        "ssd_norm": gain(ks[14], (DEPTH, SSD_WIDTH)),
        "w_out": nrm(ks[15], (DEPTH, D_MIX, D_MODEL), D_MIX ** -0.5),
        "ffn2_norm": gain(ks[16], (DEPTH, D_MODEL)),
        "ffn2_w_gate": nrm(ks[17], (DEPTH, D_MODEL, D_FF), D_MODEL ** -0.5),
        "ffn2_w_up": nrm(ks[18], (DEPTH, D_MODEL, D_FF), D_MODEL ** -0.5),
        "ffn2_w_down": nrm(ks[19], (DEPTH, D_FF, D_MODEL), D_FF ** -0.5),
        "final_norm": gain(ks[20], (D_MODEL,)),
    }


def _fwd_reference(x, ffn1_norm, ffn1_w_gate, ffn1_w_up, ffn1_w_down, mix_norm, w_in, pool_w, pool_scale,
              conv_w, conv_b, dt_bias, a_log, d_skip, ssd_norm, w_out, ffn2_norm, ffn2_w_gate,
              ffn2_w_up, ffn2_w_down, final_norm):
    h = x
    for l in range(DEPTH):
        u = rms_norm(h, ffn1_norm[l])
        h = h + 0.5 * swiglu(u, ffn1_w_gate[l], ffn1_w_up[l], ffn1_w_down[l])
        u = rms_norm(h, mix_norm[l])
        h = h + hybrid_mixer(u, w_in[l], pool_w[l], pool_scale[l], conv_w[l], conv_b[l], dt_bias[l],
                             a_log[l], d_skip[l], ssd_norm[l], w_out[l])
        u = rms_norm(h, ffn2_norm[l])
        h = h + 0.5 * swiglu(u, ffn2_w_gate[l], ffn2_w_up[l], ffn2_w_down[l])
    return rms_norm(h, final_norm)


import jax as _jax
import jax.numpy as _jnp

TWIN_FORMAT = 'train_step'
FWD_PARAMS = ['x', 'ffn1_norm', 'ffn1_w_gate', 'ffn1_w_up', 'ffn1_w_down', 'mix_norm', 'w_in', 'pool_w', 'pool_scale', 'conv_w', 'conv_b', 'dt_bias', 'a_log', 'd_skip', 'ssd_norm', 'w_out', 'ffn2_norm', 'ffn2_w_gate', 'ffn2_w_up', 'ffn2_w_down', 'final_norm']
TWIN_WEIGHTS = ['ffn1_norm', 'ffn1_w_gate', 'ffn1_w_up', 'ffn1_w_down', 'mix_norm', 'w_in', 'pool_w', 'pool_scale', 'conv_w', 'conv_b', 'dt_bias', 'a_log', 'd_skip', 'ssd_norm', 'w_out', 'ffn2_norm', 'ffn2_w_gate', 'ffn2_w_up', 'ffn2_w_down', 'final_norm']
TWIN_DIFF_INPUT = 'x'
TWIN_INPUTS = ['x', 'ffn1_norm', 'ffn1_w_gate', 'ffn1_w_up', 'ffn1_w_down', 'mix_norm', 'w_in', 'pool_w', 'pool_scale', 'conv_w', 'conv_b', 'dt_bias', 'a_log', 'd_skip', 'ssd_norm', 'w_out', 'ffn2_norm', 'ffn2_w_gate', 'ffn2_w_up', 'ffn2_w_down', 'final_norm', 'loss_target', 'm_ffn1_norm', 'm_ffn1_w_gate', 'm_ffn1_w_up', 'm_ffn1_w_down', 'm_mix_norm', 'm_w_in', 'm_pool_w', 'm_pool_scale', 'm_conv_w', 'm_conv_b', 'm_dt_bias', 'm_a_log', 'm_d_skip', 'm_ssd_norm', 'm_w_out', 'm_ffn2_norm', 'm_ffn2_w_gate', 'm_ffn2_w_up', 'm_ffn2_w_down', 'm_final_norm', 'v_ffn1_norm', 'v_ffn1_w_gate', 'v_ffn1_w_up', 'v_ffn1_w_down', 'v_mix_norm', 'v_w_in', 'v_pool_w', 'v_pool_scale', 'v_conv_w', 'v_conv_b', 'v_dt_bias', 'v_a_log', 'v_d_skip', 'v_ssd_norm', 'v_w_out', 'v_ffn2_norm', 'v_ffn2_w_gate', 'v_ffn2_w_up', 'v_ffn2_w_down', 'v_final_norm']
TWIN_OUTPUTS = ['loss', 'grad_x', 'grad_ffn1_norm', 'grad_ffn1_w_gate', 'grad_ffn1_w_up', 'grad_ffn1_w_down', 'grad_mix_norm', 'grad_w_in', 'grad_pool_w', 'grad_pool_scale', 'grad_conv_w', 'grad_conv_b', 'grad_dt_bias', 'grad_a_log', 'grad_d_skip', 'grad_ssd_norm', 'grad_w_out', 'grad_ffn2_norm', 'grad_ffn2_w_gate', 'grad_ffn2_w_up', 'grad_ffn2_w_down', 'grad_final_norm', 'delta_ffn1_norm', 'delta_ffn1_w_gate', 'delta_ffn1_w_up', 'delta_ffn1_w_down', 'delta_mix_norm', 'delta_w_in', 'delta_pool_w', 'delta_pool_scale', 'delta_conv_w', 'delta_conv_b', 'delta_dt_bias', 'delta_a_log', 'delta_d_skip', 'delta_ssd_norm', 'delta_w_out', 'delta_ffn2_norm', 'delta_ffn2_w_gate', 'delta_ffn2_w_up', 'delta_ffn2_w_down', 'delta_final_norm', 'new_m_ffn1_norm', 'new_m_ffn1_w_gate', 'new_m_ffn1_w_up', 'new_m_ffn1_w_down', 'new_m_mix_norm', 'new_m_w_in', 'new_m_pool_w', 'new_m_pool_scale', 'new_m_conv_w', 'new_m_conv_b', 'new_m_dt_bias', 'new_m_a_log', 'new_m_d_skip', 'new_m_ssd_norm', 'new_m_w_out', 'new_m_ffn2_norm', 'new_m_ffn2_w_gate', 'new_m_ffn2_w_up', 'new_m_ffn2_w_down', 'new_m_final_norm', 'new_v_ffn1_norm', 'new_v_ffn1_w_gate', 'new_v_ffn1_w_up', 'new_v_ffn1_w_down', 'new_v_mix_norm', 'new_v_w_in', 'new_v_pool_w', 'new_v_pool_scale', 'new_v_conv_w', 'new_v_conv_b', 'new_v_dt_bias', 'new_v_a_log', 'new_v_d_skip', 'new_v_ssd_norm', 'new_v_w_out', 'new_v_ffn2_norm', 'new_v_ffn2_w_gate', 'new_v_ffn2_w_up', 'new_v_ffn2_w_down', 'new_v_final_norm']
TWIN_LEAF_KINDS = {'loss': 'loss', 'grad_x': 'grad_x', 'grad_ffn1_norm': 'grad_w', 'grad_ffn1_w_gate': 'grad_w', 'grad_ffn1_w_up': 'grad_w', 'grad_ffn1_w_down': 'grad_w', 'grad_mix_norm': 'grad_w', 'grad_w_in': 'grad_w', 'grad_pool_w': 'grad_w', 'grad_pool_scale': 'grad_w', 'grad_conv_w': 'grad_w', 'grad_conv_b': 'grad_w', 'grad_dt_bias': 'grad_w', 'grad_a_log': 'grad_w', 'grad_d_skip': 'grad_w', 'grad_ssd_norm': 'grad_w', 'grad_w_out': 'grad_w', 'grad_ffn2_norm': 'grad_w', 'grad_ffn2_w_gate': 'grad_w', 'grad_ffn2_w_up': 'grad_w', 'grad_ffn2_w_down': 'grad_w', 'grad_final_norm': 'grad_w', 'delta_ffn1_norm': 'delta_w', 'delta_ffn1_w_gate': 'delta_w', 'delta_ffn1_w_up': 'delta_w', 'delta_ffn1_w_down': 'delta_w', 'delta_mix_norm': 'delta_w', 'delta_w_in': 'delta_w', 'delta_pool_w': 'delta_w', 'delta_pool_scale': 'delta_w', 'delta_conv_w': 'delta_w', 'delta_conv_b': 'delta_w', 'delta_dt_bias': 'delta_w', 'delta_a_log': 'delta_w', 'delta_d_skip': 'delta_w', 'delta_ssd_norm': 'delta_w', 'delta_w_out': 'delta_w', 'delta_ffn2_norm': 'delta_w', 'delta_ffn2_w_gate': 'delta_w', 'delta_ffn2_w_up': 'delta_w', 'delta_ffn2_w_down': 'delta_w', 'delta_final_norm': 'delta_w', 'new_m_ffn1_norm': 'new_m', 'new_m_ffn1_w_gate': 'new_m', 'new_m_ffn1_w_up': 'new_m', 'new_m_ffn1_w_down': 'new_m', 'new_m_mix_norm': 'new_m', 'new_m_w_in': 'new_m', 'new_m_pool_w': 'new_m', 'new_m_pool_scale': 'new_m', 'new_m_conv_w': 'new_m', 'new_m_conv_b': 'new_m', 'new_m_dt_bias': 'new_m', 'new_m_a_log': 'new_m', 'new_m_d_skip': 'new_m', 'new_m_ssd_norm': 'new_m', 'new_m_w_out': 'new_m', 'new_m_ffn2_norm': 'new_m', 'new_m_ffn2_w_gate': 'new_m', 'new_m_ffn2_w_up': 'new_m', 'new_m_ffn2_w_down': 'new_m', 'new_m_final_norm': 'new_m', 'new_v_ffn1_norm': 'new_v', 'new_v_ffn1_w_gate': 'new_v', 'new_v_ffn1_w_up': 'new_v', 'new_v_ffn1_w_down': 'new_v', 'new_v_mix_norm': 'new_v', 'new_v_w_in': 'new_v', 'new_v_pool_w': 'new_v', 'new_v_pool_scale': 'new_v', 'new_v_conv_w': 'new_v', 'new_v_conv_b': 'new_v', 'new_v_dt_bias': 'new_v', 'new_v_a_log': 'new_v', 'new_v_d_skip': 'new_v', 'new_v_ssd_norm': 'new_v', 'new_v_w_out': 'new_v', 'new_v_ffn2_norm': 'new_v', 'new_v_ffn2_w_gate': 'new_v', 'new_v_ffn2_w_up': 'new_v', 'new_v_ffn2_w_down': 'new_v', 'new_v_final_norm': 'new_v'}


def _forward(args):
    return _fwd_reference(*[args[k] for k in FWD_PARAMS])


def _output_shape():
    def fwd():
        inp = _fwd_setup_inputs(0)
        return _fwd_reference(*[inp[k] for k in FWD_PARAMS])
    out = _jax.eval_shape(fwd)
    return out.shape, out.dtype

N_MICROBATCH = 1
ADAM_LR = 0.001
ADAM_B1 = 0.9
ADAM_B2 = 0.999
ADAM_EPS = 1e-08
ADAM_WD = 0.01
ADAM_STEP = 10
PER_EXAMPLE_BATCH_AXIS = {'x': 0, 'loss_target': 0}
SHARED_INPUTS = []
_WEIGHT_DTYPES = {'ffn1_norm': _jnp.float32, 'ffn1_w_gate': _jnp.float32, 'ffn1_w_up': _jnp.float32, 'ffn1_w_down': _jnp.float32, 'mix_norm': _jnp.float32, 'w_in': _jnp.float32, 'pool_w': _jnp.float32, 'pool_scale': _jnp.float32, 'conv_w': _jnp.float32, 'conv_b': _jnp.float32, 'dt_bias': _jnp.float32, 'a_log': _jnp.float32, 'd_skip': _jnp.float32, 'ssd_norm': _jnp.float32, 'w_out': _jnp.float32, 'ffn2_norm': _jnp.float32, 'ffn2_w_gate': _jnp.float32, 'ffn2_w_up': _jnp.float32, 'ffn2_w_down': _jnp.float32, 'final_norm': _jnp.float32}
MOMENT_SCALE = {'ffn1_norm': 5.671013e-02, 'ffn1_w_gate': 2.427646e-02, 'ffn1_w_up': 2.351192e-02, 'ffn1_w_down': 3.897798e-02, 'mix_norm': 1.147675e-01, 'w_in': 7.548312e-02, 'pool_w': 8.007878e-02, 'pool_scale': 7.984125e-02, 'conv_w': 8.164102e-02, 'conv_b': 1.074189e-01, 'dt_bias': 1.657253e-01, 'a_log': 4.224457e-01, 'd_skip': 5.549607e-01, 'ssd_norm': 9.388436e-02, 'w_out': 8.241010e-02, 'ffn2_norm': 3.856953e-02, 'ffn2_w_gate': 1.654632e-02, 'ffn2_w_up': 1.602834e-02, 'ffn2_w_down': 2.658063e-02, 'final_norm': 3.202118e+01}


def _to_microbatches(a, axis):
    t = _jnp.moveaxis(a, axis, 0)
    t = t.reshape((N_MICROBATCH, t.shape[0] // N_MICROBATCH) + t.shape[1:])
    return _jnp.moveaxis(t, 1, axis + 1)


def setup_inputs(seed: int = 0) -> dict:
    inp = _fwd_setup_inputs(seed)
    key = _jax.random.fold_in(_jax.random.key(seed), 7919)
    shape, _ = _output_shape()
    out = dict(inp)
    out["loss_target"] = _jax.random.normal(_jax.random.fold_in(key, 0), shape, _jnp.float32)
    for i, name in enumerate(TWIN_WEIGHTS):
        w = inp[name].astype(_jnp.float32)
        if MOMENT_SCALE is None:
            s = _jnp.sqrt(_jnp.mean(_jnp.square(w)) + 1e-30)
        else:
            s = MOMENT_SCALE[name]
        km, kv = _jax.random.split(_jax.random.fold_in(key, i + 1))
        out[name] = w
        out["m_" + name] = s * _jax.random.normal(km, w.shape, _jnp.float32)
        out["v_" + name] = (s * s) * _jax.random.uniform(kv, w.shape, _jnp.float32, 0.5, 1.5)
    if N_MICROBATCH > 1:
        for name, axis in PER_EXAMPLE_BATCH_AXIS.items():
            out[name] = _to_microbatches(out[name], axis)
    return {'x': out['x'], 'ffn1_norm': out['ffn1_norm'], 'ffn1_w_gate': out['ffn1_w_gate'], 'ffn1_w_up': out['ffn1_w_up'], 'ffn1_w_down': out['ffn1_w_down'], 'mix_norm': out['mix_norm'], 'w_in': out['w_in'], 'pool_w': out['pool_w'], 'pool_scale': out['pool_scale'], 'conv_w': out['conv_w'], 'conv_b': out['conv_b'], 'dt_bias': out['dt_bias'], 'a_log': out['a_log'], 'd_skip': out['d_skip'], 'ssd_norm': out['ssd_norm'], 'w_out': out['w_out'], 'ffn2_norm': out['ffn2_norm'], 'ffn2_w_gate': out['ffn2_w_gate'], 'ffn2_w_up': out['ffn2_w_up'], 'ffn2_w_down': out['ffn2_w_down'], 'final_norm': out['final_norm'], 'loss_target': out['loss_target'], 'm_ffn1_norm': out['m_ffn1_norm'], 'm_ffn1_w_gate': out['m_ffn1_w_gate'], 'm_ffn1_w_up': out['m_ffn1_w_up'], 'm_ffn1_w_down': out['m_ffn1_w_down'], 'm_mix_norm': out['m_mix_norm'], 'm_w_in': out['m_w_in'], 'm_pool_w': out['m_pool_w'], 'm_pool_scale': out['m_pool_scale'], 'm_conv_w': out['m_conv_w'], 'm_conv_b': out['m_conv_b'], 'm_dt_bias': out['m_dt_bias'], 'm_a_log': out['m_a_log'], 'm_d_skip': out['m_d_skip'], 'm_ssd_norm': out['m_ssd_norm'], 'm_w_out': out['m_w_out'], 'm_ffn2_norm': out['m_ffn2_norm'], 'm_ffn2_w_gate': out['m_ffn2_w_gate'], 'm_ffn2_w_up': out['m_ffn2_w_up'], 'm_ffn2_w_down': out['m_ffn2_w_down'], 'm_final_norm': out['m_final_norm'], 'v_ffn1_norm': out['v_ffn1_norm'], 'v_ffn1_w_gate': out['v_ffn1_w_gate'], 'v_ffn1_w_up': out['v_ffn1_w_up'], 'v_ffn1_w_down': out['v_ffn1_w_down'], 'v_mix_norm': out['v_mix_norm'], 'v_w_in': out['v_w_in'], 'v_pool_w': out['v_pool_w'], 'v_pool_scale': out['v_pool_scale'], 'v_conv_w': out['v_conv_w'], 'v_conv_b': out['v_conv_b'], 'v_dt_bias': out['v_dt_bias'], 'v_a_log': out['v_a_log'], 'v_d_skip': out['v_d_skip'], 'v_ssd_norm': out['v_ssd_norm'], 'v_w_out': out['v_w_out'], 'v_ffn2_norm': out['v_ffn2_norm'], 'v_ffn2_w_gate': out['v_ffn2_w_gate'], 'v_ffn2_w_up': out['v_ffn2_w_up'], 'v_ffn2_w_down': out['v_ffn2_w_down'], 'v_final_norm': out['v_final_norm']}


def _loss(weights, diff, rest, loss_target):
    with _jax.named_scope("forward"):
        args = {**rest, TWIN_DIFF_INPUT: diff, **{k: w.astype(_WEIGHT_DTYPES[k]) for k, w in weights.items()}}
        y = _forward(args)
    with _jax.named_scope("loss_head"):
        err = _jnp.square(y.astype(_jnp.float32) - loss_target)
        return 0.5 * _jnp.sum(_jnp.mean(err, axis=-1)) if err.ndim else 0.5 * err


def _adamw(w, g, m, v):
    m = ADAM_B1 * m + (1.0 - ADAM_B1) * g
    v = ADAM_B2 * v + (1.0 - ADAM_B2) * _jnp.square(g)
    m_hat = m / (1.0 - ADAM_B1 ** ADAM_STEP)
    v_hat = v / (1.0 - ADAM_B2 ** ADAM_STEP)
    delta = -ADAM_LR * (m_hat / (_jnp.sqrt(v_hat) + ADAM_EPS) + ADAM_WD * w)
    return delta, m, v


def reference(x, ffn1_norm, ffn1_w_gate, ffn1_w_up, ffn1_w_down, mix_norm, w_in, pool_w, pool_scale, conv_w, conv_b, dt_bias, a_log, d_skip, ssd_norm, w_out, ffn2_norm, ffn2_w_gate, ffn2_w_up, ffn2_w_down, final_norm, loss_target, m_ffn1_norm, m_ffn1_w_gate, m_ffn1_w_up, m_ffn1_w_down, m_mix_norm, m_w_in, m_pool_w, m_pool_scale, m_conv_w, m_conv_b, m_dt_bias, m_a_log, m_d_skip, m_ssd_norm, m_w_out, m_ffn2_norm, m_ffn2_w_gate, m_ffn2_w_up, m_ffn2_w_down, m_final_norm, v_ffn1_norm, v_ffn1_w_gate, v_ffn1_w_up, v_ffn1_w_down, v_mix_norm, v_w_in, v_pool_w, v_pool_scale, v_conv_w, v_conv_b, v_dt_bias, v_a_log, v_d_skip, v_ssd_norm, v_w_out, v_ffn2_norm, v_ffn2_w_gate, v_ffn2_w_up, v_ffn2_w_down, v_final_norm):
    given = dict(x=x, ffn1_norm=ffn1_norm, ffn1_w_gate=ffn1_w_gate, ffn1_w_up=ffn1_w_up, ffn1_w_down=ffn1_w_down, mix_norm=mix_norm, w_in=w_in, pool_w=pool_w, pool_scale=pool_scale, conv_w=conv_w, conv_b=conv_b, dt_bias=dt_bias, a_log=a_log, d_skip=d_skip, ssd_norm=ssd_norm, w_out=w_out, ffn2_norm=ffn2_norm, ffn2_w_gate=ffn2_w_gate, ffn2_w_up=ffn2_w_up, ffn2_w_down=ffn2_w_down, final_norm=final_norm, loss_target=loss_target, m_ffn1_norm=m_ffn1_norm, m_ffn1_w_gate=m_ffn1_w_gate, m_ffn1_w_up=m_ffn1_w_up, m_ffn1_w_down=m_ffn1_w_down, m_mix_norm=m_mix_norm, m_w_in=m_w_in, m_pool_w=m_pool_w, m_pool_scale=m_pool_scale, m_conv_w=m_conv_w, m_conv_b=m_conv_b, m_dt_bias=m_dt_bias, m_a_log=m_a_log, m_d_skip=m_d_skip, m_ssd_norm=m_ssd_norm, m_w_out=m_w_out, m_ffn2_norm=m_ffn2_norm, m_ffn2_w_gate=m_ffn2_w_gate, m_ffn2_w_up=m_ffn2_w_up, m_ffn2_w_down=m_ffn2_w_down, m_final_norm=m_final_norm, v_ffn1_norm=v_ffn1_norm, v_ffn1_w_gate=v_ffn1_w_gate, v_ffn1_w_up=v_ffn1_w_up, v_ffn1_w_down=v_ffn1_w_down, v_mix_norm=v_mix_norm, v_w_in=v_w_in, v_pool_w=v_pool_w, v_pool_scale=v_pool_scale, v_conv_w=v_conv_w, v_conv_b=v_conv_b, v_dt_bias=v_dt_bias, v_a_log=v_a_log, v_d_skip=v_d_skip, v_ssd_norm=v_ssd_norm, v_w_out=v_w_out, v_ffn2_norm=v_ffn2_norm, v_ffn2_w_gate=v_ffn2_w_gate, v_ffn2_w_up=v_ffn2_w_up, v_ffn2_w_down=v_ffn2_w_down, v_final_norm=v_final_norm)
    weights = {n: given[n] for n in TWIN_WEIGHTS}
    shared = {n: given[n] for n in SHARED_INPUTS}
    per_example = {n: given[n] for n in ['x']}
    grad_fn = _jax.value_and_grad(_loss, argnums=(0, 1))

    def one_microbatch(ex, loss_target):
        ex = dict(ex)
        diff = ex.pop(TWIN_DIFF_INPUT)
        return grad_fn(weights, diff, {**shared, **ex}, loss_target)

    if N_MICROBATCH == 1:
        loss, (grad_w, grad_x) = one_microbatch(per_example, given["loss_target"])
    else:
        def body(carry, xs):
            loss_sum, grad_sum = carry
            l_k, (gw_k, gx_k) = one_microbatch(xs[0], xs[1])
            with _jax.named_scope("update"):
                return (loss_sum + l_k, _jax.tree.map(_jnp.add, grad_sum, gw_k)), gx_k

        init = (_jnp.zeros((), _jnp.float32), _jax.tree.map(_jnp.zeros_like, weights))
        (loss, grad_w), grad_x = _jax.lax.scan(body, init, (per_example, given["loss_target"]))
    with _jax.named_scope("update"):
        delta_w, new_m, new_v = {}, {}, {}
        for n in TWIN_WEIGHTS:
            delta_w[n], new_m[n], new_v[n] = _adamw(weights[n], grad_w[n], given["m_" + n], given["v_" + n])
    return (loss, grad_x, *[grad_w[n] for n in TWIN_WEIGHTS], *[delta_w[n] for n in TWIN_WEIGHTS],
            *[new_m[n] for n in TWIN_WEIGHTS], *[new_v[n] for n in TWIN_WEIGHTS])
```

```python
import functools
import math

import jax
import jax.numpy as jnp
from jax import lax
from jax.experimental import pallas as pl
from jax.experimental.pallas import tpu as pltpu

F32 = jnp.float32
BF16 = jnp.bfloat16
MESH_ID = pl.DeviceIdType.MESH

RMS_EPS = 1e-6
POOL_WINDOWS = (2, 4, 8, 16)
LANES = 128
HEAD_DIM = 64
SSD_HEADS = 16
SSD_CHUNK = 256
ATTN_BLOCK = 128
HALO = 16
EXP_UNDERFLOW = -105.0
VMEM_LIMIT = 56 * 1024 * 1024

ADAM_LR = 0.001
ADAM_B1 = 0.9
ADAM_B2 = 0.999
ADAM_EPS = 1e-08
ADAM_WD = 0.01
ADAM_STEP = 10

C_POOL, C_Z, C_XBC, C_Q, C_K, C_V, C_END = 0, 512, 1536, 3072, 3584, 4096, 4608


def _cparams(sem):
    return pltpu.CompilerParams(dimension_semantics=sem, vmem_limit_bytes=VMEM_LIMIT)


def _tile(dim, pref, unit=LANES):
    if dim <= pref:
        return dim
    t = (pref // unit) * unit
    while t > unit and dim % t:
        t -= unit
    assert dim % t == 0, (dim, pref)
    return t


def _silu(x):
    return x * jax.nn.sigmoid(x)


def _dsilu(x):
    s = jax.nn.sigmoid(x)
    return s * (1.0 + x * (1.0 - s))


def _dot(a, b):
    return jnp.dot(a.astype(BF16), b.astype(BF16), preferred_element_type=F32)


def _dot_nt(a, b):
    return lax.dot_general(a.astype(BF16), b.astype(BF16), (((1,), (1,)), ((), ())), preferred_element_type=F32)


def _split3(x):
    hi = x.astype(BF16)
    r = x - hi.astype(F32)
    mid = r.astype(BF16)
    lo = (r - mid.astype(F32)).astype(BF16)
    return hi, mid, lo


def _dot3(x, m):
    hi, mid, lo = _split3(x)
    dn = (((1,), (0,)), ((), ()))
    f = lambda p: lax.dot_general(p, m, dn, preferred_element_type=F32)
    return f(hi) + f(mid) + f(lo)


def _dot3_left(m, x):
    hi, mid, lo = _split3(x)
    dn = (((1,), (0,)), ((), ()))
    f = lambda p: lax.dot_general(m, p, dn, preferred_element_type=F32)
    return f(hi) + f(mid) + f(lo)


def _iota(shape, axis):
    return lax.broadcasted_iota(jnp.int32, shape, axis)


def _col(x, h):
    return jnp.sum(jnp.where(_iota(x.shape, 1) == h, x, 0.0), axis=1, keepdims=True)


def _roll_down(x, k):
    return x if k == 0 else pltpu.roll(x, k, 0)


def _roll_up(x, k):
    return x if k == 0 else pltpu.roll(x, x.shape[0] - k, 0)


def _mm(name, a_list, b_list, *, nt, epilogue, out_dtypes, acc_of=None, extras=(), stack=None, tm=1024, tn=512,
        tk=1024):
    n_pairs = len(a_list)
    acc_of = list(acc_of) if acc_of is not None else [0] * n_pairs
    n_acc = max(acc_of) + 1
    m_dim, k_dim = a_list[0].shape
    n_dim = b_list[0].shape[0] if nt else b_list[0].shape[1]
    tm, tn, tk = _tile(m_dim, tm, 8), _tile(n_dim, tn), _tile(k_dim, tk)
    nk = k_dim // tk
    n_ex, n_out = len(extras), len(out_dtypes)
    n_buf = 0 if stack is None else len(stack[0])

    def body(*refs):
        a_refs = refs[:n_pairs]
        b_refs = refs[n_pairs:2 * n_pairs]
        e_refs = refs[2 * n_pairs:2 * n_pairs + n_ex]
        first_out = 2 * n_pairs + n_ex + n_buf
        o_refs = refs[first_out:first_out + n_out]
        acc_refs = refs[first_out + n_out:]

        def partial(p):
            a, b = a_refs[p][...], b_refs[p][...]
            return _dot_nt(a, b) if nt else _dot(a, b)

        def finish(accs):
            outs = epilogue(accs, [e[...] for e in e_refs])
            for o_ref, o in zip(o_refs, outs):
                o_ref[...] = o.astype(o_ref.dtype)

        if nk == 1:
            accs = [None] * n_acc
            for p in range(n_pairs):
                d = partial(p)
                accs[acc_of[p]] = d if accs[acc_of[p]] is None else accs[acc_of[p]] + d
            finish(accs)
        else:
            k = pl.program_id(2)

            @pl.when(k == 0)
            def _():
                for acc in acc_refs:
                    acc[...] = jnp.zeros_like(acc)

            for p in range(n_pairs):
                acc_refs[acc_of[p]][...] += partial(p)

            @pl.when(k == nk - 1)
            def _():
                finish([acc[...] for acc in acc_refs])

    a_spec = pl.BlockSpec((tm, tk), lambda i, j, k: (i, k))
    b_spec = pl.BlockSpec((tn, tk), lambda i, j, k: (j, k)) if nt else pl.BlockSpec((tk, tn), lambda i, j, k: (k, j))
    t_spec = pl.BlockSpec((tm, tn), lambda i, j, k: (i, j))
    n_in = 2 * n_pairs + n_ex
    if stack is None:
        out_specs = [t_spec] * n_out
        out_shape = [jax.ShapeDtypeStruct((m_dim, n_dim), dt) for dt in out_dtypes]
        bufs, aliases = [], {}
    else:
        bufs, slot = stack
        out_specs = [pl.BlockSpec((None, tm, tn), lambda i, j, k: (slot, i, j))] * n_out
        out_shape = [jax.ShapeDtypeStruct(b.shape, b.dtype) for b in bufs]
        aliases = {n_in + o: o for o in range(n_out)}
    return pl.pallas_call(
        body, name=name, grid=(m_dim // tm, n_dim // tn, nk),
        in_specs=[a_spec] * n_pairs + [b_spec] * n_pairs + [t_spec] * n_ex + [pl.BlockSpec(memory_space=pl.ANY)] * len(bufs),
        out_specs=out_specs, out_shape=out_shape, input_output_aliases=aliases,
        scratch_shapes=[] if nk == 1 else [pltpu.VMEM((tm, tn), F32)] * n_acc,
        compiler_params=_cparams(("parallel", "parallel", "arbitrary")),
    )(*a_list, *b_list, *extras, *bufs)


def _rms_fwd(name, h, g):
    s_len, d = h.shape
    ts = _tile(s_len, 512, 8)

    def body(h_ref, g_ref, u_ref):
        x = h_ref[...]
        rstd = lax.rsqrt(jnp.mean(x * x, axis=-1, keepdims=True) + RMS_EPS)
        u_ref[...] = (x * rstd * g_ref[...]).astype(BF16)

    return pl.pallas_call(
        body, name=name, grid=(s_len // ts,),
        in_specs=[pl.BlockSpec((ts, d), lambda i: (i, 0)), pl.BlockSpec((1, d), lambda i: (0, 0))],
        out_specs=pl.BlockSpec((ts, d), lambda i: (i, 0)),
        out_shape=jax.ShapeDtypeStruct((s_len, d), BF16),
        compiler_params=_cparams(("parallel",)),
    )(h, g)


def _rms_bwd(name, h, du, dres, g):
    s_len, d = h.shape
    ts = _tile(s_len, 256, 8)

    def body(h_ref, du_ref, dres_ref, g_ref, dh_ref, dhb_ref, dg_ref):
        x = h_ref[...]
        rstd = lax.rsqrt(jnp.mean(x * x, axis=-1, keepdims=True) + RMS_EPS)
        n = x * rstd
        dn = du_ref[...] * g_ref[...]
        dh = dres_ref[...] + rstd * (dn - n * jnp.mean(dn * n, axis=-1, keepdims=True))
        dh_ref[...] = dh
        dhb_ref[...] = dh.astype(BF16)

        @pl.when(pl.program_id(0) == 0)
        def _():
            dg_ref[...] = jnp.zeros_like(dg_ref)

        dg_ref[...] += jnp.sum(du_ref[...] * n, axis=0, keepdims=True)

    row = pl.BlockSpec((ts, d), lambda i: (i, 0))
    vec = pl.BlockSpec((1, d), lambda i: (0, 0))
    return pl.pallas_call(
        body, name=name, grid=(s_len // ts,),
        in_specs=[row, row, row, vec], out_specs=[row, row, vec],
        out_shape=[jax.ShapeDtypeStruct((s_len, d), F32), jax.ShapeDtypeStruct((s_len, d), BF16),
                   jax.ShapeDtypeStruct((1, d), F32)],
        compiler_params=_cparams(("arbitrary",)),
    )(h, du, dres, g)


def _loss_head(name, h, g, target):
    s_len, d = h.shape
    ts = _tile(s_len, 256, 8)

    def body(h_ref, g_ref, t_ref, loss_ref, dh_ref, dhb_ref, dg_ref):
        x = h_ref[...]
        rstd = lax.rsqrt(jnp.mean(x * x, axis=-1, keepdims=True) + RMS_EPS)
        n = x * rstd
        err = n * g_ref[...] - t_ref[...]
        dy = err * (1.0 / d)
        dn = dy * g_ref[...]
        dh = rstd * (dn - n * jnp.mean(dn * n, axis=-1, keepdims=True))
        dh_ref[...] = dh
        dhb_ref[...] = dh.astype(BF16)

        @pl.when(pl.program_id(0) == 0)
        def _():
            dg_ref[...] = jnp.zeros_like(dg_ref)
            loss_ref[...] = jnp.zeros_like(loss_ref)

        dg_ref[...] += jnp.sum(dy * n, axis=0, keepdims=True)
        part = jnp.sum(jnp.sum(err * err, axis=1, keepdims=True), axis=0, keepdims=True) * (0.5 / d)
        loss_ref[...] += jnp.broadcast_to(part, loss_ref.shape)

    row = pl.BlockSpec((ts, d), lambda i: (i, 0))
    vec = pl.BlockSpec((1, d), lambda i: (0, 0))
    lspec = pl.BlockSpec((1, LANES), lambda i: (0, 0))
    return pl.pallas_call(
        body, name=name, grid=(s_len // ts,),
        in_specs=[row, vec, row], out_specs=[lspec, row, row, vec],
        out_shape=[jax.ShapeDtypeStruct((1, LANES), F32), jax.ShapeDtypeStruct((s_len, d), F32),
                   jax.ShapeDtypeStruct((s_len, d), BF16), jax.ShapeDtypeStruct((1, d), F32)],
        compiler_params=_cparams(("arbitrary",)),
    )(h, g, target)


def _ffn_fwd(tag, h, g, wg, wu, wd):
    u = _rms_fwd(tag + "_norm", h, g)

    def up(accs, _):
        a, b = accs
        return a, b, _silu(a) * b

    a, b, hm = _mm(tag + "_up", [u, u], [wg, wu], nt=False, acc_of=[0, 1], epilogue=up, out_dtypes=[BF16, BF16, BF16])
    (h2,) = _mm(tag + "_down", [hm], [wd], nt=False, extras=[h], epilogue=lambda accs, ex: [ex[0] + 0.5 * accs[0]],
                out_dtypes=[F32], tm=1024, tn=1024, tk=1408)
    return h2, (h, u, a, b, hm)


def _ffn_bwd(tag, saved, g, wg, wu, wd, dh2, dh2_bf, stack=None):
    h, u, a, b, hm = saved
    st_gu = None if stack is None else (stack[0][:2], stack[1])
    st_d = None if stack is None else (stack[0][2:], stack[1])

    def dact(accs, ex):
        af, bf = ex[0].astype(F32), ex[1].astype(F32)
        dhm = 0.5 * accs[0]
        return dhm * bf * _dsilu(af), dhm * _silu(af)

    da, db = _mm(tag + "_dhm", [dh2_bf], [wd], nt=True, extras=[a, b], epilogue=dact, out_dtypes=[BF16, BF16], tk=2048)
    (dwd,) = _mm(tag + "_dwd", [hm.T], [dh2_bf], nt=False, epilogue=lambda accs, _: [0.5 * accs[0]], out_dtypes=[F32], tm=512,
                 stack=st_d)
    ut = u.T
    dwg, dwu = _mm(tag + "_dwgu", [ut, ut], [da, db], nt=False, acc_of=[0, 1], epilogue=lambda accs, _: accs,
                   out_dtypes=[F32, F32], stack=st_gu)
    (du,) = _mm(tag + "_du", [da, db], [wg, wu], nt=True, epilogue=lambda accs, _: accs, out_dtypes=[F32],
                tm=1024, tn=1024, tk=1408)
    dh, dh_bf, dg = _rms_bwd(tag + "_dnorm", h, du, dh2, g)
    return dh, dh_bf, dg, dwg, dwu, dwd


def _softplus(x):
    e = jnp.exp(-jnp.abs(x))
    u = 1.0 + e
    log1p_e = jnp.where(u == 1.0, e, jnp.log(u) * (e / jnp.where(u == 1.0, 1.0, u - 1.0)))
    return jnp.maximum(x, 0.0) + log1p_e


def _row_spec(ts, width, colblock):
    return pl.BlockSpec((ts, width), lambda i: (i, colblock))


def _halo_before_spec(ts, width, colblock):
    r = ts // HALO
    return pl.BlockSpec((HALO, width), lambda i: (jnp.maximum(i * r - 1, 0), colblock))


def _halo_after_spec(ts, width, colblock, s_len):
    r = ts // HALO
    return pl.BlockSpec((HALO, width), lambda i: (jnp.minimum((i + 1) * r, s_len // HALO - 1), colblock))


def _const_spec(shape):
    nd = len(shape)
    return pl.BlockSpec(shape, lambda *_: (0,) * nd)


def _window_sum(e, win, roll):
    s, sh = e, 1
    while sh < win:
        s = s + roll(s, sh)
        sh *= 2
    return s


def _pool_center(ext, x, t, gi, win):
    sl = slice(gi * LANES, (gi + 1) * LANES)
    s = _window_sum(ext[:, sl], win, _roll_down)
    cnt = jnp.minimum(t + 1, win).astype(F32)
    return s[HALO:] / cnt - x[:, sl]


def _pool_fwd(name, proj, pw, scale):
    s_len = proj.shape[0]
    ts = _tile(s_len, 512, 8)
    width = len(POOL_WINDOWS) * LANES

    def body(x_ref, hb_ref, pw_ref, sc_ref, o_ref):
        i = pl.program_id(0)
        x = x_ref[...]
        ext = jnp.concatenate([jnp.where(i == 0, 0.0, hb_ref[...]), x], axis=0)
        t = i * ts + _iota((ts, 1), 0)
        for gi, win in enumerate(POOL_WINDOWS):
            sl = slice(gi * LANES, (gi + 1) * LANES)
            c = _pool_center(ext, x, t, gi, win)
            o_ref[:, sl] = (_dot(c, pw_ref[gi]) * sc_ref[:, sl]).astype(o_ref.dtype)

    return pl.pallas_call(
        body, name=name, grid=(s_len // ts,),
        in_specs=[_row_spec(ts, width, 0), _halo_before_spec(ts, width, 0), _const_spec(pw.shape), _const_spec(scale.shape)],
        out_specs=_row_spec(ts, width, 0),
        out_shape=jax.ShapeDtypeStruct((s_len, width), BF16),
        compiler_params=_cparams(("parallel",)),
    )(proj, proj, pw, scale)


def _pool_bwd(name, proj, dmixed, pw, scale):
    s_len = proj.shape[0]
    ts = _tile(s_len, 512, 8)
    n_tiles = s_len // ts
    width = len(POOL_WINDOWS) * LANES

    def body(x_ref, hb_ref, d_ref, da_ref, pw_ref, sc_ref, dx_ref, dpw_ref, dsc_ref):
        i = pl.program_id(0)
        x = x_ref[...]
        ext = jnp.concatenate([jnp.where(i == 0, 0.0, hb_ref[...]), x], axis=0)
        dout = d_ref[...]
        dext = jnp.concatenate([dout, jnp.where(i == n_tiles - 1, 0.0, da_ref[...])], axis=0)
        t = i * ts + _iota((ts, 1), 0)
        te = i * ts + _iota((ts + HALO, 1), 0)

        @pl.when(i == 0)
        def _():
            dpw_ref[...] = jnp.zeros_like(dpw_ref)
            dsc_ref[...] = jnp.zeros_like(dsc_ref)

        for gi, win in enumerate(POOL_WINDOWS):
            sl = slice(gi * LANES, (gi + 1) * LANES)
            c = _pool_center(ext, x, t, gi, win)
            o = _dot(c, pw_ref[gi])
            dsc_ref[:, sl] += jnp.sum(dout[:, sl] * o, axis=0, keepdims=True)
            do_ext = dext[:, sl] * sc_ref[:, sl]
            dc = _dot_nt(do_ext, pw_ref[gi])
            e = dc / jnp.minimum(te + 1, win).astype(F32)
            back = _window_sum(e, win, _roll_up)
            dx_ref[:, sl] = (back[:ts] - dc[:ts]).astype(dx_ref.dtype)
            dpw_ref[gi] += _dot(c.T, do_ext[:ts])

    return pl.pallas_call(
        body, name=name, grid=(n_tiles,),
        in_specs=[_row_spec(ts, width, 0), _halo_before_spec(ts, width, 0), _row_spec(ts, width, 0),
                  _halo_after_spec(ts, width, 0, s_len), _const_spec(pw.shape), _const_spec(scale.shape)],
        out_specs=[_row_spec(ts, width, 0), _const_spec(pw.shape), _const_spec(scale.shape)],
        out_shape=[jax.ShapeDtypeStruct((s_len, width), BF16), jax.ShapeDtypeStruct(pw.shape, F32),
                   jax.ShapeDtypeStruct(scale.shape, F32)],
        compiler_params=_cparams(("arbitrary",)),
    )(proj, proj, dmixed, dmixed, pw, scale)


def _conv_pre(ext, w_ref, b_ref):
    k_len = w_ref.shape[0]
    y = _roll_down(ext, k_len - 1) * w_ref[0:1, :]
    for k in range(1, k_len):
        y = y + _roll_down(ext, k_len - 1 - k) * w_ref[k:k + 1, :]
    return y + b_ref[...]


def _conv_fwd(name, proj, w, b):
    s_len = proj.shape[0]
    width = w.shape[1]
    ts = _tile(s_len, 512, 8)
    cb = C_XBC // width

    def body(x_ref, hb_ref, w_ref, b_ref, o_ref):
        i = pl.program_id(0)
        ext = jnp.concatenate([jnp.where(i == 0, 0.0, hb_ref[...]), x_ref[...]], axis=0)
        o_ref[...] = _silu(_conv_pre(ext, w_ref, b_ref)[HALO:])

    return pl.pallas_call(
        body, name=name, grid=(s_len // ts,),
        in_specs=[_row_spec(ts, width, cb), _halo_before_spec(ts, width, cb), _const_spec(w.shape), _const_spec(b.shape)],
        out_specs=_row_spec(ts, width, 0),
        out_shape=jax.ShapeDtypeStruct((s_len, width), F32),
        compiler_params=_cparams(("parallel",)),
    )(proj, proj, w, b)


def _conv_bwd(name, proj, dact, w, b):
    s_len = proj.shape[0]
    width = w.shape[1]
    k_len = w.shape[0]
    ts = _tile(s_len, 512, 8)
    n_tiles = s_len // ts
    cb = C_XBC // width

    def body(x_ref, hb_ref, ha_ref, d_ref, da_ref, w_ref, b_ref, dx_ref, dw_ref, db_ref):
        i = pl.program_id(0)
        last = i == n_tiles - 1
        ext = jnp.concatenate([jnp.where(i == 0, 0.0, hb_ref[...]), x_ref[...], jnp.where(last, 0.0, ha_ref[...])], axis=0)
        pre = _conv_pre(ext, w_ref, b_ref)[HALO:]
        dpre = jnp.concatenate([d_ref[...], jnp.where(last, 0.0, da_ref[...])], axis=0) * _dsilu(pre)

        @pl.when(i == 0)
        def _():
            dw_ref[...] = jnp.zeros_like(dw_ref)
            db_ref[...] = jnp.zeros_like(db_ref)

        dx = _roll_up(dpre, k_len - 1) * w_ref[0:1, :]
        for k in range(1, k_len):
            dx = dx + _roll_up(dpre, k_len - 1 - k) * w_ref[k:k + 1, :]
        dx_ref[...] = dx[:ts].astype(dx_ref.dtype)
        dtile = dpre[:ts]
        for k in range(k_len):
            xk = _roll_down(ext, k_len - 1 - k)[HALO:HALO + ts]
            dw_ref[k:k + 1, :] += jnp.sum(dtile * xk, axis=0, keepdims=True)
        db_ref[...] += jnp.sum(dtile, axis=0, keepdims=True)

    return pl.pallas_call(
        body, name=name, grid=(n_tiles,),
        in_specs=[_row_spec(ts, width, cb), _halo_before_spec(ts, width, cb), _halo_after_spec(ts, width, cb, s_len),
                  _row_spec(ts, width, 0), _halo_after_spec(ts, width, 0, s_len), _const_spec(w.shape), _const_spec(b.shape)],
        out_specs=[_row_spec(ts, width, 0), _const_spec(w.shape), _const_spec(b.shape)],
        out_shape=[jax.ShapeDtypeStruct((s_len, width), BF16), jax.ShapeDtypeStruct(w.shape, F32),
                   jax.ShapeDtypeStruct(b.shape, F32)],
        compiler_params=_cparams(("arbitrary",)),
    )(proj, proj, proj, dact, dact, w, b)


def _pair_cols(c0, c1, lo_half):
    return jnp.where(lo_half, c0, c1)


def _ssd_common(dtr_ref, bias_ref, alog_ref, acs_t_ref):
    chunk = dtr_ref.shape[0]
    xpre = dtr_ref[...] + bias_ref[...]
    dt = _softplus(xpre)
    a_neg = -jnp.exp(alog_ref[...])
    tri = _iota((chunk, chunk), 1) <= _iota((chunk, chunk), 0)
    a_cs = _dot3_left(tri.astype(BF16), dt * a_neg)
    acs_t_ref[...] = a_cs.T
    a_last = jnp.sum(jnp.where(_iota(a_cs.shape, 0) == chunk - 1, a_cs, 0.0), axis=0, keepdims=True)
    return xpre, dt, a_neg, tri, a_cs, a_last


def _ssd_specs(chunk, order):
    xs = pl.BlockSpec((chunk, 1024), lambda c: (order(c), 0))
    bm = pl.BlockSpec((chunk, 256), lambda c: (order(c), 4))
    cm = pl.BlockSpec((chunk, 256), lambda c: (order(c), 5))
    lanes = pl.BlockSpec((chunk, LANES), lambda c: (order(c), 0))
    return xs, bm, cm, lanes


def _ssd_fwd(name, xbc, dtr, dt_bias, a_log, d_skip):
    s_len = xbc.shape[0]
    chunk = SSD_CHUNK
    nc = s_len // chunk
    n_pairs = SSD_HEADS // 2

    def body(xs_ref, b_ref, c_ref, dtr_ref, bias_ref, alog_ref, dsk_ref, y_ref, st_ref, state_ref, acs_t_ref):
        @pl.when(pl.program_id(0) == 0)
        def _():
            state_ref[...] = jnp.zeros_like(state_ref)

        _, dt, _, tri, a_cs, a_last = _ssd_common(dtr_ref, bias_ref, alog_ref, acs_t_ref)
        lo_half = _iota((chunk, LANES), 1) < HEAD_DIM
        lo_lane = _iota((1, LANES), 1) < HEAD_DIM
        lo_row = _iota((LANES, 1), 0) < HEAD_DIM
        dsk = dsk_ref[...]
        for g in range(2):
            gsl = slice(g * LANES, (g + 1) * LANES)
            bg, cg = b_ref[:, gsl], c_ref[:, gsl]
            gmat = _dot_nt(cg, bg)
            for pr in range(n_pairs // 2):
                pair = g * (n_pairs // 2) + pr
                h0, h1 = 2 * pair, 2 * pair + 1
                psl = slice(pair * LANES, (pair + 1) * LANES)
                x2 = xs_ref[:, psl]
                acs0, acs1 = _col(a_cs, h0), _col(a_cs, h1)
                xdt = x2 * _pair_cols(_col(dt, h0), _col(dt, h1), lo_half)
                y2 = jnp.zeros((chunk, LANES), F32)
                for h, acs_c, hmask in ((h0, acs0, lo_half), (h1, acs1, ~lo_half)):
                    lam = jnp.where(tri, jnp.exp(jnp.minimum(acs_c - acs_t_ref[h:h + 1, :], 0.0)), 0.0)
                    y2 = y2 + _dot(gmat * lam, jnp.where(hmask, xdt, 0.0))
                s2 = state_ref[pair]
                st_ref[0, pair] = s2
                y2 = y2 + _pair_cols(jnp.exp(acs0), jnp.exp(acs1), lo_half) * _dot_nt(cg, s2)
                y_ref[:, psl] = y2 + _pair_cols(_col(dsk, h0), _col(dsk, h1), lo_lane) * x2
                al0, al1 = _col(a_last, h0), _col(a_last, h1)
                wl2 = _pair_cols(jnp.exp(al0 - acs0), jnp.exp(al1 - acs1), lo_half)
                state_ref[pair] = _pair_cols(jnp.exp(al0), jnp.exp(al1), lo_row) * s2 + _dot((xdt * wl2).T, bg)

    xs, bm, cm, lanes = _ssd_specs(chunk, lambda c: c)
    vec = _const_spec((1, LANES))
    return pl.pallas_call(
        body, name=name, grid=(nc,),
        in_specs=[xs, bm, cm, lanes, vec, vec, vec],
        out_specs=[xs, pl.BlockSpec((1, n_pairs, LANES, LANES), lambda c: (c, 0, 0, 0))],
        out_shape=[jax.ShapeDtypeStruct((s_len, 1024), F32), jax.ShapeDtypeStruct((nc, n_pairs, LANES, LANES), F32)],
        scratch_shapes=[pltpu.VMEM((n_pairs, LANES, LANES), F32), pltpu.VMEM((LANES, chunk), F32)],
        compiler_params=_cparams(("arbitrary",)),
    )(xbc, xbc, xbc, dtr, dt_bias, a_log, d_skip)


def _ssd_bwd(name, xbc, dtr, states, dy, dt_bias, a_log, d_skip):
    s_len = xbc.shape[0]
    chunk = SSD_CHUNK
    nc = s_len // chunk
    n_pairs = SSD_HEADS // 2
    rev = lambda c: nc - 1 - c

    def body(xs_ref, b_ref, c_ref, dtr_ref, dy_ref, sin_ref, bias_ref, alog_ref, dsk_ref,
             dxs_ref, db_ref, dc_ref, ddtr_ref, dbias_ref, dalog_ref, ddsk_ref, dstate_ref, acs_t_ref):
        @pl.when(pl.program_id(0) == 0)
        def _():
            dstate_ref[...] = jnp.zeros_like(dstate_ref)
            dbias_ref[...] = jnp.zeros_like(dbias_ref)
            dalog_ref[...] = jnp.zeros_like(dalog_ref)
            ddsk_ref[...] = jnp.zeros_like(ddsk_ref)

        xpre, dt, a_neg, tri, a_cs, a_last = _ssd_common(dtr_ref, bias_ref, alog_ref, acs_t_ref)
        lane = _iota((chunk, LANES), 1)
        lo_half = lane < HEAD_DIM
        lane1 = _iota((1, LANES), 1)
        lo_lane = lane1 < HEAD_DIM
        lo_row = _iota((LANES, 1), 0) < HEAD_DIM
        head_row = _iota((LANES, chunk), 0)
        sq_row, sq_col = _iota((chunk, chunk), 0), _iota((chunk, chunk), 1)
        before = (sq_row < sq_col).astype(BF16)
        dsk = dsk_ref[...]
        da_rows = jnp.zeros((LANES, chunk), F32)
        yo = jnp.zeros((chunk, LANES), F32)
        to = jnp.zeros((chunk, LANES), F32)
        vs = jnp.zeros((1, LANES), F32)
        ddt = jnp.zeros((chunk, LANES), F32)
        ddsk = jnp.zeros((1, LANES), F32)

        def half_sums(v):
            lo = jnp.sum(jnp.where(lo_half, v, 0.0), axis=1, keepdims=True)
            return lo, jnp.sum(v, axis=1, keepdims=True) - lo

        for g in range(2):
            gsl = slice(g * LANES, (g + 1) * LANES)
            bg, cg = b_ref[:, gsl], c_ref[:, gsl]
            gmat = _dot_nt(cg, bg)
            dgm = jnp.zeros((chunk, chunk), F32)
            dbg = jnp.zeros((chunk, LANES), F32)
            dcg = jnp.zeros((chunk, LANES), F32)
            for pr in range(n_pairs // 2):
                pair = g * (n_pairs // 2) + pr
                h0, h1 = 2 * pair, 2 * pair + 1
                psl = slice(pair * LANES, (pair + 1) * LANES)
                x2, dy2 = xs_ref[:, psl], dy_ref[:, psl]
                acs0, acs1 = _col(a_cs, h0), _col(a_cs, h1)
                dt2 = _pair_cols(_col(dt, h0), _col(dt, h1), lo_half)
                xdt = x2 * dt2
                al0, al1 = _col(a_last, h0), _col(a_last, h1)
                v2 = _pair_cols(jnp.exp(acs0), jnp.exp(acs1), lo_half)
                wl2 = _pair_cols(jnp.exp(al0 - acs0), jnp.exp(al1 - acs1), lo_half)
                s_in, ds2 = sin_ref[0, pair], dstate_ref[pair]
                y_off = v2 * _dot_nt(cg, s_in)
                dx_state = wl2 * _dot_nt(bg, ds2)
                dx2 = dx_state
                for h, acs_c, hmask in ((h0, acs0, lo_half), (h1, acs1, ~lo_half)):
                    lam = jnp.where(tri, jnp.exp(jnp.minimum(acs_c - acs_t_ref[h:h + 1, :], 0.0)), 0.0)
                    m = gmat * lam
                    dyh = jnp.where(hmask, dy2, 0.0)
                    dx2 = dx2 + _dot(m.T, dyh)
                    dml = _dot_nt(dyh, xdt) * lam
                    dgm = dgm + dml
                    crossed = jnp.where(sq_row >= sq_col, _dot3(dml * gmat, before), 0.0)
                    da_rows = jnp.where(head_row == h, jnp.sum(crossed, axis=0, keepdims=True), da_rows)
                vdy = v2 * dy2
                dcg = dcg + _dot(vdy, s_in)
                dbg = dbg + _dot(wl2 * xdt, ds2)
                yo0, yo1 = half_sums(dy2 * y_off)
                yo = jnp.where(lane == h0, yo0, jnp.where(lane == h1, yo1, yo))
                to0, to1 = half_sums(dx_state * xdt)
                to = jnp.where(lane == h0, to0, jnp.where(lane == h1, to1, to))
                prod = jnp.sum(ds2 * s_in, axis=1, keepdims=True)
                e0 = jnp.sum(jnp.where(lo_row, prod, 0.0), axis=0, keepdims=True)
                e1 = jnp.sum(prod, axis=0, keepdims=True) - e0
                vs = jnp.where(lane1 == h0, jnp.exp(al0) * e0, jnp.where(lane1 == h1, jnp.exp(al1) * e1, vs))
                q0, q1 = half_sums(dx2 * x2)
                ddt = jnp.where(lane == h0, q0, jnp.where(lane == h1, q1, ddt))
                dxs_ref[:, psl] = dx2 * dt2 + _pair_cols(_col(dsk, h0), _col(dsk, h1), lo_lane) * dy2
                s0, s1 = half_sums(dy2 * x2)
                ddsk = jnp.where(lane1 == h0, jnp.sum(s0, axis=0, keepdims=True),
                                 jnp.where(lane1 == h1, jnp.sum(s1, axis=0, keepdims=True), ddsk))
                dstate_ref[pair] = _pair_cols(jnp.exp(al0), jnp.exp(al1), lo_row) * ds2 + _dot(vdy.T, cg)
            dc_ref[:, gsl] = dcg + _dot(dgm, bg)
            db_ref[:, gsl] = dbg + _dot(dgm.T, cg)

        da = (da_rows.T + _dot3_left((sq_col >= sq_row).astype(BF16), yo)
              + _dot3_left((sq_col < sq_row).astype(BF16), to) + vs)
        ddt = ddt + da * a_neg
        dalog_ref[...] += jnp.sum(da * dt, axis=0, keepdims=True) * a_neg
        ddtr = jnp.where(lane < SSD_HEADS, ddt * jax.nn.sigmoid(xpre), 0.0)
        ddtr_ref[...] = ddtr
        dbias_ref[...] += jnp.sum(ddtr, axis=0, keepdims=True)
        ddsk_ref[...] += ddsk

    xs, bm, cm, lanes = _ssd_specs(chunk, rev)
    vec = _const_spec((1, LANES))
    st_in = pl.BlockSpec((1, n_pairs, LANES, LANES), lambda c: (rev(c), 0, 0, 0))
    bc_out = pl.BlockSpec((chunk, 256), lambda c: (rev(c), 0))
    return pl.pallas_call(
        body, name=name, grid=(nc,),
        in_specs=[xs, bm, cm, lanes, xs, st_in, vec, vec, vec],
        out_specs=[xs, bc_out, bc_out, lanes, vec, vec, vec],
        out_shape=[jax.ShapeDtypeStruct((s_len, 1024), F32), jax.ShapeDtypeStruct((s_len, 256), F32),
                   jax.ShapeDtypeStruct((s_len, 256), F32), jax.ShapeDtypeStruct((s_len, LANES), F32),
                   jax.ShapeDtypeStruct((1, LANES), F32), jax.ShapeDtypeStruct((1, LANES), F32),
                   jax.ShapeDtypeStruct((1, LANES), F32)],
        scratch_shapes=[pltpu.VMEM((n_pairs, LANES, LANES), F32), pltpu.VMEM((LANES, chunk), F32)],
        compiler_params=_cparams(("arbitrary",)),
    )(xbc, xbc, xbc, dtr, dy, states, dt_bias, a_log, d_skip)


def _gatenorm_fwd(name, y, proj, g):
    s_len = y.shape[0]
    ts = _tile(s_len, 512, 8)
    gw = 512

    def body(y_ref, z_ref, g_ref, o_ref):
        yg = y_ref[...] * _silu(z_ref[...])
        rstd = lax.rsqrt(jnp.mean(yg * yg, axis=-1, keepdims=True) + RMS_EPS)
        o_ref[...] = (yg * rstd * g_ref[...]).astype(o_ref.dtype)

    return pl.pallas_call(
        body, name=name, grid=(2, s_len // ts),
        in_specs=[pl.BlockSpec((ts, gw), lambda gi, i: (i, gi)), pl.BlockSpec((ts, gw), lambda gi, i: (i, C_Z // gw + gi)),
                  pl.BlockSpec((1, gw), lambda gi, i: (0, gi))],
        out_specs=pl.BlockSpec((ts, gw), lambda gi, i: (i, gi)),
        out_shape=jax.ShapeDtypeStruct((s_len, 2 * gw), BF16),
        compiler_params=_cparams(("parallel", "parallel")),
    )(y, proj, g)


def _gatenorm_bwd(name, y, proj, g, dmixed):
    s_len = y.shape[0]
    ts = _tile(s_len, 512, 8)
    gw = 512

    def body(y_ref, z_ref, g_ref, d_ref, dy_ref, dz_ref, dg_ref):
        yv, z = y_ref[...], z_ref[...]
        sz = _silu(z)
        yg = yv * sz
        rstd = lax.rsqrt(jnp.mean(yg * yg, axis=-1, keepdims=True) + RMS_EPS)
        n = yg * rstd
        dn = d_ref[...] * g_ref[...]
        dyg = rstd * (dn - n * jnp.mean(dn * n, axis=-1, keepdims=True))
        dy_ref[...] = dyg * sz
        dz_ref[...] = (dyg * yv * _dsilu(z)).astype(dz_ref.dtype)

        @pl.when(pl.program_id(1) == 0)
        def _():
            dg_ref[...] = jnp.zeros_like(dg_ref)

        dg_ref[...] += jnp.sum(d_ref[...] * n, axis=0, keepdims=True)

    grp = pl.BlockSpec((ts, gw), lambda gi, i: (i, gi))
    vec = pl.BlockSpec((1, gw), lambda gi, i: (0, gi))
    return pl.pallas_call(
        body, name=name, grid=(2, s_len // ts),
        in_specs=[grp, pl.BlockSpec((ts, gw), lambda gi, i: (i, C_Z // gw + gi)), vec,
                  pl.BlockSpec((ts, gw), lambda gi, i: (i, 1 + gi))],
        out_specs=[grp, grp, vec],
        out_shape=[jax.ShapeDtypeStruct((s_len, 2 * gw), F32), jax.ShapeDtypeStruct((s_len, 2 * gw), BF16),
                   jax.ShapeDtypeStruct((1, 2 * gw), F32)],
        compiler_params=_cparams(("parallel", "arbitrary")),
    )(y, proj, g, dmixed)


def _attn_scores(qh, kblk, mask, ustrict, r):
    z = _dot_nt(qh, kblk)
    sp = _softplus(-jnp.abs(z))
    ls = jnp.minimum(z, 0.0) - sp
    lm_raw = jnp.minimum(-z, 0.0) - sp
    lm = jnp.where(mask, lm_raw, 0.0)
    suffix = _dot3(lm, ustrict)
    w = jnp.where(mask, jnp.exp(ls + suffix + r), 0.0)
    return ls, lm_raw, lm, w


def _attn_specs(tq, s_len):
    qcol, kcol, vcol = C_Q // LANES, C_K // LANES, C_V // LANES
    q = pl.BlockSpec((tq, LANES), lambda p, i: (i, qcol + p))
    k = pl.BlockSpec((s_len, LANES), lambda p, i: (0, kcol + p))
    v = pl.BlockSpec((s_len, LANES), lambda p, i: (0, vcol + p))
    return q, k, v


def _attn_fwd(name, proj):
    s_len = proj.shape[0]
    tq = ATTN_BLOCK
    n_slabs = 4

    def body(q_ref, k_ref, v_ref, o_ref):
        qi = pl.program_id(1)
        q2 = q_ref[...] * (HEAD_DIM ** -0.5)
        lo = _iota((tq, LANES), 1) < HEAD_DIM
        heads = ((jnp.where(lo, q2, 0.0).astype(BF16), lo), (jnp.where(lo, 0.0, q2).astype(BF16), ~lo))
        row, col = _iota((tq, tq), 0), _iota((tq, tq), 1)
        ustrict = (row > col).astype(BF16)

        def step(carry):
            kb, _, r0, r1, acc = carry
            rows = pl.ds(pl.multiple_of(kb * tq, tq), tq)
            kblk, vblk = k_ref[rows, :].astype(BF16), v_ref[rows, :]
            mask = (col < row) | (kb < qi)
            new_r = []
            for (qh, hmask), r in zip(heads, (r0, r1)):
                _, _, lm, w = _attn_scores(qh, kblk, mask, ustrict, r)
                acc = acc + _dot(w, jnp.where(hmask, vblk, 0.0))
                new_r.append(r + jnp.sum(lm, axis=1, keepdims=True))
            go = (jnp.maximum(jnp.max(new_r[0]), jnp.max(new_r[1])) > EXP_UNDERFLOW).astype(jnp.int32)
            return kb - 1, go, new_r[0], new_r[1], acc

        zero = jnp.zeros((tq, 1), F32)
        init = (qi, jnp.int32(1), zero, zero, jnp.zeros((tq, LANES), F32))
        o_ref[...] = lax.while_loop(lambda c: (c[0] >= 0) & (c[1] > 0), step, init)[4]

    q, k, v = _attn_specs(tq, s_len)
    return pl.pallas_call(
        body, name=name, grid=(n_slabs, s_len // tq),
        in_specs=[q, k, v], out_specs=pl.BlockSpec((tq, LANES), lambda p, i: (i, p)),
        out_shape=jax.ShapeDtypeStruct((s_len, n_slabs * LANES), F32),
        compiler_params=_cparams(("parallel", "arbitrary")),
    )(proj, proj, proj)


def _attn_bwd(name, proj, dmixed):
    s_len = proj.shape[0]
    tq = ATTN_BLOCK
    n_slabs = 4
    scale = HEAD_DIM ** -0.5

    def body(q_ref, k_ref, v_ref, do_ref, dq_ref, dk_ref, dv_ref, dk_acc, dv_acc, r_hist):
        qi = pl.program_id(1)

        @pl.when(qi == 0)
        def _():
            dk_acc[...] = jnp.zeros_like(dk_acc)
            dv_acc[...] = jnp.zeros_like(dv_acc)

        q2 = q_ref[...] * scale
        do2 = do_ref[...]
        lo = _iota((tq, LANES), 1) < HEAD_DIM
        heads = ((jnp.where(lo, q2, 0.0).astype(BF16), jnp.where(lo, do2, 0.0).astype(BF16), lo),
                 (jnp.where(lo, 0.0, q2).astype(BF16), jnp.where(lo, 0.0, do2).astype(BF16), ~lo))
        row, col = _iota((tq, tq), 0), _iota((tq, tq), 1)
        ustrict = (row > col).astype(BF16)
        earlier = (row < col).astype(BF16)
        zero = jnp.zeros((tq, 1), F32)

        def scan(carry):
            kb, _, r0, r1 = carry
            kblk = k_ref[pl.ds(pl.multiple_of(kb * tq, tq), tq), :].astype(BF16)
            mask = (col < row) | (kb < qi)
            r_hist[kb] = jnp.where(lo, r0, r1)
            new_r = []
            for (qh, _, _), r in zip(heads, (r0, r1)):
                z = _dot_nt(qh, kblk)
                lm = jnp.where(mask, jnp.minimum(-z, 0.0) - _softplus(-jnp.abs(z)), 0.0)
                new_r.append(r + jnp.sum(lm, axis=1, keepdims=True))
            go = (jnp.maximum(jnp.max(new_r[0]), jnp.max(new_r[1])) > EXP_UNDERFLOW).astype(jnp.int32)
            return kb - 1, go, new_r[0], new_r[1]

        first = lax.while_loop(lambda c: (c[0] >= 0) & (c[1] > 0), scan, (qi, jnp.int32(1), zero, zero))[0] + 1

        def step(carry):
            kb, p0, p1, dq = carry
            rows = pl.ds(pl.multiple_of(kb * tq, tq), tq)
            kf, vblk = k_ref[rows, :], v_ref[rows, :].astype(BF16)
            kblk = kf.astype(BF16)
            mask = (col < row) | (kb < qi)
            rr = r_hist[kb]
            dk_blk = jnp.zeros((tq, LANES), F32)
            dv_blk = jnp.zeros((tq, LANES), F32)
            new_p = []
            for (qh, doh, hmask), r, p in zip(heads, (_col(rr, 0), _col(rr, HEAD_DIM)), (p0, p1)):
                ls, lm_raw, _, w = _attn_scores(qh, kblk, mask, ustrict, r)
                ew = _dot_nt(doh, vblk) * w
                before = p + _dot3(ew, earlier)
                dz = jnp.where(mask, ew * jnp.exp(lm_raw) - jnp.exp(ls) * before, 0.0)
                dq = dq + _dot(dz, jnp.where(hmask, kf, 0.0))
                dk_blk = dk_blk + _dot(dz.T, qh)
                dv_blk = dv_blk + _dot(w.T, doh)
                new_p.append(p + jnp.sum(ew, axis=1, keepdims=True))
            dk_acc[rows, :] += dk_blk
            dv_acc[rows, :] += dv_blk
            return kb + 1, new_p[0], new_p[1], dq

        dq = lax.while_loop(lambda c: c[0] <= qi, step, (first, zero, zero, jnp.zeros((tq, LANES), F32)))[3]
        dq_ref[...] = (dq * scale).astype(dq_ref.dtype)

        @pl.when(qi == pl.num_programs(1) - 1)
        def _():
            dk_ref[...] = dk_acc[...].astype(dk_ref.dtype)
            dv_ref[...] = dv_acc[...].astype(dv_ref.dtype)

    q, k, v = _attn_specs(tq, s_len)
    blk = pl.BlockSpec((tq, LANES), lambda p, i: (i, p))
    full = pl.BlockSpec((s_len, LANES), lambda p, i: (0, p))
    shape = jax.ShapeDtypeStruct((s_len, n_slabs * LANES), BF16)
    return pl.pallas_call(
        body, name=name, grid=(n_slabs, s_len // tq),
        in_specs=[q, k, v, pl.BlockSpec((tq, LANES), lambda p, i: (i, 1536 // LANES + p))],
        out_specs=[blk, full, full], out_shape=[shape, shape, shape],
        scratch_shapes=[pltpu.VMEM((s_len, LANES), F32), pltpu.VMEM((s_len, LANES), F32),
                        pltpu.VMEM((s_len // tq, tq, LANES), F32)],
        compiler_params=_cparams(("parallel", "arbitrary")),
    )(proj, proj, proj, dmixed)


def _ident(accs, _):
    return accs


def _mixer_fwd(tag, h, p):
    u = _rms_fwd(tag + "_norm", h, p["mix_norm"])
    (proj,) = _mm(tag + "_in", [u], [p["w_main"]], nt=False, epilogue=_ident, out_dtypes=[F32], tk=2048)
    (dtr,) = _mm(tag + "_indt", [u], [p["w_dt"]], nt=False, epilogue=_ident, out_dtypes=[F32], tk=2048)
    pool_out = _pool_fwd(tag + "_pool", proj, p["pool_w"], p["pool_scale"])
    xbc = _conv_fwd(tag + "_conv", proj, p["conv_w"], p["conv_b"])
    y, states = _ssd_fwd(tag + "_ssd", xbc, dtr, p["dt_bias"], p["a_log"], p["d_skip"])
    ssd_out = _gatenorm_fwd(tag + "_gate", y, proj, p["ssd_norm"])
    attn = _attn_fwd(tag + "_attn", proj)
    mixed = jnp.concatenate([pool_out, ssd_out, attn.astype(BF16)], axis=1)
    (h2,) = _mm(tag + "_out", [mixed], [p["w_out"]], nt=False, extras=[h],
                epilogue=lambda accs, ex: [ex[0] + accs[0]], out_dtypes=[F32], tk=2048)
    return h2, (h, u, proj, dtr, xbc, y, states, mixed)


def _mixer_bwd(tag, saved, p, dh2, dh2_bf, stack=None):
    h, u, proj, dtr, xbc, y, states, mixed = saved
    (dmixed,) = _mm(tag + "_dmix", [dh2_bf], [p["w_out"]], nt=True, epilogue=_ident, out_dtypes=[F32], tk=2048)
    (dw_out,) = _mm(tag + "_dwout", [mixed.T], [dh2_bf], nt=False, epilogue=_ident, out_dtypes=[F32], stack=stack)
    dpool_in, dpool_w, dpool_scale = _pool_bwd(tag + "_dpool", proj, dmixed, p["pool_w"], p["pool_scale"])
    dy, dz, dssd_norm = _gatenorm_bwd(tag + "_dgate", y, proj, p["ssd_norm"], dmixed)
    dxs, dbm, dcm, ddtr, ddt_bias, da_log, dd_skip = _ssd_bwd(tag + "_dssd", xbc, dtr, states, dy, p["dt_bias"],
                                                             p["a_log"], p["d_skip"])
    dxbc, dconv_w, dconv_b = _conv_bwd(tag + "_dconv", proj, jnp.concatenate([dxs, dbm, dcm], axis=1), p["conv_w"],
                                       p["conv_b"])
    dq, dk, dv = _attn_bwd(tag + "_dattn", proj, dmixed)
    dproj = jnp.concatenate([dpool_in, dz, dxbc, dq, dk, dv], axis=1)
    ddtr_bf = ddtr.astype(BF16)
    ut = u.T
    (dw_main,) = _mm(tag + "_dwin", [ut], [dproj], nt=False, epilogue=_ident, out_dtypes=[F32])
    (dw_dt,) = _mm(tag + "_dwdt", [ut], [ddtr_bf], nt=False, epilogue=_ident, out_dtypes=[F32])
    (du_dt,) = _mm(tag + "_dudt", [ddtr_bf], [p["w_dt"]], nt=True, epilogue=_ident, out_dtypes=[F32], tn=1024)
    (du,) = _mm(tag + "_du", [dproj], [p["w_main"]], nt=True, extras=[du_dt],
                epilogue=lambda accs, ex: [accs[0] + ex[0]], out_dtypes=[F32], tm=1024, tn=1024, tk=1536)
    dh, dh_bf, dg = _rms_bwd(tag + "_dnorm", h, du, dh2, p["mix_norm"])
    grads = dict(mix_norm=dg, w_main=dw_main, w_dt=dw_dt, pool_w=dpool_w, pool_scale=dpool_scale, conv_w=dconv_w,
                 conv_b=dconv_b, dt_bias=ddt_bias, a_log=da_log, d_skip=dd_skip, ssd_norm=dssd_norm, w_out=dw_out)
    return dh, dh_bf, grads


def _axes():
    return lax.axis_index("x"), lax.axis_index("y"), lax.axis_index("c")


def _any_specs(n):
    return [pl.BlockSpec(memory_space=pl.ANY) for _ in range(n)]


def _remote(src, dst, send, recv, k, dev):
    return pltpu.make_async_remote_copy(src_ref=src, dst_ref=dst, send_sem=send.at[k], recv_sem=recv.at[k],
                                        device_id=dev, device_id_type=MESH_ID)


def _gather_chips(name, arrs):
    n = len(arrs)

    def body(*refs):
        srcs, outs = refs[:n], refs[n:2 * n]
        send, recv, loc = refs[2 * n:]
        x, y, c = _axes()
        me = 2 * x + y
        peers = [(1 - x, y), (x, 1 - y), (1 - x, 1 - y)]
        local = [pltpu.make_async_copy(srcs[a], outs[a].at[me], loc.at[a]) for a in range(n)]
        for cp in local:
            cp.start()
        sent = [_remote(srcs[a], outs[a].at[me], send, recv, 3 * a + k, (px, py, c))
                for a in range(n) for k, (px, py) in enumerate(peers)]
        for cp in sent:
            cp.start()
        for a in range(n):
            for k, (px, py) in enumerate(peers):
                _remote(srcs[a], outs[a].at[2 * px + py], send, recv, 3 * a + k, (px, py, c)).wait_recv()
        for cp in sent:
            cp.wait_send()
        for cp in local:
            cp.wait()

    return pl.pallas_call(
        body, name=name, in_specs=_any_specs(n), out_specs=_any_specs(n),
        out_shape=[jax.ShapeDtypeStruct((4,) + a.shape, a.dtype) for a in arrs],
        scratch_shapes=[pltpu.SemaphoreType.DMA((3 * n,)), pltpu.SemaphoreType.DMA((3 * n,)), pltpu.SemaphoreType.DMA((n,))],
    )(*arrs)


def _half_view(kind, ref, hc):
    return ref.at[:, pl.ds(2 * hc, 2)] if kind == "win" else ref.at[pl.ds(2 * hc, 2)]


def _half_shape(kind, shape):
    return (shape[0], 2) + tuple(shape[2:]) if kind == "win" else (2,) + tuple(shape[1:])


def _shard_view(kind, ref, j):
    if kind == "col":
        w = ref.shape[2] // 4
        return ref.at[:, :, pl.ds(pl.multiple_of(j * w, LANES), w)]
    if kind == "row":
        r = ref.shape[1] // 4
        return ref.at[:, pl.ds(pl.multiple_of(j * r, 16), r), :]
    return ref.at[j]


def _shard_shape(kind, hshape):
    if kind == "col":
        return (2, hshape[1], hshape[2] // 4)
    if kind == "row":
        return (2, hshape[1] // 4, hshape[2])
    return tuple(hshape[1:])


def _rs_pair(name, kinds, grads):
    n = len(grads)

    def body(*refs):
        g_refs, o_refs = refs[:n], refs[n:2 * n]
        send, recv = refs[2 * n:]
        x, y, c = _axes()
        cps = [_remote(_half_view(kinds[a], g_refs[a], 1 - c), o_refs[a], send, recv, a, (x, y, 1 - c)) for a in range(n)]
        for cp in cps:
            cp.start()
        for cp in cps:
            cp.wait()

    return pl.pallas_call(
        body, name=name, in_specs=_any_specs(n), out_specs=_any_specs(n),
        out_shape=[jax.ShapeDtypeStruct(_half_shape(k, g.shape), g.dtype) for k, g in zip(kinds, grads)],
        scratch_shapes=[pltpu.SemaphoreType.DMA((n,)), pltpu.SemaphoreType.DMA((n,))],
    )(*grads)


def _rs_chips(name, kinds, halves):
    n = len(halves)

    def body(*refs):
        h_refs, o_refs = refs[:n], refs[n:2 * n]
        send, recv = refs[2 * n:]
        x, y, c = _axes()
        peers = [(1 - x, y), (x, 1 - y), (1 - x, 1 - y)]
        cps = [_remote(_shard_view(kinds[a], h_refs[a], 2 * px + py), o_refs[a].at[k], send, recv, 3 * a + k, (px, py, c))
               for a in range(n) for k, (px, py) in enumerate(peers)]
        for cp in cps:
            cp.start()
        for cp in cps:
            cp.wait()

    return pl.pallas_call(
        body, name=name, in_specs=_any_specs(n), out_specs=_any_specs(n),
        out_shape=[jax.ShapeDtypeStruct((3,) + _shard_shape(k, h.shape), h.dtype) for k, h in zip(kinds, halves)],
        scratch_shapes=[pltpu.SemaphoreType.DMA((3 * n,)), pltpu.SemaphoreType.DMA((3 * n,))],
    )(*halves)


def _ag_pair(name, parts):
    n = len(parts)

    def body(*refs):
        t_refs, o_refs = refs[:n], refs[n:2 * n]
        send, recv, loc = refs[2 * n:]
        x, y, c = _axes()
        local = [pltpu.make_async_copy(t_refs[a], o_refs[a].at[pl.ds(2 * c, 2)], loc.at[a]) for a in range(n)]
        sent = [_remote(t_refs[a], o_refs[a].at[pl.ds(2 * c, 2)], send, recv, a, (x, y, 1 - c)) for a in range(n)]
        for cp in local + sent:
            cp.start()
        for a in range(n):
            _remote(t_refs[a], o_refs[a].at[pl.ds(2 * (1 - c), 2)], send, recv, a, (x, y, 1 - c)).wait_recv()
        for cp in sent:
            cp.wait_send()
        for cp in local:
            cp.wait()

    return pl.pallas_call(
        body, name=name, in_specs=_any_specs(n), out_specs=_any_specs(n),
        out_shape=[jax.ShapeDtypeStruct((4,) + t.shape[1:], t.dtype) for t in parts],
        scratch_shapes=[pltpu.SemaphoreType.DMA((n,)), pltpu.SemaphoreType.DMA((n,)), pltpu.SemaphoreType.DMA((n,))],
    )(*parts)


def _esum(name, grid, block, ins, outs, where):
    n_in = len(ins)

    def body(s_ref, *refs):
        tot = refs[0][...].astype(F32)
        for r in refs[1:n_in]:
            tot = tot + r[...].astype(F32)
        for o in refs[n_in:]:
            o[...] = tot.astype(o.dtype)

    spec = lambda nd, imap: pl.BlockSpec((None,) * (nd - 2) + tuple(block), imap)
    return pl.pallas_call(
        body, name=name,
        grid_spec=pltpu.PrefetchScalarGridSpec(
            num_scalar_prefetch=1, grid=grid,
            in_specs=[spec(a.ndim, m) for a, m in ins], out_specs=[spec(len(s), m) for s, _, m in outs]),
        out_shape=[jax.ShapeDtypeStruct(s, dt) for s, dt, _ in outs],
        compiler_params=_cparams(("parallel",) * len(grid)),
    )(where, *[a for a, _ in ins])


def _sum_pair(name, kind, g, r1, where):
    hshape = _half_shape(kind, g.shape)
    if kind == "win":
        block = (_tile(g.shape[2], 512, 16), g.shape[3])
        grid = (g.shape[0], 2, g.shape[2] // block[0])
        gmap = lambda s4, a, i, s: (s4, 2 * s[0] + a, i, 0)
        hmap = lambda s4, a, i, s: (s4, a, i, 0)
    else:
        block = (_tile(g.shape[1], 512, 16), _tile(g.shape[2], 2048))
        grid = (2, g.shape[1] // block[0], g.shape[2] // block[1])
        gmap = lambda a, i, j, s: (2 * s[0] + a, i, j)
        hmap = lambda a, i, j, s: (a, i, j)
    return _esum(name, grid, block, [(g, gmap), (r1, hmap)], [(hshape, F32, hmap), (hshape, BF16, hmap)], where)


def _sum_chips(name, kind, h32, r2, where):
    tshape = _shard_shape(kind, h32.shape)
    if kind == "col":
        block = (_tile(tshape[1], 512, 16), tshape[2])
        hmap = lambda a, i, s: (a, i, s[1])
    elif kind == "row":
        block = (_tile(tshape[1], 512, 16), tshape[2])
        nb = tshape[1] // block[0]
        hmap = lambda a, i, s: (a, s[1] * nb + i, 0)
    else:
        block = (_tile(tshape[1], 512, 16), tshape[2])
        hmap = lambda a, i, s: (s[1], a, i, 0)
    grid = (2, tshape[1] // block[0])
    tmap = lambda a, i, s: (a, i, 0)
    rmap = lambda k: (lambda a, i, s: (k, a, i, 0))
    return _esum(name, grid, block, [(h32, hmap)] + [(r2, rmap(k)) for k in range(3)], [(tshape, F32, tmap)], where)[0]


def _allreduce_small(name, vec):
    rows_n = vec.shape[0]

    def body(x_ref, sum_ref, all_ref, send, recv, local_sem):
        x, y, c = _axes()
        me, sibling = (x, y, c), (x, y, 1 - c)
        chips = [(1 - x, y), (x, 1 - y), (1 - x, 1 - y)]

        def rows(px, py, pc):
            return all_ref.at[pl.ds(pl.multiple_of((4 * px + 2 * py + pc) * rows_n, 8), rows_n), :]

        def copy(k, block, to, src=None):
            return _remote(rows(*block) if src is None else src, rows(*block), send, recv, k, to)

        mine = pltpu.make_async_copy(x_ref, rows(*me), local_sem)
        mine.start()
        first = [copy(0, me, sibling, src=x_ref)] + [copy(1 + j, me, (*chip, c), src=x_ref) for j, chip in enumerate(chips)]
        for cp in first:
            cp.start()
        passed = [copy(4 + j, (*chip, c), sibling) for j, chip in enumerate(chips)]
        for j, chip in enumerate(chips):
            copy(1 + j, (*chip, c), me).wait_recv()
            passed[j].start()
        copy(0, sibling, me).wait_recv()
        for j, chip in enumerate(chips):
            copy(4 + j, (*chip, 1 - c), me).wait_recv()
        for cp in first + passed:
            cp.wait_send()
        mine.wait()
        tot = all_ref[0:rows_n, :]
        for d in range(1, 8):
            tot = tot + all_ref[d * rows_n:(d + 1) * rows_n, :]
        sum_ref[...] = tot

    vm = pl.BlockSpec(memory_space=pltpu.VMEM)
    return pl.pallas_call(
        body, name=name, in_specs=[vm], out_specs=[vm, vm],
        out_shape=[jax.ShapeDtypeStruct(vec.shape, F32), jax.ShapeDtypeStruct((8 * rows_n, LANES), F32)],
        scratch_shapes=[pltpu.SemaphoreType.DMA((7,)), pltpu.SemaphoreType.DMA((7,)), pltpu.SemaphoreType.DMA],
        compiler_params=pltpu.CompilerParams(vmem_limit_bytes=VMEM_LIMIT),
    )(vec)[0]


def _adamw(name, w, g, m, v):
    shape = w.shape
    rows_n, cols = shape[-2], shape[-1]
    lead = math.prod(shape[:-2])
    tr = _tile(rows_n, 256, 8)

    def body(w_ref, g_ref, m_ref, v_ref, d_ref, m2_ref, v2_ref):
        gv = g_ref[...]
        m2 = ADAM_B1 * m_ref[...] + (1.0 - ADAM_B1) * gv
        v2 = ADAM_B2 * v_ref[...] + (1.0 - ADAM_B2) * jnp.square(gv)
        m_hat = m2 / (1.0 - ADAM_B1 ** ADAM_STEP)
        v_hat = v2 / (1.0 - ADAM_B2 ** ADAM_STEP)
        d_ref[...] = -ADAM_LR * (m_hat / (jnp.sqrt(v_hat) + ADAM_EPS) + ADAM_WD * w_ref[...])
        m2_ref[...] = m2
        v2_ref[...] = v2

    spec = pl.BlockSpec((None, tr, cols), lambda l, i: (l, i, 0))
    flat = (lead, rows_n, cols)
    outs = pl.pallas_call(
        body, name=name, grid=(lead, rows_n // tr), in_specs=[spec] * 4, out_specs=[spec] * 3,
        out_shape=[jax.ShapeDtypeStruct(flat, F32)] * 3,
        compiler_params=_cparams(("parallel", "parallel")),
    )(*[t.reshape(flat) for t in (w, g, m, v)])
    return [o.reshape(shape) for o in outs]


WEIGHTS = ("ffn1_norm", "ffn1_w_gate", "ffn1_w_up", "ffn1_w_down", "mix_norm", "w_in", "pool_w", "pool_scale", "conv_w",
           "conv_b", "dt_bias", "a_log", "d_skip", "ssd_norm", "w_out", "ffn2_norm", "ffn2_w_gate", "ffn2_w_up",
           "ffn2_w_down", "final_norm")
BIG = {"ffn1_w_gate": "col", "ffn1_w_up": "col", "ffn1_w_down": "row", "w_in": "win", "w_out": "row",
       "ffn2_w_gate": "col", "ffn2_w_up": "col", "ffn2_w_down": "row"}
SMALL = tuple(n for n in WEIGHTS if n not in BIG and n != "conv_w")
REF_DT = 3072
N_CHIPS = 4


def _pack(parts):
    flat = jnp.concatenate([p.reshape(-1) for p in parts])
    rows_n = -(-flat.shape[0] // (8 * LANES)) * 8
    return jnp.pad(flat, (0, rows_n * LANES - flat.shape[0])).reshape(rows_n, LANES)


def _unpack(block, shapes):
    flat, out, at = block.reshape(-1), [], 0
    for s in shapes:
        n = math.prod(s)
        out.append(flat[at:at + n].reshape(s))
        at += n
    return out


def _train_step(a):
    depth = a["ffn1_norm"].shape[0]
    x_id, y_id, c_id = _axes()
    chip = 2 * x_id + y_id
    where = jnp.stack([c_id, chip]).astype(jnp.int32)

    big = list(BIG)
    gathered = _gather_chips("gather_weights", [a[n].astype(BF16) for n in big] + [a["conv_w"]])
    full = {n: jnp.concatenate([g[s] for s in range(N_CHIPS)], axis=1 if BIG.get(n) == "row" else 2)
            for n, g in zip(big + ["conv_w"], gathered)}
    w_in = full["w_in"]
    w_main = jnp.concatenate([w_in[:, :, :REF_DT], w_in[:, :, REF_DT + SSD_HEADS:]], axis=2)
    w_dt = jnp.pad(w_in[:, :, REF_DT:REF_DT + SSD_HEADS], ((0, 0), (0, 0), (0, LANES - SSD_HEADS)))
    heads128 = lambda v: jnp.pad(v, ((0, 0), (0, LANES - SSD_HEADS)))
    dt_bias, a_log, d_skip = heads128(a["dt_bias"]), heads128(a["a_log"]), heads128(a["d_skip"])

    def mixer_params(l):
        return dict(mix_norm=a["mix_norm"][l][None], w_main=w_main[l], w_dt=w_dt[l], pool_w=a["pool_w"][l],
                    pool_scale=a["pool_scale"][l][None], conv_w=full["conv_w"][l], conv_b=a["conv_b"][l][None],
                    dt_bias=dt_bias[l][None], a_log=a_log[l][None], d_skip=d_skip[l][None],
                    ssd_norm=a["ssd_norm"][l][None], w_out=full["w_out"][l])

    def ffn_params(l, which):
        return (a[which + "_norm"][l][None], full[which + "_w_gate"][l], full[which + "_w_up"][l], full[which + "_w_down"][l])

    h = a["x"][0]
    saved = []
    for l in range(depth):
        h, s1 = _ffn_fwd(f"l{l}_ffn1", h, *ffn_params(l, "ffn1"))
        h, sm = _mixer_fwd(f"l{l}_mix", h, mixer_params(l))
        h, s2 = _ffn_fwd(f"l{l}_ffn2", h, *ffn_params(l, "ffn2"))
        saved.append((s1, sm, s2))
    loss_part, dh, dh_bf, dfinal = _loss_head("loss_head", h, a["final_norm"][None], a["loss_target"][0])

    bufs = {n: lax.empty((depth,) + tuple(full[n].shape[1:]), F32) for n in big if n != "w_in"}
    small = {n: [None] * depth for n in SMALL if n != "final_norm"}
    small["conv_w"] = [None] * depth
    dw_in = [None] * depth
    for l in reversed(range(depth)):
        s1, sm, s2 = saved[l]
        names = ["ffn2_w_gate", "ffn2_w_up", "ffn2_w_down"]
        dh, dh_bf, small["ffn2_norm"][l], *new = _ffn_bwd(f"l{l}_ffn2", s2, *ffn_params(l, "ffn2"), dh, dh_bf,
                                                          stack=([bufs[n] for n in names], l))
        bufs.update(zip(names, new))
        dh, dh_bf, g = _mixer_bwd(f"l{l}_mix", sm, mixer_params(l), dh, dh_bf, stack=([bufs["w_out"]], l))
        bufs["w_out"] = g["w_out"]
        for n in ("mix_norm", "pool_w", "pool_scale", "conv_w", "conv_b", "ssd_norm"):
            small[n][l] = g[n]
        for n in ("dt_bias", "a_log", "d_skip"):
            small[n][l] = g[n][:, :SSD_HEADS]
        dw_in[l] = jnp.concatenate([g["w_main"][:, :REF_DT], g["w_dt"][:, :SSD_HEADS], g["w_main"][:, REF_DT:]], axis=1)
        names = ["ffn1_w_gate", "ffn1_w_up", "ffn1_w_down"]
        dh, dh_bf, small["ffn1_norm"][l], *new = _ffn_bwd(f"l{l}_ffn1", s1, *ffn_params(l, "ffn1"), dh, dh_bf,
                                                          stack=([bufs[n] for n in names], l))
        bufs.update(zip(names, new))
    grad_x = dh[None]

    win = jnp.stack(dw_in)
    win = win.reshape(depth, win.shape[1], N_CHIPS, win.shape[2] // N_CHIPS).transpose(2, 0, 1, 3)
    kinds = [BIG[n] for n in big]
    grads = [win if n == "w_in" else bufs[n] for n in big]
    from_sibling = _rs_pair("reduce_pair", kinds, grads)
    halves = [_sum_pair(f"sum_pair_{n}", k, g, r, where) for n, k, g, r in zip(big, kinds, grads, from_sibling)]
    from_chips = _rs_chips("reduce_chips", kinds, [h16 for _, h16 in halves])
    parts = [_sum_chips(f"sum_chips_{n}", k, h32, r, where) for n, k, (h32, _), r in zip(big, kinds, halves, from_chips)]
    grad = dict(zip(big, _ag_pair("share_pair", parts)))

    small_full = {n: jnp.stack([t.reshape(a[n].shape[1:]) for t in small[n]]) for n in SMALL if n != "final_norm"}
    small_full["final_norm"] = dfinal.reshape(a["final_norm"].shape)
    conv_full = jnp.stack(small["conv_w"])
    shapes = [a[n].shape for n in SMALL] + [conv_full.shape]
    reduced = _unpack(_allreduce_small("allreduce_small", _pack([small_full[n] for n in SMALL] + [conv_full])), shapes)
    grad.update(zip(SMALL, reduced[:-1]))
    shard = a["conv_w"].shape[2]
    grad["conv_w"] = lax.dynamic_slice_in_dim(reduced[-1], chip * shard, shard, axis=2)

    delta, new_m, new_v = {}, {}, {}
    for n in big + ["conv_w"]:
        delta[n], new_m[n], new_v[n] = _adamw(f"adamw_{n}", a[n], grad[n], a["m_" + n], a["v_" + n])
    packed = [_pack([a[pre + n] for n in SMALL]) for pre in ("", "m_", "v_")]
    outs = _adamw("adamw_small", packed[0], _pack([grad[n] for n in SMALL]), packed[1], packed[2])
    for store, block in zip((delta, new_m, new_v), outs):
        store.update(zip(SMALL, _unpack(block, [a[n].shape for n in SMALL])))

    loss = lax.psum(loss_part[0, 0], ("x", "y", "c"))
    return (loss, grad_x, *[grad[n] for n in WEIGHTS], *[delta[n] for n in WEIGHTS], *[new_m[n] for n in WEIGHTS],
            *[new_v[n] for n in WEIGHTS])


def kernel(x, ffn1_norm, ffn1_w_gate, ffn1_w_up, ffn1_w_down, mix_norm, w_in, pool_w, pool_scale, conv_w, conv_b, dt_bias, a_log, d_skip, ssd_norm, w_out, ffn2_norm, ffn2_w_gate, ffn2_w_up, ffn2_w_down, final_norm, loss_target, m_ffn1_norm, m_ffn1_w_gate, m_ffn1_w_up, m_ffn1_w_down, m_mix_norm, m_w_in, m_pool_w, m_pool_scale, m_conv_w, m_conv_b, m_dt_bias, m_a_log, m_d_skip, m_ssd_norm, m_w_out, m_ffn2_norm, m_ffn2_w_gate, m_ffn2_w_up, m_ffn2_w_down, m_final_norm, v_ffn1_norm, v_ffn1_w_gate, v_ffn1_w_up, v_ffn1_w_down, v_mix_norm, v_w_in, v_pool_w, v_pool_scale, v_conv_w, v_conv_b, v_dt_bias, v_a_log, v_d_skip, v_ssd_norm, v_w_out, v_ffn2_norm, v_ffn2_w_gate, v_ffn2_w_up, v_ffn2_w_down, v_final_norm):
    return _train_step(dict(locals()))
```

```python
import functools
import math

import jax
import jax.numpy as jnp
from jax import lax
from jax.experimental import pallas as pl
from jax.experimental.pallas import tpu as pltpu

F32 = jnp.float32
BF16 = jnp.bfloat16
MESH_ID = pl.DeviceIdType.MESH

RMS_EPS = 1e-6
POOL_WINDOWS = (2, 4, 8, 16)
LANES = 128
HEAD_DIM = 64
SSD_HEADS = 16
SSD_CHUNK = 256
ATTN_BLOCK = 128
HALO = 16
EXP_UNDERFLOW = -105.0
VMEM_LIMIT = 56 * 1024 * 1024
MM_SUB = 256
N_CHIPS = 4

ADAM_LR = 0.001
ADAM_B1 = 0.9
ADAM_B2 = 0.999
ADAM_EPS = 1e-08
ADAM_WD = 0.01
ADAM_STEP = 10

C_POOL, C_Z, C_XBC, C_Q, C_K, C_V, C_END = 0, 512, 1536, 3072, 3584, 4096, 4608


def _cparams(sem):
    return pltpu.CompilerParams(dimension_semantics=sem, vmem_limit_bytes=VMEM_LIMIT)


def _tile(dim, pref, unit=LANES):
    if dim <= pref:
        return dim
    t = (pref // unit) * unit
    while t > unit and dim % t:
        t -= unit
    assert dim % t == 0, (dim, pref)
    return t


def _silu(x):
    return x * jax.nn.sigmoid(x)


def _dsilu(x):
    s = jax.nn.sigmoid(x)
    return s * (1.0 + x * (1.0 - s))


def _dot(a, b):
    return jnp.dot(a.astype(BF16), b.astype(BF16), preferred_element_type=F32)


def _dot_nt(a, b):
    return lax.dot_general(a.astype(BF16), b.astype(BF16), (((1,), (1,)), ((), ())), preferred_element_type=F32)


def _split3(x):
    hi = x.astype(BF16)
    r = x - hi.astype(F32)
    mid = r.astype(BF16)
    lo = (r - mid.astype(F32)).astype(BF16)
    return hi, mid, lo


def _dot3(x, m):
    hi, mid, lo = _split3(x)
    dn = (((1,), (0,)), ((), ()))
    f = lambda p: lax.dot_general(p, m, dn, preferred_element_type=F32)
    return f(hi) + f(mid) + f(lo)


def _dot3_left(m, x):
    hi, mid, lo = _split3(x)
    dn = (((1,), (0,)), ((), ()))
    f = lambda p: lax.dot_general(m, p, dn, preferred_element_type=F32)
    return f(hi) + f(mid) + f(lo)


def _iota(shape, axis):
    return lax.broadcasted_iota(jnp.int32, shape, axis)


def _col(x, h):
    return jnp.sum(jnp.where(_iota(x.shape, 1) == h, x, 0.0), axis=1, keepdims=True)


def _roll_down(x, k):
    return x if k == 0 else pltpu.roll(x, k, 0)


def _roll_up(x, k):
    return x if k == 0 else pltpu.roll(x, x.shape[0] - k, 0)


def _mm(name, a_list, b_list, *, nt, epilogue, out_dtypes, acc_of=None, extras=(), stack=None, gather=(), tm=1024,
        tn=512, tk=1024):
    n_pairs = len(a_list)
    acc_of = list(acc_of) if acc_of is not None else [0] * n_pairs
    n_acc = max(acc_of) + 1
    m_dim, k_dim = a_list[0].shape
    n_dim = b_list[0].shape[0] if nt else b_list[0].shape[1]
    tm, tn, tk = _tile(m_dim, tm, 8), _tile(n_dim, tn), _tile(k_dim, tk)
    nk = k_dim // tk
    n_ex, n_out = len(extras), len(out_dtypes)
    n_buf = 0 if stack is None else len(stack[0])
    n_g = len(gather)
    n_scr = 0 if nk == 1 else n_acc
    grid = (m_dim // tm, n_dim // tn, nk)
    sub = MM_SUB if (nk == 1 and tn > MM_SUB and tn % MM_SUB == 0) else tn

    def body(*refs):
        a_refs = refs[:n_pairs]
        b_refs = refs[n_pairs:2 * n_pairs]
        e_refs = refs[2 * n_pairs:2 * n_pairs + n_ex]
        g_src = refs[2 * n_pairs + n_ex + n_buf:2 * n_pairs + n_ex + n_buf + n_g]
        first_out = 2 * n_pairs + n_ex + n_buf + n_g
        o_refs = refs[first_out:first_out + n_out]
        g_out = refs[first_out + n_out:first_out + n_out + n_g]
        acc_refs = refs[first_out + n_out + n_g:first_out + n_out + n_g + n_scr]
        sems = refs[first_out + n_out + n_g + n_scr:]
        if n_g:
            at = [pl.program_id(d) for d in range(3)]
            copies = _gather_copies(g_src, g_out, *sems)

            @pl.when((at[0] == 0) & (at[1] == 0) & (at[2] == 0))
            def _():
                _gather_start(*copies)

        if nk == 1:
            for s in range(tn // sub):
                cs = slice(s * sub, (s + 1) * sub)
                accs = [None] * n_acc
                for p in range(n_pairs):
                    d = _dot_nt(a_refs[p][...], b_refs[p][cs, :]) if nt else _dot(a_refs[p][...], b_refs[p][:, cs])
                    accs[acc_of[p]] = d if accs[acc_of[p]] is None else accs[acc_of[p]] + d
                outs = epilogue(accs, [e[:, cs] for e in e_refs])
                for o_ref, o in zip(o_refs, outs):
                    o_ref[:, cs] = o.astype(o_ref.dtype)
        else:
            k = pl.program_id(2)

            @pl.when(k == 0)
            def _():
                for acc in acc_refs:
                    acc[...] = jnp.zeros_like(acc)

            for p in range(n_pairs):
                a, b = a_refs[p][...], b_refs[p][...]
                acc_refs[acc_of[p]][...] += _dot_nt(a, b) if nt else _dot(a, b)

            @pl.when(k == nk - 1)
            def _():
                outs = epilogue([acc[...] for acc in acc_refs], [e[...] for e in e_refs])
                for o_ref, o in zip(o_refs, outs):
                    o_ref[...] = o.astype(o_ref.dtype)

        if n_g:
            @pl.when((at[0] == grid[0] - 1) & (at[1] == grid[1] - 1) & (at[2] == grid[2] - 1))
            def _():
                _gather_finish(*copies)

    a_spec = pl.BlockSpec((tm, tk), lambda i, j, k: (i, k))
    b_spec = pl.BlockSpec((tn, tk), lambda i, j, k: (j, k)) if nt else pl.BlockSpec((tk, tn), lambda i, j, k: (k, j))
    t_spec = pl.BlockSpec((tm, tn), lambda i, j, k: (i, j))
    n_in = 2 * n_pairs + n_ex
    if stack is None:
        out_specs = [t_spec] * n_out
        out_shape = [jax.ShapeDtypeStruct((m_dim, n_dim), dt) for dt in out_dtypes]
        bufs, aliases = [], {}
    else:
        bufs, slot = stack
        out_specs = [pl.BlockSpec((None, tm, tn), lambda i, j, k: (slot, i, j))] * n_out
        out_shape = [jax.ShapeDtypeStruct(b.shape, b.dtype) for b in bufs]
        aliases = {n_in + o: o for o in range(n_out)}
    hbm = lambda n: [pl.BlockSpec(memory_space=pl.ANY)] * n
    sems = [pltpu.SemaphoreType.DMA((3 * n_g,)), pltpu.SemaphoreType.DMA((3 * n_g,)), pltpu.SemaphoreType.DMA((n_g,))]
    return pl.pallas_call(
        body, name=name, grid=grid,
        in_specs=[a_spec] * n_pairs + [b_spec] * n_pairs + [t_spec] * n_ex + hbm(n_buf + n_g),
        out_specs=out_specs + hbm(n_g),
        out_shape=out_shape + [jax.ShapeDtypeStruct((N_CHIPS,) + g.shape, g.dtype) for g in gather],
        input_output_aliases=aliases,
        scratch_shapes=[pltpu.VMEM((tm, tn), F32)] * n_scr + (sems if n_g else []),
        compiler_params=_cparams(("arbitrary",) * 3 if n_g else ("parallel", "parallel", "arbitrary")),
    )(*a_list, *b_list, *extras, *bufs, *gather)


def _rms_fwd(name, h, g):
    s_len, d = h.shape
    ts = _tile(s_len, 512, 8)

    def body(h_ref, g_ref, u_ref):
        x = h_ref[...]
        rstd = lax.rsqrt(jnp.mean(x * x, axis=-1, keepdims=True) + RMS_EPS)
        u_ref[...] = (x * rstd * g_ref[...]).astype(BF16)

    return pl.pallas_call(
        body, name=name, grid=(s_len // ts,),
        in_specs=[pl.BlockSpec((ts, d), lambda i: (i, 0)), pl.BlockSpec((1, d), lambda i: (0, 0))],
        out_specs=pl.BlockSpec((ts, d), lambda i: (i, 0)),
        out_shape=jax.ShapeDtypeStruct((s_len, d), BF16),
        compiler_params=_cparams(("parallel",)),
    )(h, g)


def _rms_bwd(name, h, du, dres, g):
    s_len, d = h.shape
    ts = _tile(s_len, 256, 8)

    def body(h_ref, du_ref, dres_ref, g_ref, dh_ref, dhb_ref, dg_ref):
        x = h_ref[...]
        rstd = lax.rsqrt(jnp.mean(x * x, axis=-1, keepdims=True) + RMS_EPS)
        n = x * rstd
        dn = du_ref[...] * g_ref[...]
        dh = dres_ref[...] + rstd * (dn - n * jnp.mean(dn * n, axis=-1, keepdims=True))
        dh_ref[...] = dh
        dhb_ref[...] = dh.astype(BF16)

        @pl.when(pl.program_id(0) == 0)
        def _():
            dg_ref[...] = jnp.zeros_like(dg_ref)

        dg_ref[...] += jnp.sum(du_ref[...] * n, axis=0, keepdims=True)

    row = pl.BlockSpec((ts, d), lambda i: (i, 0))
    vec = pl.BlockSpec((1, d), lambda i: (0, 0))
    return pl.pallas_call(
        body, name=name, grid=(s_len // ts,),
        in_specs=[row, row, row, vec], out_specs=[row, row, vec],
        out_shape=[jax.ShapeDtypeStruct((s_len, d), F32), jax.ShapeDtypeStruct((s_len, d), BF16),
                   jax.ShapeDtypeStruct((1, d), F32)],
        compiler_params=_cparams(("arbitrary",)),
    )(h, du, dres, g)


def _loss_head(name, h, g, target):
    s_len, d = h.shape
    ts = _tile(s_len, 256, 8)

    def body(h_ref, g_ref, t_ref, loss_ref, dh_ref, dhb_ref, dg_ref):
        x = h_ref[...]
        rstd = lax.rsqrt(jnp.mean(x * x, axis=-1, keepdims=True) + RMS_EPS)
        n = x * rstd
        err = n * g_ref[...] - t_ref[...]
        dy = err * (1.0 / d)
        dn = dy * g_ref[...]
        dh = rstd * (dn - n * jnp.mean(dn * n, axis=-1, keepdims=True))
        dh_ref[...] = dh
        dhb_ref[...] = dh.astype(BF16)

        @pl.when(pl.program_id(0) == 0)
        def _():
            dg_ref[...] = jnp.zeros_like(dg_ref)
            loss_ref[...] = jnp.zeros_like(loss_ref)

        dg_ref[...] += jnp.sum(dy * n, axis=0, keepdims=True)
        part = jnp.sum(jnp.sum(err * err, axis=1, keepdims=True), axis=0, keepdims=True) * (0.5 / d)
        loss_ref[...] += jnp.broadcast_to(part, loss_ref.shape)

    row = pl.BlockSpec((ts, d), lambda i: (i, 0))
    vec = pl.BlockSpec((1, d), lambda i: (0, 0))
    lspec = pl.BlockSpec((1, LANES), lambda i: (0, 0))
    return pl.pallas_call(
        body, name=name, grid=(s_len // ts,),
        in_specs=[row, vec, row], out_specs=[lspec, row, row, vec],
        out_shape=[jax.ShapeDtypeStruct((1, LANES), F32), jax.ShapeDtypeStruct((s_len, d), F32),
                   jax.ShapeDtypeStruct((s_len, d), BF16), jax.ShapeDtypeStruct((1, d), F32)],
        compiler_params=_cparams(("arbitrary",)),
    )(h, g, target)


def _ffn_fwd(tag, h, g, wg, wu, wd, nxt=()):
    u = _rms_fwd(tag + "_norm", h, g)

    def up(accs, _):
        a, b = accs
        return a, b, _silu(a) * b

    a, b, hm, *got_up = _mm(tag + "_up", [u, u], [wg, wu], nt=False, acc_of=[0, 1], epilogue=up,
                            out_dtypes=[BF16, BF16, BF16], gather=nxt[:2])
    h2, *got_down = _mm(tag + "_down", [hm], [wd], nt=False, extras=[h], epilogue=lambda accs, ex: [ex[0] + 0.5 * accs[0]],
                        out_dtypes=[F32], gather=nxt[2:], tm=1024, tn=1024, tk=1408)
    return h2, (h, u, a, b, hm), got_up + got_down


def _ffn_bwd(tag, saved, g, wg, wu, wd, dh2, dh2_bf, stack=None):
    h, u, a, b, hm = saved
    st_gu = None if stack is None else (stack[0][:2], stack[1])
    st_d = None if stack is None else (stack[0][2:], stack[1])

    def dact(accs, ex):
        af, bf = ex[0].astype(F32), ex[1].astype(F32)
        dhm = 0.5 * accs[0]
        return dhm * bf * _dsilu(af), dhm * _silu(af)

    da, db = _mm(tag + "_dhm", [dh2_bf], [wd], nt=True, extras=[a, b], epilogue=dact, out_dtypes=[BF16, BF16], tk=2048)
    (dwd,) = _mm(tag + "_dwd", [hm.T], [dh2_bf], nt=False, epilogue=lambda accs, _: [0.5 * accs[0]], out_dtypes=[F32],
                 stack=st_d, tm=1408, tn=1024, tk=512)
    ut = u.T
    dwg, dwu = _mm(tag + "_dwgu", [ut, ut], [da, db], nt=False, acc_of=[0, 1], epilogue=lambda accs, _: accs,
                   out_dtypes=[F32, F32], stack=st_gu, tm=2048, tn=512, tk=1024)
    (du,) = _mm(tag + "_du", [da, db], [wg, wu], nt=True, epilogue=lambda accs, _: accs, out_dtypes=[F32],
                tm=1024, tn=1024, tk=1408)
    dh, dh_bf, dg = _rms_bwd(tag + "_dnorm", h, du, dh2, g)
    return dh, dh_bf, dg, dwg, dwu, dwd


def _softplus(x):
    e = jnp.exp(-jnp.abs(x))
    u = 1.0 + e
    log1p_e = jnp.where(u == 1.0, e, jnp.log(u) * (e / jnp.where(u == 1.0, 1.0, u - 1.0)))
    return jnp.maximum(x, 0.0) + log1p_e


def _row_spec(ts, width, colblock):
    return pl.BlockSpec((ts, width), lambda i: (i, colblock))


def _halo_before_spec(ts, width, colblock):
    r = ts // HALO
    return pl.BlockSpec((HALO, width), lambda i: (jnp.maximum(i * r - 1, 0), colblock))


def _halo_after_spec(ts, width, colblock, s_len):
    r = ts // HALO
    return pl.BlockSpec((HALO, width), lambda i: (jnp.minimum((i + 1) * r, s_len // HALO - 1), colblock))


def _const_spec(shape):
    nd = len(shape)
    return pl.BlockSpec(shape, lambda *_: (0,) * nd)


def _window_sum(e, win, roll):
    s, sh = e, 1
    while sh < win:
        s = s + roll(s, sh)
        sh *= 2
    return s


def _pool_center(ext, x, t, gi, win):
    sl = slice(gi * LANES, (gi + 1) * LANES)
    s = _window_sum(ext[:, sl], win, _roll_down)
    cnt = jnp.minimum(t + 1, win).astype(F32)
    return s[HALO:] / cnt - x[:, sl]


def _pool_fwd(name, proj, pw, scale):
    s_len = proj.shape[0]
    ts = _tile(s_len, 512, 8)
    width = len(POOL_WINDOWS) * LANES

    def body(x_ref, hb_ref, pw_ref, sc_ref, o_ref):
        i = pl.program_id(0)
        x = x_ref[...]
        ext = jnp.concatenate([jnp.where(i == 0, 0.0, hb_ref[...]), x], axis=0)
        t = i * ts + _iota((ts, 1), 0)
        for gi, win in enumerate(POOL_WINDOWS):
            sl = slice(gi * LANES, (gi + 1) * LANES)
            c = _pool_center(ext, x, t, gi, win)
            o_ref[:, sl] = (_dot(c, pw_ref[gi]) * sc_ref[:, sl]).astype(o_ref.dtype)

    return pl.pallas_call(
        body, name=name, grid=(s_len // ts,),
        in_specs=[_row_spec(ts, width, 0), _halo_before_spec(ts, width, 0), _const_spec(pw.shape), _const_spec(scale.shape)],
        out_specs=_row_spec(ts, width, 0),
        out_shape=jax.ShapeDtypeStruct((s_len, width), BF16),
        compiler_params=_cparams(("parallel",)),
    )(proj, proj, pw, scale)


def _pool_bwd(name, proj, dmixed, pw, scale):
    s_len = proj.shape[0]
    ts = _tile(s_len, 512, 8)
    n_tiles = s_len // ts
    width = len(POOL_WINDOWS) * LANES

    def body(x_ref, hb_ref, d_ref, da_ref, pw_ref, sc_ref, dx_ref, dpw_ref, dsc_ref):
        i = pl.program_id(0)
        x = x_ref[...]
        ext = jnp.concatenate([jnp.where(i == 0, 0.0, hb_ref[...]), x], axis=0)
        dout = d_ref[...]
        dext = jnp.concatenate([dout, jnp.where(i == n_tiles - 1, 0.0, da_ref[...])], axis=0)
        t = i * ts + _iota((ts, 1), 0)
        te = i * ts + _iota((ts + HALO, 1), 0)

        @pl.when(i == 0)
        def _():
            dpw_ref[...] = jnp.zeros_like(dpw_ref)
            dsc_ref[...] = jnp.zeros_like(dsc_ref)

        for gi, win in enumerate(POOL_WINDOWS):
            sl = slice(gi * LANES, (gi + 1) * LANES)
            c = _pool_center(ext, x, t, gi, win)
            o = _dot(c, pw_ref[gi])
            dsc_ref[:, sl] += jnp.sum(dout[:, sl] * o, axis=0, keepdims=True)
            do_ext = dext[:, sl] * sc_ref[:, sl]
            dc = _dot_nt(do_ext, pw_ref[gi])
            e = dc / jnp.minimum(te + 1, win).astype(F32)
            back = _window_sum(e, win, _roll_up)
            dx_ref[:, sl] = (back[:ts] - dc[:ts]).astype(dx_ref.dtype)
            dpw_ref[gi] += _dot(c.T, do_ext[:ts])

    return pl.pallas_call(
        body, name=name, grid=(n_tiles,),
        in_specs=[_row_spec(ts, width, 0), _halo_before_spec(ts, width, 0), _row_spec(ts, width, 0),
                  _halo_after_spec(ts, width, 0, s_len), _const_spec(pw.shape), _const_spec(scale.shape)],
        out_specs=[_row_spec(ts, width, 0), _const_spec(pw.shape), _const_spec(scale.shape)],
        out_shape=[jax.ShapeDtypeStruct((s_len, width), BF16), jax.ShapeDtypeStruct(pw.shape, F32),
                   jax.ShapeDtypeStruct(scale.shape, F32)],
        compiler_params=_cparams(("arbitrary",)),
    )(proj, proj, dmixed, dmixed, pw, scale)


def _conv_pre(ext, w_ref, b_ref):
    k_len = w_ref.shape[0]
    y = _roll_down(ext, k_len - 1) * w_ref[0:1, :]
    for k in range(1, k_len):
        y = y + _roll_down(ext, k_len - 1 - k) * w_ref[k:k + 1, :]
    return y + b_ref[...]


def _conv_fwd(name, proj, w, b):
    s_len = proj.shape[0]
    width = w.shape[1]
    ts = _tile(s_len, 512, 8)
    cb = C_XBC // width

    def body(x_ref, hb_ref, w_ref, b_ref, o_ref):
        i = pl.program_id(0)
        ext = jnp.concatenate([jnp.where(i == 0, 0.0, hb_ref[...]), x_ref[...]], axis=0)
        o_ref[...] = _silu(_conv_pre(ext, w_ref, b_ref)[HALO:])

    return pl.pallas_call(
        body, name=name, grid=(s_len // ts,),
        in_specs=[_row_spec(ts, width, cb), _halo_before_spec(ts, width, cb), _const_spec(w.shape), _const_spec(b.shape)],
        out_specs=_row_spec(ts, width, 0),
        out_shape=jax.ShapeDtypeStruct((s_len, width), F32),
        compiler_params=_cparams(("parallel",)),
    )(proj, proj, w, b)


def _conv_bwd(name, proj, dact, w, b):
    s_len = proj.shape[0]
    width = w.shape[1]
    k_len = w.shape[0]
    ts = _tile(s_len, 512, 8)
    n_tiles = s_len // ts
    cb = C_XBC // width

    def body(x_ref, hb_ref, ha_ref, d_ref, da_ref, w_ref, b_ref, dx_ref, dw_ref, db_ref):
        i = pl.program_id(0)
        last = i == n_tiles - 1
        ext = jnp.concatenate([jnp.where(i == 0, 0.0, hb_ref[...]), x_ref[...], jnp.where(last, 0.0, ha_ref[...])], axis=0)
        pre = _conv_pre(ext, w_ref, b_ref)[HALO:]
        dpre = jnp.concatenate([d_ref[...], jnp.where(last, 0.0, da_ref[...])], axis=0) * _dsilu(pre)

        @pl.when(i == 0)
        def _():
            dw_ref[...] = jnp.zeros_like(dw_ref)
            db_ref[...] = jnp.zeros_like(db_ref)

        dx = _roll_up(dpre, k_len - 1) * w_ref[0:1, :]
        for k in range(1, k_len):
            dx = dx + _roll_up(dpre, k_len - 1 - k) * w_ref[k:k + 1, :]
        dx_ref[...] = dx[:ts].astype(dx_ref.dtype)
        dtile = dpre[:ts]
        for k in range(k_len):
            xk = _roll_down(ext, k_len - 1 - k)[HALO:HALO + ts]
            dw_ref[k:k + 1, :] += jnp.sum(dtile * xk, axis=0, keepdims=True)
        db_ref[...] += jnp.sum(dtile, axis=0, keepdims=True)

    return pl.pallas_call(
        body, name=name, grid=(n_tiles,),
        in_specs=[_row_spec(ts, width, cb), _halo_before_spec(ts, width, cb), _halo_after_spec(ts, width, cb, s_len),
                  _row_spec(ts, width, 0), _halo_after_spec(ts, width, 0, s_len), _const_spec(w.shape), _const_spec(b.shape)],
        out_specs=[_row_spec(ts, width, 0), _const_spec(w.shape), _const_spec(b.shape)],
        out_shape=[jax.ShapeDtypeStruct((s_len, width), BF16), jax.ShapeDtypeStruct(w.shape, F32),
                   jax.ShapeDtypeStruct(b.shape, F32)],
        compiler_params=_cparams(("arbitrary",)),
    )(proj, proj, proj, dact, dact, w, b)


def _pair_cols(c0, c1, lo_half):
    return jnp.where(lo_half, c0, c1)


def _ssd_common(dtr_ref, bias_ref, alog_ref, acs_t_ref):
    chunk = dtr_ref.shape[0]
    xpre = dtr_ref[...] + bias_ref[...]
    dt = _softplus(xpre)
    a_neg = -jnp.exp(alog_ref[...])
    tri = _iota((chunk, chunk), 1) <= _iota((chunk, chunk), 0)
    a_cs = _dot3_left(tri.astype(BF16), dt * a_neg)
    acs_t_ref[...] = a_cs.T
    a_last = jnp.sum(jnp.where(_iota(a_cs.shape, 0) == chunk - 1, a_cs, 0.0), axis=0, keepdims=True)
    return xpre, dt, a_neg, tri, a_cs, a_last


def _ssd_specs(chunk, order):
    xs = pl.BlockSpec((chunk, 1024), lambda c: (order(c), 0))
    bm = pl.BlockSpec((chunk, 256), lambda c: (order(c), 4))
    cm = pl.BlockSpec((chunk, 256), lambda c: (order(c), 5))
    lanes = pl.BlockSpec((chunk, LANES), lambda c: (order(c), 0))
    return xs, bm, cm, lanes


def _ssd_fwd(name, xbc, dtr, dt_bias, a_log, d_skip):
    s_len = xbc.shape[0]
    chunk = SSD_CHUNK
    nc = s_len // chunk
    n_pairs = SSD_HEADS // 2

    def body(xs_ref, b_ref, c_ref, dtr_ref, bias_ref, alog_ref, dsk_ref, y_ref, st_ref, state_ref, acs_t_ref):
        @pl.when(pl.program_id(0) == 0)
        def _():
            state_ref[...] = jnp.zeros_like(state_ref)

        _, dt, _, tri, a_cs, a_last = _ssd_common(dtr_ref, bias_ref, alog_ref, acs_t_ref)
        lo_half = _iota((chunk, LANES), 1) < HEAD_DIM
        lo_lane = _iota((1, LANES), 1) < HEAD_DIM
        lo_row = _iota((LANES, 1), 0) < HEAD_DIM
        dsk = dsk_ref[...]
        for g in range(2):
            gsl = slice(g * LANES, (g + 1) * LANES)
            bg, cg = b_ref[:, gsl], c_ref[:, gsl]
            gmat = _dot_nt(cg, bg)
            for pr in range(n_pairs // 2):
                pair = g * (n_pairs // 2) + pr
                h0, h1 = 2 * pair, 2 * pair + 1
                psl = slice(pair * LANES, (pair + 1) * LANES)
                x2 = xs_ref[:, psl]
                acs0, acs1 = _col(a_cs, h0), _col(a_cs, h1)
                xdt = x2 * _pair_cols(_col(dt, h0), _col(dt, h1), lo_half)
                y2 = jnp.zeros((chunk, LANES), F32)
                for h, acs_c, hmask in ((h0, acs0, lo_half), (h1, acs1, ~lo_half)):
                    lam = jnp.where(tri, jnp.exp(jnp.minimum(acs_c - acs_t_ref[h:h + 1, :], 0.0)), 0.0)
                    y2 = y2 + _dot(gmat * lam, jnp.where(hmask, xdt, 0.0))
                s2 = state_ref[pair]
                st_ref[0, pair] = s2
                y2 = y2 + _pair_cols(jnp.exp(acs0), jnp.exp(acs1), lo_half) * _dot_nt(cg, s2)
                y_ref[:, psl] = y2 + _pair_cols(_col(dsk, h0), _col(dsk, h1), lo_lane) * x2
                al0, al1 = _col(a_last, h0), _col(a_last, h1)
                wl2 = _pair_cols(jnp.exp(al0 - acs0), jnp.exp(al1 - acs1), lo_half)
                state_ref[pair] = _pair_cols(jnp.exp(al0), jnp.exp(al1), lo_row) * s2 + _dot((xdt * wl2).T, bg)

    xs, bm, cm, lanes = _ssd_specs(chunk, lambda c: c)
    vec = _const_spec((1, LANES))
    return pl.pallas_call(
        body, name=name, grid=(nc,),
        in_specs=[xs, bm, cm, lanes, vec, vec, vec],
        out_specs=[xs, pl.BlockSpec((1, n_pairs, LANES, LANES), lambda c: (c, 0, 0, 0))],
        out_shape=[jax.ShapeDtypeStruct((s_len, 1024), F32), jax.ShapeDtypeStruct((nc, n_pairs, LANES, LANES), F32)],
        scratch_shapes=[pltpu.VMEM((n_pairs, LANES, LANES), F32), pltpu.VMEM((LANES, chunk), F32)],
        compiler_params=_cparams(("arbitrary",)),
    )(xbc, xbc, xbc, dtr, dt_bias, a_log, d_skip)


def _ssd_bwd(name, xbc, dtr, states, dy, dt_bias, a_log, d_skip):
    s_len = xbc.shape[0]
    chunk = SSD_CHUNK
    nc = s_len // chunk
    n_pairs = SSD_HEADS // 2
    rev = lambda c: nc - 1 - c

    def body(xs_ref, b_ref, c_ref, dtr_ref, dy_ref, sin_ref, bias_ref, alog_ref, dsk_ref,
             dxs_ref, db_ref, dc_ref, ddtr_ref, dbias_ref, dalog_ref, ddsk_ref, dstate_ref, acs_t_ref):
        @pl.when(pl.program_id(0) == 0)
        def _():
            dstate_ref[...] = jnp.zeros_like(dstate_ref)
            dbias_ref[...] = jnp.zeros_like(dbias_ref)
            dalog_ref[...] = jnp.zeros_like(dalog_ref)
            ddsk_ref[...] = jnp.zeros_like(ddsk_ref)

        xpre, dt, a_neg, tri, a_cs, a_last = _ssd_common(dtr_ref, bias_ref, alog_ref, acs_t_ref)
        lane = _iota((chunk, LANES), 1)
        lo_half = lane < HEAD_DIM
        lane1 = _iota((1, LANES), 1)
        lo_lane = lane1 < HEAD_DIM
        lo_row = _iota((LANES, 1), 0) < HEAD_DIM
        head_row = _iota((LANES, chunk), 0)
        sq_row, sq_col = _iota((chunk, chunk), 0), _iota((chunk, chunk), 1)
        before = (sq_row < sq_col).astype(BF16)
        dsk = dsk_ref[...]
        da_rows = jnp.zeros((LANES, chunk), F32)
        yo = jnp.zeros((chunk, LANES), F32)
        to = jnp.zeros((chunk, LANES), F32)
        vs = jnp.zeros((1, LANES), F32)
        ddt = jnp.zeros((chunk, LANES), F32)
        ddsk = jnp.zeros((1, LANES), F32)

        def half_sums(v):
            lo = jnp.sum(jnp.where(lo_half, v, 0.0), axis=1, keepdims=True)
            return lo, jnp.sum(v, axis=1, keepdims=True) - lo

        for g in range(2):
            gsl = slice(g * LANES, (g + 1) * LANES)
            bg, cg = b_ref[:, gsl], c_ref[:, gsl]
            gmat = _dot_nt(cg, bg)
            dgm = jnp.zeros((chunk, chunk), F32)
            dbg = jnp.zeros((chunk, LANES), F32)
            dcg = jnp.zeros((chunk, LANES), F32)
            for pr in range(n_pairs // 2):
                pair = g * (n_pairs // 2) + pr
                h0, h1 = 2 * pair, 2 * pair + 1
                psl = slice(pair * LANES, (pair + 1) * LANES)
                x2, dy2 = xs_ref[:, psl], dy_ref[:, psl]
                acs0, acs1 = _col(a_cs, h0), _col(a_cs, h1)
                dt2 = _pair_cols(_col(dt, h0), _col(dt, h1), lo_half)
                xdt = x2 * dt2
                al0, al1 = _col(a_last, h0), _col(a_last, h1)
                v2 = _pair_cols(jnp.exp(acs0), jnp.exp(acs1), lo_half)
                wl2 = _pair_cols(jnp.exp(al0 - acs0), jnp.exp(al1 - acs1), lo_half)
                s_in, ds2 = sin_ref[0, pair], dstate_ref[pair]
                y_off = v2 * _dot_nt(cg, s_in)
                dx_state = wl2 * _dot_nt(bg, ds2)
                dx2 = dx_state
                for h, acs_c, hmask in ((h0, acs0, lo_half), (h1, acs1, ~lo_half)):
                    lam = jnp.where(tri, jnp.exp(jnp.minimum(acs_c - acs_t_ref[h:h + 1, :], 0.0)), 0.0)
                    m = gmat * lam
                    dyh = jnp.where(hmask, dy2, 0.0)
                    dx2 = dx2 + _dot(m.T, dyh)
                    dml = _dot_nt(dyh, xdt) * lam
                    dgm = dgm + dml
                    crossed = jnp.where(sq_row >= sq_col, _dot3(dml * gmat, before), 0.0)
                    da_rows = jnp.where(head_row == h, jnp.sum(crossed, axis=0, keepdims=True), da_rows)
                vdy = v2 * dy2
                dcg = dcg + _dot(vdy, s_in)
                dbg = dbg + _dot(wl2 * xdt, ds2)
                yo0, yo1 = half_sums(dy2 * y_off)
                yo = jnp.where(lane == h0, yo0, jnp.where(lane == h1, yo1, yo))
                to0, to1 = half_sums(dx_state * xdt)
                to = jnp.where(lane == h0, to0, jnp.where(lane == h1, to1, to))
                prod = jnp.sum(ds2 * s_in, axis=1, keepdims=True)
                e0 = jnp.sum(jnp.where(lo_row, prod, 0.0), axis=0, keepdims=True)
                e1 = jnp.sum(prod, axis=0, keepdims=True) - e0
                vs = jnp.where(lane1 == h0, jnp.exp(al0) * e0, jnp.where(lane1 == h1, jnp.exp(al1) * e1, vs))
                q0, q1 = half_sums(dx2 * x2)
                ddt = jnp.where(lane == h0, q0, jnp.where(lane == h1, q1, ddt))
                dxs_ref[:, psl] = dx2 * dt2 + _pair_cols(_col(dsk, h0), _col(dsk, h1), lo_lane) * dy2
                s0, s1 = half_sums(dy2 * x2)
                ddsk = jnp.where(lane1 == h0, jnp.sum(s0, axis=0, keepdims=True),
                                 jnp.where(lane1 == h1, jnp.sum(s1, axis=0, keepdims=True), ddsk))
                dstate_ref[pair] = _pair_cols(jnp.exp(al0), jnp.exp(al1), lo_row) * ds2 + _dot(vdy.T, cg)
            dc_ref[:, gsl] = dcg + _dot(dgm, bg)
            db_ref[:, gsl] = dbg + _dot(dgm.T, cg)

        da = (da_rows.T + _dot3_left((sq_col >= sq_row).astype(BF16), yo)
              + _dot3_left((sq_col < sq_row).astype(BF16), to) + vs)
        ddt = ddt + da * a_neg
        dalog_ref[...] += jnp.sum(da * dt, axis=0, keepdims=True) * a_neg
        ddtr = jnp.where(lane < SSD_HEADS, ddt * jax.nn.sigmoid(xpre), 0.0)
        ddtr_ref[...] = ddtr
        dbias_ref[...] += jnp.sum(ddtr, axis=0, keepdims=True)
        ddsk_ref[...] += ddsk

    xs, bm, cm, lanes = _ssd_specs(chunk, rev)
    vec = _const_spec((1, LANES))
    st_in = pl.BlockSpec((1, n_pairs, LANES, LANES), lambda c: (rev(c), 0, 0, 0))
    bc_out = pl.BlockSpec((chunk, 256), lambda c: (rev(c), 0))
    return pl.pallas_call(
        body, name=name, grid=(nc,),
        in_specs=[xs, bm, cm, lanes, xs, st_in, vec, vec, vec],
        out_specs=[xs, bc_out, bc_out, lanes, vec, vec, vec],
        out_shape=[jax.ShapeDtypeStruct((s_len, 1024), F32), jax.ShapeDtypeStruct((s_len, 256), F32),
                   jax.ShapeDtypeStruct((s_len, 256), F32), jax.ShapeDtypeStruct((s_len, LANES), F32),
                   jax.ShapeDtypeStruct((1, LANES), F32), jax.ShapeDtypeStruct((1, LANES), F32),
                   jax.ShapeDtypeStruct((1, LANES), F32)],
        scratch_shapes=[pltpu.VMEM((n_pairs, LANES, LANES), F32), pltpu.VMEM((LANES, chunk), F32)],
        compiler_params=_cparams(("arbitrary",)),
    )(xbc, xbc, xbc, dtr, dy, states, dt_bias, a_log, d_skip)


def _gatenorm_fwd(name, y, proj, g):
    s_len = y.shape[0]
    ts = _tile(s_len, 512, 8)
    gw = 512

    def body(y_ref, z_ref, g_ref, o_ref):
        yg = y_ref[...] * _silu(z_ref[...])
        rstd = lax.rsqrt(jnp.mean(yg * yg, axis=-1, keepdims=True) + RMS_EPS)
        o_ref[...] = (yg * rstd * g_ref[...]).astype(o_ref.dtype)

    return pl.pallas_call(
        body, name=name, grid=(2, s_len // ts),
        in_specs=[pl.BlockSpec((ts, gw), lambda gi, i: (i, gi)), pl.BlockSpec((ts, gw), lambda gi, i: (i, C_Z // gw + gi)),
                  pl.BlockSpec((1, gw), lambda gi, i: (0, gi))],
        out_specs=pl.BlockSpec((ts, gw), lambda gi, i: (i, gi)),
        out_shape=jax.ShapeDtypeStruct((s_len, 2 * gw), BF16),
        compiler_params=_cparams(("parallel", "parallel")),
    )(y, proj, g)


def _gatenorm_bwd(name, y, proj, g, dmixed):
    s_len = y.shape[0]
    ts = _tile(s_len, 512, 8)
    gw = 512

    def body(y_ref, z_ref, g_ref, d_ref, dy_ref, dz_ref, dg_ref):
        yv, z = y_ref[...], z_ref[...]
        sz = _silu(z)
        yg = yv * sz
        rstd = lax.rsqrt(jnp.mean(yg * yg, axis=-1, keepdims=True) + RMS_EPS)
        n = yg * rstd
        dn = d_ref[...] * g_ref[...]
        dyg = rstd * (dn - n * jnp.mean(dn * n, axis=-1, keepdims=True))
        dy_ref[...] = dyg * sz
        dz_ref[...] = (dyg * yv * _dsilu(z)).astype(dz_ref.dtype)

        @pl.when(pl.program_id(1) == 0)
        def _():
            dg_ref[...] = jnp.zeros_like(dg_ref)

        dg_ref[...] += jnp.sum(d_ref[...] * n, axis=0, keepdims=True)

    grp = pl.BlockSpec((ts, gw), lambda gi, i: (i, gi))
    vec = pl.BlockSpec((1, gw), lambda gi, i: (0, gi))
    return pl.pallas_call(
        body, name=name, grid=(2, s_len // ts),
        in_specs=[grp, pl.BlockSpec((ts, gw), lambda gi, i: (i, C_Z // gw + gi)), vec,
                  pl.BlockSpec((ts, gw), lambda gi, i: (i, 1 + gi))],
        out_specs=[grp, grp, vec],
        out_shape=[jax.ShapeDtypeStruct((s_len, 2 * gw), F32), jax.ShapeDtypeStruct((s_len, 2 * gw), BF16),
                   jax.ShapeDtypeStruct((1, 2 * gw), F32)],
        compiler_params=_cparams(("parallel", "arbitrary")),
    )(y, proj, g, dmixed)


def _attn_scores(qh, kblk, mask, ustrict, r):
    z = _dot_nt(qh, kblk)
    sp = _softplus(-jnp.abs(z))
    ls = jnp.minimum(z, 0.0) - sp
    lm_raw = jnp.minimum(-z, 0.0) - sp
    lm = jnp.where(mask, lm_raw, 0.0)
    suffix = _dot3(lm, ustrict)
    w = jnp.where(mask, jnp.exp(ls + suffix + r), 0.0)
    return ls, lm_raw, lm, w


def _attn_specs(tq, s_len):
    qcol, kcol, vcol = C_Q // LANES, C_K // LANES, C_V // LANES
    q = pl.BlockSpec((tq, LANES), lambda p, i: (i, qcol + p))
    k = pl.BlockSpec((s_len, LANES), lambda p, i: (0, kcol + p))
    v = pl.BlockSpec((s_len, LANES), lambda p, i: (0, vcol + p))
    return q, k, v


def _attn_fwd(name, proj):
    s_len = proj.shape[0]
    tq = ATTN_BLOCK
    n_slabs = 4

    def body(q_ref, k_ref, v_ref, o_ref):
        qi = pl.program_id(1)
        q2 = q_ref[...] * (HEAD_DIM ** -0.5)
        lo = _iota((tq, LANES), 1) < HEAD_DIM
        heads = ((jnp.where(lo, q2, 0.0).astype(BF16), lo), (jnp.where(lo, 0.0, q2).astype(BF16), ~lo))
        row, col = _iota((tq, tq), 0), _iota((tq, tq), 1)
        ustrict = (row > col).astype(BF16)

        def step(carry):
            kb, _, r0, r1, acc = carry
            rows = pl.ds(pl.multiple_of(kb * tq, tq), tq)
            kblk, vblk = k_ref[rows, :].astype(BF16), v_ref[rows, :]
            mask = (col < row) | (kb < qi)
            new_r = []
            for (qh, hmask), r in zip(heads, (r0, r1)):
                _, _, lm, w = _attn_scores(qh, kblk, mask, ustrict, r)
                acc = acc + _dot(w, jnp.where(hmask, vblk, 0.0))
                new_r.append(r + jnp.sum(lm, axis=1, keepdims=True))
            go = (jnp.maximum(jnp.max(new_r[0]), jnp.max(new_r[1])) > EXP_UNDERFLOW).astype(jnp.int32)
            return kb - 1, go, new_r[0], new_r[1], acc

        zero = jnp.zeros((tq, 1), F32)
        init = (qi, jnp.int32(1), zero, zero, jnp.zeros((tq, LANES), F32))
        o_ref[...] = lax.while_loop(lambda c: (c[0] >= 0) & (c[1] > 0), step, init)[4]

    q, k, v = _attn_specs(tq, s_len)
    return pl.pallas_call(
        body, name=name, grid=(n_slabs, s_len // tq),
        in_specs=[q, k, v], out_specs=pl.BlockSpec((tq, LANES), lambda p, i: (i, p)),
        out_shape=jax.ShapeDtypeStruct((s_len, n_slabs * LANES), F32),
        compiler_params=_cparams(("parallel", "arbitrary")),
    )(proj, proj, proj)


def _attn_bwd(name, proj, dmixed):
    s_len = proj.shape[0]
    tq = ATTN_BLOCK
    n_slabs = 4
    scale = HEAD_DIM ** -0.5

    def body(q_ref, k_ref, v_ref, do_ref, dq_ref, dk_ref, dv_ref, dk_acc, dv_acc, r_hist):
        qi = pl.program_id(1)

        @pl.when(qi == 0)
        def _():
            dk_acc[...] = jnp.zeros_like(dk_acc)
            dv_acc[...] = jnp.zeros_like(dv_acc)

        q2 = q_ref[...] * scale
        do2 = do_ref[...]
        lo = _iota((tq, LANES), 1) < HEAD_DIM
        heads = ((jnp.where(lo, q2, 0.0).astype(BF16), jnp.where(lo, do2, 0.0).astype(BF16), lo),
                 (jnp.where(lo, 0.0, q2).astype(BF16), jnp.where(lo, 0.0, do2).astype(BF16), ~lo))
        row, col = _iota((tq, tq), 0), _iota((tq, tq), 1)
        ustrict = (row > col).astype(BF16)
        earlier = (row < col).astype(BF16)
        zero = jnp.zeros((tq, 1), F32)

        def scan(carry):
            kb, _, r0, r1 = carry
            kblk = k_ref[pl.ds(pl.multiple_of(kb * tq, tq), tq), :].astype(BF16)
            mask = (col < row) | (kb < qi)
            r_hist[kb] = jnp.where(lo, r0, r1)
            new_r = []
            for (qh, _, _), r in zip(heads, (r0, r1)):
                z = _dot_nt(qh, kblk)
                lm = jnp.where(mask, jnp.minimum(-z, 0.0) - _softplus(-jnp.abs(z)), 0.0)
                new_r.append(r + jnp.sum(lm, axis=1, keepdims=True))
            go = (jnp.maximum(jnp.max(new_r[0]), jnp.max(new_r[1])) > EXP_UNDERFLOW).astype(jnp.int32)
            return kb - 1, go, new_r[0], new_r[1]

        first = lax.while_loop(lambda c: (c[0] >= 0) & (c[1] > 0), scan, (qi, jnp.int32(1), zero, zero))[0] + 1

        def step(carry):
            kb, p0, p1, dq = carry
            rows = pl.ds(pl.multiple_of(kb * tq, tq), tq)
            kf, vblk = k_ref[rows, :], v_ref[rows, :].astype(BF16)
            kblk = kf.astype(BF16)
            mask = (col < row) | (kb < qi)
            rr = r_hist[kb]
            dk_blk = jnp.zeros((tq, LANES), F32)
            dv_blk = jnp.zeros((tq, LANES), F32)
            new_p = []
            for (qh, doh, hmask), r, p in zip(heads, (_col(rr, 0), _col(rr, HEAD_DIM)), (p0, p1)):
                ls, lm_raw, _, w = _attn_scores(qh, kblk, mask, ustrict, r)
                ew = _dot_nt(doh, vblk) * w
                before = p + _dot3(ew, earlier)
                dz = jnp.where(mask, ew * jnp.exp(lm_raw) - jnp.exp(ls) * before, 0.0)
                dq = dq + _dot(dz, jnp.where(hmask, kf, 0.0))
                dk_blk = dk_blk + _dot(dz.T, qh)
                dv_blk = dv_blk + _dot(w.T, doh)
                new_p.append(p + jnp.sum(ew, axis=1, keepdims=True))
            dk_acc[rows, :] += dk_blk
            dv_acc[rows, :] += dv_blk
            return kb + 1, new_p[0], new_p[1], dq

        dq = lax.while_loop(lambda c: c[0] <= qi, step, (first, zero, zero, jnp.zeros((tq, LANES), F32)))[3]
        dq_ref[...] = (dq * scale).astype(dq_ref.dtype)

        @pl.when(qi == pl.num_programs(1) - 1)
        def _():
            dk_ref[...] = dk_acc[...].astype(dk_ref.dtype)
            dv_ref[...] = dv_acc[...].astype(dv_ref.dtype)

    q, k, v = _attn_specs(tq, s_len)
    blk = pl.BlockSpec((tq, LANES), lambda p, i: (i, p))
    full = pl.BlockSpec((s_len, LANES), lambda p, i: (0, p))
    shape = jax.ShapeDtypeStruct((s_len, n_slabs * LANES), BF16)
    return pl.pallas_call(
        body, name=name, grid=(n_slabs, s_len // tq),
        in_specs=[q, k, v, pl.BlockSpec((tq, LANES), lambda p, i: (i, 1536 // LANES + p))],
        out_specs=[blk, full, full], out_shape=[shape, shape, shape],
        scratch_shapes=[pltpu.VMEM((s_len, LANES), F32), pltpu.VMEM((s_len, LANES), F32),
                        pltpu.VMEM((s_len // tq, tq, LANES), F32)],
        compiler_params=_cparams(("parallel", "arbitrary")),
    )(proj, proj, proj, dmixed)


def _ident(accs, _):
    return accs


def _mixer_fwd(tag, h, p, nxt=()):
    u = _rms_fwd(tag + "_norm", h, p["mix_norm"])
    proj, *got_in = _mm(tag + "_in", [u], [p["w_main"]], nt=False, epilogue=_ident, out_dtypes=[F32], gather=nxt[:1],
                        tk=2048)
    (dtr,) = _mm(tag + "_indt", [u], [p["w_dt"]], nt=False, epilogue=_ident, out_dtypes=[F32], tk=2048)
    pool_out = _pool_fwd(tag + "_pool", proj, p["pool_w"], p["pool_scale"])
    xbc = _conv_fwd(tag + "_conv", proj, p["conv_w"], p["conv_b"])
    y, states = _ssd_fwd(tag + "_ssd", xbc, dtr, p["dt_bias"], p["a_log"], p["d_skip"])
    ssd_out = _gatenorm_fwd(tag + "_gate", y, proj, p["ssd_norm"])
    attn = _attn_fwd(tag + "_attn", proj)
    mixed = jnp.concatenate([pool_out, ssd_out, attn.astype(BF16)], axis=1)
    h2, *got_out = _mm(tag + "_out", [mixed], [p["w_out"]], nt=False, extras=[h],
                       epilogue=lambda accs, ex: [ex[0] + accs[0]], out_dtypes=[F32], gather=nxt[1:], tk=2048)
    return h2, (h, u, proj, dtr, xbc, y, states, mixed), got_in + got_out


def _mixer_bwd(tag, saved, p, dh2, dh2_bf, stack=None):
    h, u, proj, dtr, xbc, y, states, mixed = saved
    (dmixed,) = _mm(tag + "_dmix", [dh2_bf], [p["w_out"]], nt=True, epilogue=_ident, out_dtypes=[F32], tk=2048)
    (dw_out,) = _mm(tag + "_dwout", [mixed.T], [dh2_bf], nt=False, epilogue=_ident, out_dtypes=[F32], stack=stack)
    dpool_in, dpool_w, dpool_scale = _pool_bwd(tag + "_dpool", proj, dmixed, p["pool_w"], p["pool_scale"])
    dy, dz, dssd_norm = _gatenorm_bwd(tag + "_dgate", y, proj, p["ssd_norm"], dmixed)
    dxs, dbm, dcm, ddtr, ddt_bias, da_log, dd_skip = _ssd_bwd(tag + "_dssd", xbc, dtr, states, dy, p["dt_bias"],
                                                             p["a_log"], p["d_skip"])
    dxbc, dconv_w, dconv_b = _conv_bwd(tag + "_dconv", proj, jnp.concatenate([dxs, dbm, dcm], axis=1), p["conv_w"],
                                       p["conv_b"])
    dq, dk, dv = _attn_bwd(tag + "_dattn", proj, dmixed)
    dproj = jnp.concatenate([dpool_in, dz, dxbc, dq, dk, dv], axis=1)
    ddtr_bf = ddtr.astype(BF16)
    ut = u.T
    (dw_main,) = _mm(tag + "_dwin", [ut], [dproj], nt=False, epilogue=_ident, out_dtypes=[F32])
    (dw_dt,) = _mm(tag + "_dwdt", [ut], [ddtr_bf], nt=False, epilogue=_ident, out_dtypes=[F32])
    (du_dt,) = _mm(tag + "_dudt", [ddtr_bf], [p["w_dt"]], nt=True, epilogue=_ident, out_dtypes=[F32], tn=1024)
    (du,) = _mm(tag + "_du", [dproj], [p["w_main"]], nt=True, extras=[du_dt],
                epilogue=lambda accs, ex: [accs[0] + ex[0]], out_dtypes=[F32], tm=1024, tn=1024, tk=1536)
    dh, dh_bf, dg = _rms_bwd(tag + "_dnorm", h, du, dh2, p["mix_norm"])
    grads = dict(mix_norm=dg, w_main=dw_main, w_dt=dw_dt, pool_w=dpool_w, pool_scale=dpool_scale, conv_w=dconv_w,
                 conv_b=dconv_b, dt_bias=ddt_bias, a_log=da_log, d_skip=dd_skip, ssd_norm=dssd_norm, w_out=dw_out)
    return dh, dh_bf, grads


def _axes():
    return lax.axis_index("x"), lax.axis_index("y"), lax.axis_index("c")


def _any_specs(n):
    return [pl.BlockSpec(memory_space=pl.ANY) for _ in range(n)]


def _remote(src, dst, send, recv, k, dev):
    return pltpu.make_async_remote_copy(src_ref=src, dst_ref=dst, send_sem=send.at[k], recv_sem=recv.at[k],
                                        device_id=dev, device_id_type=MESH_ID)


def _gather_copies(srcs, outs, send, recv, loc):
    n = len(srcs)
    x, y, c = _axes()
    me = 2 * x + y
    peers = [(1 - x, y), (x, 1 - y), (1 - x, 1 - y)]
    local = [pltpu.make_async_copy(srcs[a], outs[a].at[me], loc.at[a]) for a in range(n)]
    sent = [_remote(srcs[a], outs[a].at[me], send, recv, 3 * a + k, (px, py, c))
            for a in range(n) for k, (px, py) in enumerate(peers)]
    received = [_remote(srcs[a], outs[a].at[2 * px + py], send, recv, 3 * a + k, (px, py, c))
                for a in range(n) for k, (px, py) in enumerate(peers)]
    return local, sent, received


def _gather_start(local, sent, received):
    for cp in local + sent:
        cp.start()


def _gather_finish(local, sent, received):
    for cp in received:
        cp.wait_recv()
    for cp in sent:
        cp.wait_send()
    for cp in local:
        cp.wait()


def _gather_chips(name, arrs):
    n = len(arrs)

    def body(*refs):
        copies = _gather_copies(refs[:n], refs[n:2 * n], *refs[2 * n:])
        _gather_start(*copies)
        _gather_finish(*copies)

    return pl.pallas_call(
        body, name=name, in_specs=_any_specs(n), out_specs=_any_specs(n),
        out_shape=[jax.ShapeDtypeStruct((4,) + a.shape, a.dtype) for a in arrs],
        scratch_shapes=[pltpu.SemaphoreType.DMA((3 * n,)), pltpu.SemaphoreType.DMA((3 * n,)), pltpu.SemaphoreType.DMA((n,))],
    )(*arrs)


def _half_view(kind, ref, hc):
    return ref.at[:, pl.ds(2 * hc, 2)] if kind == "win" else ref.at[pl.ds(2 * hc, 2)]


def _half_shape(kind, shape):
    return (shape[0], 2) + tuple(shape[2:]) if kind == "win" else (2,) + tuple(shape[1:])


def _shard_view(kind, ref, j):
    if kind == "col":
        w = ref.shape[2] // 4
        return ref.at[:, :, pl.ds(pl.multiple_of(j * w, LANES), w)]
    if kind == "row":
        r = ref.shape[1] // 4
        return ref.at[:, pl.ds(pl.multiple_of(j * r, 16), r), :]
    return ref.at[j]


def _shard_shape(kind, hshape):
    if kind == "col":
        return (2, hshape[1], hshape[2] // 4)
    if kind == "row":
        return (2, hshape[1] // 4, hshape[2])
    return tuple(hshape[1:])


def _rs_pair(name, kinds, grads):
    n = len(grads)

    def body(*refs):
        g_refs, o_refs = refs[:n], refs[n:2 * n]
        send, recv = refs[2 * n:]
        x, y, c = _axes()
        cps = [_remote(_half_view(kinds[a], g_refs[a], 1 - c), o_refs[a], send, recv, a, (x, y, 1 - c)) for a in range(n)]
        for cp in cps:
            cp.start()
        for cp in cps:
            cp.wait()

    return pl.pallas_call(
        body, name=name, in_specs=_any_specs(n), out_specs=_any_specs(n),
        out_shape=[jax.ShapeDtypeStruct(_half_shape(k, g.shape), g.dtype) for k, g in zip(kinds, grads)],
        scratch_shapes=[pltpu.SemaphoreType.DMA((n,)), pltpu.SemaphoreType.DMA((n,))],
    )(*grads)


def _rs_chips(name, kinds, halves):
    n = len(halves)

    def body(*refs):
        h_refs, o_refs = refs[:n], refs[n:2 * n]
        send, recv = refs[2 * n:]
        x, y, c = _axes()
        peers = [(1 - x, y), (x, 1 - y), (1 - x, 1 - y)]
        cps = [_remote(_shard_view(kinds[a], h_refs[a], 2 * px + py), o_refs[a].at[k], send, recv, 3 * a + k, (px, py, c))
               for a in range(n) for k, (px, py) in enumerate(peers)]
        for cp in cps:
            cp.start()
        for cp in cps:
            cp.wait()

    return pl.pallas_call(
        body, name=name, in_specs=_any_specs(n), out_specs=_any_specs(n),
        out_shape=[jax.ShapeDtypeStruct((3,) + _shard_shape(k, h.shape), h.dtype) for k, h in zip(kinds, halves)],
        scratch_shapes=[pltpu.SemaphoreType.DMA((3 * n,)), pltpu.SemaphoreType.DMA((3 * n,))],
    )(*halves)


def _ag_pair(name, parts):
    n = len(parts)

    def body(*refs):
        t_refs, o_refs = refs[:n], refs[n:2 * n]
        send, recv, loc = refs[2 * n:]
        x, y, c = _axes()
        local = [pltpu.make_async_copy(t_refs[a], o_refs[a].at[pl.ds(2 * c, 2)], loc.at[a]) for a in range(n)]
        sent = [_remote(t_refs[a], o_refs[a].at[pl.ds(2 * c, 2)], send, recv, a, (x, y, 1 - c)) for a in range(n)]
        for cp in local + sent:
            cp.start()
        for a in range(n):
            _remote(t_refs[a], o_refs[a].at[pl.ds(2 * (1 - c), 2)], send, recv, a, (x, y, 1 - c)).wait_recv()
        for cp in sent:
            cp.wait_send()
        for cp in local:
            cp.wait()

    return pl.pallas_call(
        body, name=name, in_specs=_any_specs(n), out_specs=_any_specs(n),
        out_shape=[jax.ShapeDtypeStruct((4,) + t.shape[1:], t.dtype) for t in parts],
        scratch_shapes=[pltpu.SemaphoreType.DMA((n,)), pltpu.SemaphoreType.DMA((n,)), pltpu.SemaphoreType.DMA((n,))],
    )(*parts)


def _esum(name, grid, block, ins, outs, where):
    n_in = len(ins)

    def body(s_ref, *refs):
        tot = refs[0][...].astype(F32)
        for r in refs[1:n_in]:
            tot = tot + r[...].astype(F32)
        for o in refs[n_in:]:
            o[...] = tot.astype(o.dtype)

    spec = lambda nd, imap: pl.BlockSpec((None,) * (nd - 2) + tuple(block), imap)
    return pl.pallas_call(
        body, name=name,
        grid_spec=pltpu.PrefetchScalarGridSpec(
            num_scalar_prefetch=1, grid=grid,
            in_specs=[spec(a.ndim, m) for a, m in ins], out_specs=[spec(len(s), m) for s, _, m in outs]),
        out_shape=[jax.ShapeDtypeStruct(s, dt) for s, dt, _ in outs],
        compiler_params=_cparams(("parallel",) * len(grid)),
    )(where, *[a for a, _ in ins])


def _sum_pair(name, kind, g, r1, where):
    hshape = _half_shape(kind, g.shape)
    if kind == "win":
        block = (_tile(g.shape[2], 512, 16), g.shape[3])
        grid = (g.shape[0], 2, g.shape[2] // block[0])
        gmap = lambda s4, a, i, s: (s4, 2 * s[0] + a, i, 0)
        hmap = lambda s4, a, i, s: (s4, a, i, 0)
    elif kind == "col":
        block = (_tile(g.shape[1], 512, 16), g.shape[2] // N_CHIPS)
        grid = (2, g.shape[1] // block[0], N_CHIPS)
        gmap = lambda a, i, j, s: (2 * s[0] + a, i, j)
        hmap = lambda a, i, j, s: (a, i, j)
        wire = (N_CHIPS, 2, g.shape[1], block[1])
        return _esum(name, grid, block, [(g, gmap), (r1, hmap)],
                     [(hshape, F32, hmap), (wire, BF16, lambda a, i, j, s: (j, a, i, 0))], where)
    else:
        block = (_tile(g.shape[1], 512, 16), _tile(g.shape[2], 2048))
        grid = (2, g.shape[1] // block[0], g.shape[2] // block[1])
        gmap = lambda a, i, j, s: (2 * s[0] + a, i, j)
        hmap = lambda a, i, j, s: (a, i, j)
    return _esum(name, grid, block, [(g, gmap), (r1, hmap)], [(hshape, F32, hmap), (hshape, BF16, hmap)], where)


def _sum_chips(name, kind, h32, r2, where):
    tshape = _shard_shape(kind, h32.shape)
    if kind == "col":
        block = (_tile(tshape[1], 512, 16), tshape[2])
        hmap = lambda a, i, s: (a, i, s[1])
    elif kind == "row":
        block = (_tile(tshape[1], 512, 16), tshape[2])
        nb = tshape[1] // block[0]
        hmap = lambda a, i, s: (a, s[1] * nb + i, 0)
    else:
        block = (_tile(tshape[1], 512, 16), tshape[2])
        hmap = lambda a, i, s: (s[1], a, i, 0)
    grid = (2, tshape[1] // block[0])
    tmap = lambda a, i, s: (a, i, 0)
    rmap = lambda k: (lambda a, i, s: (k, a, i, 0))
    return _esum(name, grid, block, [(h32, hmap)] + [(r2, rmap(k)) for k in range(3)], [(tshape, F32, tmap)], where)[0]


def _allreduce_small(name, vec):
    rows_n = vec.shape[0]

    def body(x_ref, sum_ref, all_ref, send, recv, local_sem):
        x, y, c = _axes()
        me, sibling = (x, y, c), (x, y, 1 - c)
        chips = [(1 - x, y), (x, 1 - y), (1 - x, 1 - y)]

        def rows(px, py, pc):
            return all_ref.at[pl.ds(pl.multiple_of((4 * px + 2 * py + pc) * rows_n, 8), rows_n), :]

        def copy(k, block, to, src=None):
            return _remote(rows(*block) if src is None else src, rows(*block), send, recv, k, to)

        mine = pltpu.make_async_copy(x_ref, rows(*me), local_sem)
        mine.start()
        first = [copy(0, me, sibling, src=x_ref)] + [copy(1 + j, me, (*chip, c), src=x_ref) for j, chip in enumerate(chips)]
        for cp in first:
            cp.start()
        passed = [copy(4 + j, (*chip, c), sibling) for j, chip in enumerate(chips)]
        for j, chip in enumerate(chips):
            copy(1 + j, (*chip, c), me).wait_recv()
            passed[j].start()
        copy(0, sibling, me).wait_recv()
        for j, chip in enumerate(chips):
            copy(4 + j, (*chip, 1 - c), me).wait_recv()
        for cp in first + passed:
            cp.wait_send()
        mine.wait()
        tot = all_ref[0:rows_n, :]
        for d in range(1, 8):
            tot = tot + all_ref[d * rows_n:(d + 1) * rows_n, :]
        sum_ref[...] = tot

    vm = pl.BlockSpec(memory_space=pltpu.VMEM)
    return pl.pallas_call(
        body, name=name, in_specs=[vm], out_specs=[vm, vm],
        out_shape=[jax.ShapeDtypeStruct(vec.shape, F32), jax.ShapeDtypeStruct((8 * rows_n, LANES), F32)],
        scratch_shapes=[pltpu.SemaphoreType.DMA((7,)), pltpu.SemaphoreType.DMA((7,)), pltpu.SemaphoreType.DMA],
        compiler_params=pltpu.CompilerParams(vmem_limit_bytes=VMEM_LIMIT),
    )(vec)[0]


def _adamw(name, w, g, m, v):
    shape = w.shape
    rows_n, cols = shape[-2], shape[-1]
    lead = math.prod(shape[:-2])
    tr = _tile(rows_n, 256, 8)

    def body(w_ref, g_ref, m_ref, v_ref, d_ref, m2_ref, v2_ref):
        gv = g_ref[...]
        m2 = ADAM_B1 * m_ref[...] + (1.0 - ADAM_B1) * gv
        v2 = ADAM_B2 * v_ref[...] + (1.0 - ADAM_B2) * jnp.square(gv)
        m_hat = m2 / (1.0 - ADAM_B1 ** ADAM_STEP)
        v_hat = v2 / (1.0 - ADAM_B2 ** ADAM_STEP)
        d_ref[...] = -ADAM_LR * (m_hat / (jnp.sqrt(v_hat) + ADAM_EPS) + ADAM_WD * w_ref[...])
        m2_ref[...] = m2
        v2_ref[...] = v2

    spec = pl.BlockSpec((None, tr, cols), lambda l, i: (l, i, 0))
    flat = (lead, rows_n, cols)
    outs = pl.pallas_call(
        body, name=name, grid=(lead, rows_n // tr), in_specs=[spec] * 4, out_specs=[spec] * 3,
        out_shape=[jax.ShapeDtypeStruct(flat, F32)] * 3,
        compiler_params=_cparams(("parallel", "parallel")),
    )(*[t.reshape(flat) for t in (w, g, m, v)])
    return [o.reshape(shape) for o in outs]


WEIGHTS = ("ffn1_norm", "ffn1_w_gate", "ffn1_w_up", "ffn1_w_down", "mix_norm", "w_in", "pool_w", "pool_scale", "conv_w",
           "conv_b", "dt_bias", "a_log", "d_skip", "ssd_norm", "w_out", "ffn2_norm", "ffn2_w_gate", "ffn2_w_up",
           "ffn2_w_down", "final_norm")
BIG = {"ffn1_w_gate": "col", "ffn1_w_up": "col", "ffn1_w_down": "row", "w_in": "win", "w_out": "row",
       "ffn2_w_gate": "col", "ffn2_w_up": "col", "ffn2_w_down": "row"}
SMALL = tuple(n for n in WEIGHTS if n not in BIG and n != "conv_w")
REF_DT = 3072


def _pack(parts):
    flat = jnp.concatenate([p.reshape(-1) for p in parts])
    rows_n = -(-flat.shape[0] // (8 * LANES)) * 8
    return jnp.pad(flat, (0, rows_n * LANES - flat.shape[0])).reshape(rows_n, LANES)


def _unpack(block, shapes):
    flat, out, at = block.reshape(-1), [], 0
    for s in shapes:
        n = math.prod(s)
        out.append(flat[at:at + n].reshape(s))
        at += n
    return out


def _train_step(a):
    depth = a["ffn1_norm"].shape[0]
    x_id, y_id, c_id = _axes()
    chip = 2 * x_id + y_id
    where = jnp.stack([c_id, chip]).astype(jnp.int32)

    big = list(BIG)
    shards = lambda l: [a[n][l].astype(BF16) for n in big]
    join = lambda n, g: jnp.concatenate([g[s] for s in range(N_CHIPS)], axis=0 if BIG[n] == "row" else 1)
    *first, conv4 = _gather_chips("gather_layer0", shards(0) + [a["conv_w"]])
    conv_w = jnp.concatenate([conv4[s] for s in range(N_CHIPS)], axis=2)
    full = [None] * depth
    full[0] = {n: join(n, g) for n, g in zip(big, first)}
    heads128 = lambda v: jnp.pad(v, ((0, 0), (0, LANES - SSD_HEADS)))
    dt_bias, a_log, d_skip = heads128(a["dt_bias"]), heads128(a["a_log"]), heads128(a["d_skip"])

    def mixer_params(l):
        w_in = full[l]["w_in"]
        w_main = jnp.concatenate([w_in[:, :REF_DT], w_in[:, REF_DT + SSD_HEADS:]], axis=1)
        w_dt = jnp.pad(w_in[:, REF_DT:REF_DT + SSD_HEADS], ((0, 0), (0, LANES - SSD_HEADS)))
        return dict(mix_norm=a["mix_norm"][l][None], w_main=w_main, w_dt=w_dt, pool_w=a["pool_w"][l],
                    pool_scale=a["pool_scale"][l][None], conv_w=conv_w[l], conv_b=a["conv_b"][l][None],
                    dt_bias=dt_bias[l][None], a_log=a_log[l][None], d_skip=d_skip[l][None],
                    ssd_norm=a["ssd_norm"][l][None], w_out=full[l]["w_out"])

    def ffn_params(l, which):
        return (a[which + "_norm"][l][None], full[l][which + "_w_gate"], full[l][which + "_w_up"], full[l][which + "_w_down"])

    h = a["x"][0]
    saved, mixer_p = [], []
    for l in range(depth):
        nxt = dict(zip(big, shards(l + 1))) if l + 1 < depth else None
        pick = lambda *names: tuple(nxt[n] for n in names) if nxt else ()
        ffn1, mix, ffn2 = ("ffn1_w_gate", "ffn1_w_up", "ffn1_w_down"), ("w_in", "w_out"), ("ffn2_w_gate", "ffn2_w_up", "ffn2_w_down")
        mixer_p.append(mixer_params(l))
        h, s1, g1 = _ffn_fwd(f"l{l}_ffn1", h, *ffn_params(l, "ffn1"), nxt=pick(*ffn1))
        h, sm, gm = _mixer_fwd(f"l{l}_mix", h, mixer_p[l], nxt=pick(*mix))
        h, s2, g2 = _ffn_fwd(f"l{l}_ffn2", h, *ffn_params(l, "ffn2"), nxt=pick(*ffn2))
        saved.append((s1, sm, s2))
        if nxt:
            full[l + 1] = {n: join(n, g) for n, g in zip(ffn1 + mix + ffn2, g1 + gm + g2)}
    loss_part, dh, dh_bf, dfinal = _loss_head("loss_head", h, a["final_norm"][None], a["loss_target"][0])

    bufs = {n: lax.empty((depth,) + tuple(full[0][n].shape), F32) for n in big if n != "w_in"}
    small = {n: [None] * depth for n in SMALL if n != "final_norm"}
    small["conv_w"] = [None] * depth
    dw_in = [None] * depth
    for l in reversed(range(depth)):
        s1, sm, s2 = saved[l]
        names = ["ffn2_w_gate", "ffn2_w_up", "ffn2_w_down"]
        dh, dh_bf, small["ffn2_norm"][l], *new = _ffn_bwd(f"l{l}_ffn2", s2, *ffn_params(l, "ffn2"), dh, dh_bf,
                                                          stack=([bufs[n] for n in names], l))
        bufs.update(zip(names, new))
        dh, dh_bf, g = _mixer_bwd(f"l{l}_mix", sm, mixer_p[l], dh, dh_bf, stack=([bufs["w_out"]], l))
        bufs["w_out"] = g["w_out"]
        for n in ("mix_norm", "pool_w", "pool_scale", "conv_w", "conv_b", "ssd_norm"):
            small[n][l] = g[n]
        for n in ("dt_bias", "a_log", "d_skip"):
            small[n][l] = g[n][:, :SSD_HEADS]
        dw_in[l] = jnp.concatenate([g["w_main"][:, :REF_DT], g["w_dt"][:, :SSD_HEADS], g["w_main"][:, REF_DT:]], axis=1)
        names = ["ffn1_w_gate", "ffn1_w_up", "ffn1_w_down"]
        dh, dh_bf, small["ffn1_norm"][l], *new = _ffn_bwd(f"l{l}_ffn1", s1, *ffn_params(l, "ffn1"), dh, dh_bf,
                                                          stack=([bufs[n] for n in names], l))
        bufs.update(zip(names, new))
    grad_x = dh[None]

    win = jnp.stack(dw_in)
    win = win.reshape(depth, win.shape[1], N_CHIPS, win.shape[2] // N_CHIPS).transpose(2, 0, 1, 3)
    kinds = [BIG[n] for n in big]
    grads = [win if n == "w_in" else bufs[n] for n in big]
    from_sibling = _rs_pair("reduce_pair", kinds, grads)
    halves = [_sum_pair(f"sum_pair_{n}", k, g, r, where) for n, k, g, r in zip(big, kinds, grads, from_sibling)]
    wire_kinds = ["win" if k == "col" else k for k in kinds]
    from_chips = _rs_chips("reduce_chips", wire_kinds, [h16 for _, h16 in halves])
    parts = [_sum_chips(f"sum_chips_{n}", k, h32, r, where) for n, k, (h32, _), r in zip(big, kinds, halves, from_chips)]
    grad = dict(zip(big, _ag_pair("share_pair", parts)))

    small_full = {n: jnp.stack([t.reshape(a[n].shape[1:]) for t in small[n]]) for n in SMALL if n != "final_norm"}
    small_full["final_norm"] = dfinal.reshape(a["final_norm"].shape)
    conv_full = jnp.stack(small["conv_w"])
    shapes = [a[n].shape for n in SMALL] + [conv_full.shape]
    reduced = _unpack(_allreduce_small("allreduce_small", _pack([small_full[n] for n in SMALL] + [conv_full])), shapes)
    grad.update(zip(SMALL, reduced[:-1]))
    shard = a["conv_w"].shape[2]
    grad["conv_w"] = lax.dynamic_slice_in_dim(reduced[-1], chip * shard, shard, axis=2)

    delta, new_m, new_v = {}, {}, {}
    for n in big + ["conv_w"]:
        delta[n], new_m[n], new_v[n] = _adamw(f"adamw_{n}", a[n], grad[n], a["m_" + n], a["v_" + n])
    packed = [_pack([a[pre + n] for n in SMALL]) for pre in ("", "m_", "v_")]
    outs = _adamw("adamw_small", packed[0], _pack([grad[n] for n in SMALL]), packed[1], packed[2])
    for store, block in zip((delta, new_m, new_v), outs):
        store.update(zip(SMALL, _unpack(block, [a[n].shape for n in SMALL])))

    loss = lax.psum(loss_part[0, 0], ("x", "y", "c"))
    return (loss, grad_x, *[grad[n] for n in WEIGHTS], *[delta[n] for n in WEIGHTS], *[new_m[n] for n in WEIGHTS],
            *[new_v[n] for n in WEIGHTS])


def kernel(x, ffn1_norm, ffn1_w_gate, ffn1_w_up, ffn1_w_down, mix_norm, w_in, pool_w, pool_scale, conv_w, conv_b, dt_bias, a_log, d_skip, ssd_norm, w_out, ffn2_norm, ffn2_w_gate, ffn2_w_up, ffn2_w_down, final_norm, loss_target, m_ffn1_norm, m_ffn1_w_gate, m_ffn1_w_up, m_ffn1_w_down, m_mix_norm, m_w_in, m_pool_w, m_pool_scale, m_conv_w, m_conv_b, m_dt_bias, m_a_log, m_d_skip, m_ssd_norm, m_w_out, m_ffn2_norm, m_ffn2_w_gate, m_ffn2_w_up, m_ffn2_w_down, m_final_norm, v_ffn1_norm, v_ffn1_w_gate, v_ffn1_w_up, v_ffn1_w_down, v_mix_norm, v_w_in, v_pool_w, v_pool_scale, v_conv_w, v_conv_b, v_dt_bias, v_a_log, v_d_skip, v_ssd_norm, v_w_out, v_ffn2_norm, v_ffn2_w_gate, v_ffn2_w_up, v_ffn2_w_down, v_final_norm):
    return _train_step(dict(locals()))
```

```python
import functools
import math

import jax
import jax.numpy as jnp
from jax import lax
from jax.experimental import pallas as pl
from jax.experimental.pallas import tpu as pltpu

F32 = jnp.float32
BF16 = jnp.bfloat16
MESH_ID = pl.DeviceIdType.MESH

RMS_EPS = 1e-6
POOL_WINDOWS = (2, 4, 8, 16)
LANES = 128
HEAD_DIM = 64
SSD_HEADS = 16
SSD_CHUNK = 256
ATTN_BLOCK = 128
HALO = 16
EXP_UNDERFLOW = -105.0
VMEM_LIMIT = 56 * 1024 * 1024
MM_SUB = 256
N_CHIPS = 4

ADAM_LR = 0.001
ADAM_B1 = 0.9
ADAM_B2 = 0.999
ADAM_EPS = 1e-08
ADAM_WD = 0.01
ADAM_STEP = 10

C_POOL, C_Z, C_XBC, C_Q, C_K, C_V, C_END = 0, 512, 1536, 3072, 3584, 4096, 4608


def _cparams(sem):
    return pltpu.CompilerParams(dimension_semantics=sem, vmem_limit_bytes=VMEM_LIMIT)


def _tile(dim, pref, unit=LANES):
    if dim <= pref:
        return dim
    t = (pref // unit) * unit
    while t > unit and dim % t:
        t -= unit
    assert dim % t == 0, (dim, pref)
    return t


def _silu(x):
    return x * jax.nn.sigmoid(x)


def _dsilu(x):
    s = jax.nn.sigmoid(x)
    return s * (1.0 + x * (1.0 - s))


def _dot(a, b):
    return jnp.dot(a.astype(BF16), b.astype(BF16), preferred_element_type=F32)


def _dot_nt(a, b):
    return lax.dot_general(a.astype(BF16), b.astype(BF16), (((1,), (1,)), ((), ())), preferred_element_type=F32)


def _dot_tn(a, b):
    return lax.dot_general(a.astype(BF16), b.astype(BF16), (((0,), (0,)), ((), ())), preferred_element_type=F32)


def _split3(x):
    hi = x.astype(BF16)
    r = x - hi.astype(F32)
    mid = r.astype(BF16)
    lo = (r - mid.astype(F32)).astype(BF16)
    return hi, mid, lo


def _dot3(x, m):
    hi, mid, lo = _split3(x)
    dn = (((1,), (0,)), ((), ()))
    f = lambda p: lax.dot_general(p, m, dn, preferred_element_type=F32)
    return f(hi) + f(mid) + f(lo)


def _dot3_left(m, x):
    hi, mid, lo = _split3(x)
    dn = (((1,), (0,)), ((), ()))
    f = lambda p: lax.dot_general(m, p, dn, preferred_element_type=F32)
    return f(hi) + f(mid) + f(lo)


def _iota(shape, axis):
    return lax.broadcasted_iota(jnp.int32, shape, axis)


def _col(x, h):
    return jnp.sum(jnp.where(_iota(x.shape, 1) == h, x, 0.0), axis=1, keepdims=True)


def _roll_down(x, k):
    return x if k == 0 else pltpu.roll(x, k, 0)


def _roll_up(x, k):
    return x if k == 0 else pltpu.roll(x, x.shape[0] - k, 0)


def _mm(name, a_list, b_list, *, nt, epilogue, out_dtypes, acc_of=None, extras=(), stack=None, gather=(), ta=False,
        tm=1024, tn=512, tk=1024):
    n_pairs = len(a_list)
    acc_of = list(acc_of) if acc_of is not None else [0] * n_pairs
    n_acc = max(acc_of) + 1
    m_dim, k_dim = a_list[0].shape[::-1] if ta else a_list[0].shape
    n_dim = b_list[0].shape[0] if nt else b_list[0].shape[1]
    tm, tn, tk = _tile(m_dim, tm, 8), _tile(n_dim, tn), _tile(k_dim, tk)
    nk = k_dim // tk
    n_ex, n_out = len(extras), len(out_dtypes)
    n_buf = 0 if stack is None else len(stack[0])
    n_g = len(gather)
    n_scr = 0 if nk == 1 else n_acc
    grid = (m_dim // tm, n_dim // tn, nk)
    sub = MM_SUB if (nk == 1 and tn > MM_SUB and tn % MM_SUB == 0) else tn

    def body(*refs):
        a_refs = refs[:n_pairs]
        b_refs = refs[n_pairs:2 * n_pairs]
        e_refs = refs[2 * n_pairs:2 * n_pairs + n_ex]
        g_src = refs[2 * n_pairs + n_ex + n_buf:2 * n_pairs + n_ex + n_buf + n_g]
        first_out = 2 * n_pairs + n_ex + n_buf + n_g
        o_refs = refs[first_out:first_out + n_out]
        g_out = refs[first_out + n_out:first_out + n_out + n_g]
        acc_refs = refs[first_out + n_out + n_g:first_out + n_out + n_g + n_scr]
        sems = refs[first_out + n_out + n_g + n_scr:]
        if n_g:
            at = [pl.program_id(d) for d in range(3)]
            copies = _gather_copies(g_src, g_out, *sems)

            @pl.when((at[0] == 0) & (at[1] == 0) & (at[2] == 0))
            def _():
                _gather_start(*copies)

        if nk == 1:
            for s in range(tn // sub):
                cs = slice(s * sub, (s + 1) * sub)
                accs = [None] * n_acc
                for p in range(n_pairs):
                    a = a_refs[p][...]
                    d = _dot_tn(a, b_refs[p][:, cs]) if ta else _dot_nt(a, b_refs[p][cs, :]) if nt else _dot(a, b_refs[p][:, cs])
                    accs[acc_of[p]] = d if accs[acc_of[p]] is None else accs[acc_of[p]] + d
                outs = epilogue(accs, [e[:, cs] for e in e_refs])
                for o_ref, o in zip(o_refs, outs):
                    o_ref[:, cs] = o.astype(o_ref.dtype)
        else:
            k = pl.program_id(2)

            @pl.when(k == 0)
            def _():
                for acc in acc_refs:
                    acc[...] = jnp.zeros_like(acc)

            for p in range(n_pairs):
                a, b = a_refs[p][...], b_refs[p][...]
                acc_refs[acc_of[p]][...] += _dot_tn(a, b) if ta else _dot_nt(a, b) if nt else _dot(a, b)

            @pl.when(k == nk - 1)
            def _():
                outs = epilogue([acc[...] for acc in acc_refs], [e[...] for e in e_refs])
                for o_ref, o in zip(o_refs, outs):
                    o_ref[...] = o.astype(o_ref.dtype)

        if n_g:
            @pl.when((at[0] == grid[0] - 1) & (at[1] == grid[1] - 1) & (at[2] == grid[2] - 1))
            def _():
                _gather_finish(*copies)

    a_spec = pl.BlockSpec((tk, tm), lambda i, j, k: (k, i)) if ta else pl.BlockSpec((tm, tk), lambda i, j, k: (i, k))
    b_spec = pl.BlockSpec((tn, tk), lambda i, j, k: (j, k)) if nt else pl.BlockSpec((tk, tn), lambda i, j, k: (k, j))
    t_spec = pl.BlockSpec((tm, tn), lambda i, j, k: (i, j))
    n_in = 2 * n_pairs + n_ex
    if stack is None:
        out_specs = [t_spec] * n_out
        out_shape = [jax.ShapeDtypeStruct((m_dim, n_dim), dt) for dt in out_dtypes]
        bufs, aliases = [], {}
    else:
        bufs, slot = stack
        out_specs = [pl.BlockSpec((None, tm, tn), lambda i, j, k: (slot, i, j))] * n_out
        out_shape = [jax.ShapeDtypeStruct(b.shape, b.dtype) for b in bufs]
        aliases = {n_in + o: o for o in range(n_out)}
    hbm = lambda n: [pl.BlockSpec(memory_space=pl.ANY)] * n
    sems = [pltpu.SemaphoreType.DMA((3 * n_g,)), pltpu.SemaphoreType.DMA((3 * n_g,)), pltpu.SemaphoreType.DMA((n_g,))]
    return pl.pallas_call(
        body, name=name, grid=grid,
        in_specs=[a_spec] * n_pairs + [b_spec] * n_pairs + [t_spec] * n_ex + hbm(n_buf + n_g),
        out_specs=out_specs + hbm(n_g),
        out_shape=out_shape + [jax.ShapeDtypeStruct((N_CHIPS,) + g.shape, g.dtype) for g in gather],
        input_output_aliases=aliases,
        scratch_shapes=[pltpu.VMEM((tm, tn), F32)] * n_scr + (sems if n_g else []),
        compiler_params=_cparams(("arbitrary",) * 3 if n_g else ("parallel", "parallel", "arbitrary")),
    )(*a_list, *b_list, *extras, *bufs, *gather)


def _rms_fwd(name, h, g):
    s_len, d = h.shape
    ts = _tile(s_len, 512, 8)

    def body(h_ref, g_ref, u_ref):
        x = h_ref[...]
        rstd = lax.rsqrt(jnp.mean(x * x, axis=-1, keepdims=True) + RMS_EPS)
        u_ref[...] = (x * rstd * g_ref[...]).astype(BF16)

    return pl.pallas_call(
        body, name=name, grid=(s_len // ts,),
        in_specs=[pl.BlockSpec((ts, d), lambda i: (i, 0)), pl.BlockSpec((1, d), lambda i: (0, 0))],
        out_specs=pl.BlockSpec((ts, d), lambda i: (i, 0)),
        out_shape=jax.ShapeDtypeStruct((s_len, d), BF16),
        compiler_params=_cparams(("parallel",)),
    )(h, g)


def _rms_bwd(name, h, du, dres, g):
    s_len, d = h.shape
    ts = _tile(s_len, 256, 8)

    def body(h_ref, du_ref, dres_ref, g_ref, dh_ref, dhb_ref, dg_ref):
        x = h_ref[...]
        rstd = lax.rsqrt(jnp.mean(x * x, axis=-1, keepdims=True) + RMS_EPS)
        n = x * rstd
        dn = du_ref[...] * g_ref[...]
        dh = dres_ref[...] + rstd * (dn - n * jnp.mean(dn * n, axis=-1, keepdims=True))
        dh_ref[...] = dh
        dhb_ref[...] = dh.astype(BF16)

        @pl.when(pl.program_id(0) == 0)
        def _():
            dg_ref[...] = jnp.zeros_like(dg_ref)

        dg_ref[...] += jnp.sum(du_ref[...] * n, axis=0, keepdims=True)

    row = pl.BlockSpec((ts, d), lambda i: (i, 0))
    vec = pl.BlockSpec((1, d), lambda i: (0, 0))
    return pl.pallas_call(
        body, name=name, grid=(s_len // ts,),
        in_specs=[row, row, row, vec], out_specs=[row, row, vec],
        out_shape=[jax.ShapeDtypeStruct((s_len, d), F32), jax.ShapeDtypeStruct((s_len, d), BF16),
                   jax.ShapeDtypeStruct((1, d), F32)],
        compiler_params=_cparams(("arbitrary",)),
    )(h, du, dres, g)


def _loss_head(name, h, g, target):
    s_len, d = h.shape
    ts = _tile(s_len, 256, 8)

    def body(h_ref, g_ref, t_ref, loss_ref, dh_ref, dhb_ref, dg_ref):
        x = h_ref[...]
        rstd = lax.rsqrt(jnp.mean(x * x, axis=-1, keepdims=True) + RMS_EPS)
        n = x * rstd
        err = n * g_ref[...] - t_ref[...]
        dy = err * (1.0 / d)
        dn = dy * g_ref[...]
        dh = rstd * (dn - n * jnp.mean(dn * n, axis=-1, keepdims=True))
        dh_ref[...] = dh
        dhb_ref[...] = dh.astype(BF16)

        @pl.when(pl.program_id(0) == 0)
        def _():
            dg_ref[...] = jnp.zeros_like(dg_ref)
            loss_ref[...] = jnp.zeros_like(loss_ref)

        dg_ref[...] += jnp.sum(dy * n, axis=0, keepdims=True)
        part = jnp.sum(jnp.sum(err * err, axis=1, keepdims=True), axis=0, keepdims=True) * (0.5 / d)
        loss_ref[...] += jnp.broadcast_to(part, loss_ref.shape)

    row = pl.BlockSpec((ts, d), lambda i: (i, 0))
    vec = pl.BlockSpec((1, d), lambda i: (0, 0))
    lspec = pl.BlockSpec((1, LANES), lambda i: (0, 0))
    return pl.pallas_call(
        body, name=name, grid=(s_len // ts,),
        in_specs=[row, vec, row], out_specs=[lspec, row, row, vec],
        out_shape=[jax.ShapeDtypeStruct((1, LANES), F32), jax.ShapeDtypeStruct((s_len, d), F32),
                   jax.ShapeDtypeStruct((s_len, d), BF16), jax.ShapeDtypeStruct((1, d), F32)],
        compiler_params=_cparams(("arbitrary",)),
    )(h, g, target)


def _ffn_fwd(tag, h, g, wg, wu, wd, nxt=()):
    u = _rms_fwd(tag + "_norm", h, g)

    def up(accs, _):
        a, b = accs
        return a, b, _silu(a) * b

    a, b, hm, *got_up = _mm(tag + "_up", [u, u], [wg, wu], nt=False, acc_of=[0, 1], epilogue=up,
                            out_dtypes=[BF16, BF16, BF16], gather=nxt[:2])
    h2, *got_down = _mm(tag + "_down", [hm], [wd], nt=False, extras=[h], epilogue=lambda accs, ex: [ex[0] + 0.5 * accs[0]],
                        out_dtypes=[F32], gather=nxt[2:], tm=1024, tn=1024, tk=1408)
    return h2, (h, u, a, b, hm), got_up + got_down


def _ffn_bwd(tag, saved, g, wg, wu, wd, dh2, dh2_bf, stack=None):
    h, u, a, b, hm = saved
    st_gu = None if stack is None else (stack[0][:2], stack[1])
    st_d = None if stack is None else (stack[0][2:], stack[1])

    def dact(accs, ex):
        af, bf = ex[0].astype(F32), ex[1].astype(F32)
        dhm = 0.5 * accs[0]
        return dhm * bf * _dsilu(af), dhm * _silu(af)

    da, db = _mm(tag + "_dhm", [dh2_bf], [wd], nt=True, extras=[a, b], epilogue=dact, out_dtypes=[BF16, BF16], tk=2048)
    (dwd,) = _mm(tag + "_dwd", [hm], [dh2_bf], nt=False, ta=True, epilogue=lambda accs, _: [0.5 * accs[0]],
                 out_dtypes=[F32], stack=st_d, tm=1408, tn=1024, tk=512)
    dwg, dwu = _mm(tag + "_dwgu", [u, u], [da, db], nt=False, ta=True, acc_of=[0, 1], epilogue=lambda accs, _: accs,
                   out_dtypes=[F32, F32], stack=st_gu, tm=2048, tn=512, tk=1024)
    (du,) = _mm(tag + "_du", [da, db], [wg, wu], nt=True, epilogue=lambda accs, _: accs, out_dtypes=[F32],
                tm=1024, tn=1024, tk=1408)
    dh, dh_bf, dg = _rms_bwd(tag + "_dnorm", h, du, dh2, g)
    return dh, dh_bf, dg, dwg, dwu, dwd


def _softplus(x):
    e = jnp.exp(-jnp.abs(x))
    u = 1.0 + e
    log1p_e = jnp.where(u == 1.0, e, jnp.log(u) * (e / jnp.where(u == 1.0, 1.0, u - 1.0)))
    return jnp.maximum(x, 0.0) + log1p_e


def _row_spec(ts, width, colblock):
    return pl.BlockSpec((ts, width), lambda i: (i, colblock))


def _halo_before_spec(ts, width, colblock):
    r = ts // HALO
    return pl.BlockSpec((HALO, width), lambda i: (jnp.maximum(i * r - 1, 0), colblock))


def _halo_after_spec(ts, width, colblock, s_len):
    r = ts // HALO
    return pl.BlockSpec((HALO, width), lambda i: (jnp.minimum((i + 1) * r, s_len // HALO - 1), colblock))


def _const_spec(shape):
    nd = len(shape)
    return pl.BlockSpec(shape, lambda *_: (0,) * nd)


def _window_sum(e, win, roll):
    s, sh = e, 1
    while sh < win:
        s = s + roll(s, sh)
        sh *= 2
    return s


def _pool_center(ext, x, t, gi, win):
    sl = slice(gi * LANES, (gi + 1) * LANES)
    s = _window_sum(ext[:, sl], win, _roll_down)
    cnt = jnp.minimum(t + 1, win).astype(F32)
    return s[HALO:] / cnt - x[:, sl]


def _pool_fwd(name, proj, pw, scale):
    s_len = proj.shape[0]
    ts = _tile(s_len, 512, 8)
    width = len(POOL_WINDOWS) * LANES

    def body(x_ref, hb_ref, pw_ref, sc_ref, o_ref):
        i = pl.program_id(0)
        x = x_ref[...]
        ext = jnp.concatenate([jnp.where(i == 0, 0.0, hb_ref[...]), x], axis=0)
        t = i * ts + _iota((ts, 1), 0)
        for gi, win in enumerate(POOL_WINDOWS):
            sl = slice(gi * LANES, (gi + 1) * LANES)
            c = _pool_center(ext, x, t, gi, win)
            o_ref[:, sl] = (_dot(c, pw_ref[gi]) * sc_ref[:, sl]).astype(o_ref.dtype)

    return pl.pallas_call(
        body, name=name, grid=(s_len // ts,),
        in_specs=[_row_spec(ts, width, 0), _halo_before_spec(ts, width, 0), _const_spec(pw.shape), _const_spec(scale.shape)],
        out_specs=_row_spec(ts, width, 0),
        out_shape=jax.ShapeDtypeStruct((s_len, width), BF16),
        compiler_params=_cparams(("parallel",)),
    )(proj, proj, pw, scale)


def _pool_bwd(name, proj, dmixed, pw, scale):
    s_len = proj.shape[0]
    ts = _tile(s_len, 512, 8)
    n_tiles = s_len // ts
    width = len(POOL_WINDOWS) * LANES

    def body(x_ref, hb_ref, d_ref, da_ref, pw_ref, sc_ref, dx_ref, dpw_ref, dsc_ref):
        i = pl.program_id(0)
        x = x_ref[...]
        ext = jnp.concatenate([jnp.where(i == 0, 0.0, hb_ref[...]), x], axis=0)
        dout = d_ref[...]
        dext = jnp.concatenate([dout, jnp.where(i == n_tiles - 1, 0.0, da_ref[...])], axis=0)
        t = i * ts + _iota((ts, 1), 0)
        te = i * ts + _iota((ts + HALO, 1), 0)

        @pl.when(i == 0)
        def _():
            dpw_ref[...] = jnp.zeros_like(dpw_ref)
            dsc_ref[...] = jnp.zeros_like(dsc_ref)

        for gi, win in enumerate(POOL_WINDOWS):
            sl = slice(gi * LANES, (gi + 1) * LANES)
            c = _pool_center(ext, x, t, gi, win)
            o = _dot(c, pw_ref[gi])
            dsc_ref[:, sl] += jnp.sum(dout[:, sl] * o, axis=0, keepdims=True)
            do_ext = dext[:, sl] * sc_ref[:, sl]
            dc = _dot_nt(do_ext, pw_ref[gi])
            e = dc / jnp.minimum(te + 1, win).astype(F32)
            back = _window_sum(e, win, _roll_up)
            dx_ref[:, sl] = (back[:ts] - dc[:ts]).astype(dx_ref.dtype)
            dpw_ref[gi] += _dot(c.T, do_ext[:ts])

    return pl.pallas_call(
        body, name=name, grid=(n_tiles,),
        in_specs=[_row_spec(ts, width, 0), _halo_before_spec(ts, width, 0), _row_spec(ts, width, 0),
                  _halo_after_spec(ts, width, 0, s_len), _const_spec(pw.shape), _const_spec(scale.shape)],
        out_specs=[_row_spec(ts, width, 0), _const_spec(pw.shape), _const_spec(scale.shape)],
        out_shape=[jax.ShapeDtypeStruct((s_len, width), BF16), jax.ShapeDtypeStruct(pw.shape, F32),
                   jax.ShapeDtypeStruct(scale.shape, F32)],
        compiler_params=_cparams(("arbitrary",)),
    )(proj, proj, dmixed, dmixed, pw, scale)


def _conv_pre(ext, w_ref, b_ref):
    k_len = w_ref.shape[0]
    y = _roll_down(ext, k_len - 1) * w_ref[0:1, :]
    for k in range(1, k_len):
        y = y + _roll_down(ext, k_len - 1 - k) * w_ref[k:k + 1, :]
    return y + b_ref[...]


def _conv_fwd(name, proj, w, b):
    s_len = proj.shape[0]
    width = w.shape[1]
    ts = _tile(s_len, 512, 8)
    cb = C_XBC // width

    def body(x_ref, hb_ref, w_ref, b_ref, o_ref):
        i = pl.program_id(0)
        ext = jnp.concatenate([jnp.where(i == 0, 0.0, hb_ref[...]), x_ref[...]], axis=0)
        o_ref[...] = _silu(_conv_pre(ext, w_ref, b_ref)[HALO:])

    return pl.pallas_call(
        body, name=name, grid=(s_len // ts,),
        in_specs=[_row_spec(ts, width, cb), _halo_before_spec(ts, width, cb), _const_spec(w.shape), _const_spec(b.shape)],
        out_specs=_row_spec(ts, width, 0),
        out_shape=jax.ShapeDtypeStruct((s_len, width), F32),
        compiler_params=_cparams(("parallel",)),
    )(proj, proj, w, b)


def _conv_bwd(name, proj, dact, w, b):
    s_len = proj.shape[0]
    width = w.shape[1]
    k_len = w.shape[0]
    ts = _tile(s_len, 512, 8)
    n_tiles = s_len // ts
    cb = C_XBC // width

    def body(x_ref, hb_ref, ha_ref, d_ref, da_ref, w_ref, b_ref, dx_ref, dw_ref, db_ref):
        i = pl.program_id(0)
        last = i == n_tiles - 1
        ext = jnp.concatenate([jnp.where(i == 0, 0.0, hb_ref[...]), x_ref[...], jnp.where(last, 0.0, ha_ref[...])], axis=0)
        pre = _conv_pre(ext, w_ref, b_ref)[HALO:]
        dpre = jnp.concatenate([d_ref[...], jnp.where(last, 0.0, da_ref[...])], axis=0) * _dsilu(pre)

        @pl.when(i == 0)
        def _():
            dw_ref[...] = jnp.zeros_like(dw_ref)
            db_ref[...] = jnp.zeros_like(db_ref)

        dx = _roll_up(dpre, k_len - 1) * w_ref[0:1, :]
        for k in range(1, k_len):
            dx = dx + _roll_up(dpre, k_len - 1 - k) * w_ref[k:k + 1, :]
        dx_ref[...] = dx[:ts].astype(dx_ref.dtype)
        dtile = dpre[:ts]
        for k in range(k_len):
            xk = _roll_down(ext, k_len - 1 - k)[HALO:HALO + ts]
            dw_ref[k:k + 1, :] += jnp.sum(dtile * xk, axis=0, keepdims=True)
        db_ref[...] += jnp.sum(dtile, axis=0, keepdims=True)

    return pl.pallas_call(
        body, name=name, grid=(n_tiles,),
        in_specs=[_row_spec(ts, width, cb), _halo_before_spec(ts, width, cb), _halo_after_spec(ts, width, cb, s_len),
                  _row_spec(ts, width, 0), _halo_after_spec(ts, width, 0, s_len), _const_spec(w.shape), _const_spec(b.shape)],
        out_specs=[_row_spec(ts, width, 0), _const_spec(w.shape), _const_spec(b.shape)],
        out_shape=[jax.ShapeDtypeStruct((s_len, width), BF16), jax.ShapeDtypeStruct(w.shape, F32),
                   jax.ShapeDtypeStruct(b.shape, F32)],
        compiler_params=_cparams(("arbitrary",)),
    )(proj, proj, proj, dact, dact, w, b)


def _pair_cols(c0, c1, lo_half):
    return jnp.where(lo_half, c0, c1)


def _ssd_common(dtr_ref, bias_ref, alog_ref, acs_t_ref):
    chunk = dtr_ref.shape[0]
    xpre = dtr_ref[...] + bias_ref[...]
    dt = _softplus(xpre)
    a_neg = -jnp.exp(alog_ref[...])
    tri = _iota((chunk, chunk), 1) <= _iota((chunk, chunk), 0)
    a_cs = _dot3_left(tri.astype(BF16), dt * a_neg)
    acs_t_ref[...] = a_cs.T
    a_last = jnp.sum(jnp.where(_iota(a_cs.shape, 0) == chunk - 1, a_cs, 0.0), axis=0, keepdims=True)
    return xpre, dt, a_neg, tri, a_cs, a_last


def _ssd_specs(chunk, order):
    xs = pl.BlockSpec((chunk, 1024), lambda c: (order(c), 0))
    bm = pl.BlockSpec((chunk, 256), lambda c: (order(c), 4))
    cm = pl.BlockSpec((chunk, 256), lambda c: (order(c), 5))
    lanes = pl.BlockSpec((chunk, LANES), lambda c: (order(c), 0))
    return xs, bm, cm, lanes


def _ssd_fwd(name, xbc, dtr, dt_bias, a_log, d_skip):
    s_len = xbc.shape[0]
    chunk = SSD_CHUNK
    nc = s_len // chunk
    n_pairs = SSD_HEADS // 2

    def body(xs_ref, b_ref, c_ref, dtr_ref, bias_ref, alog_ref, dsk_ref, y_ref, st_ref, state_ref, acs_t_ref):
        @pl.when(pl.program_id(0) == 0)
        def _():
            state_ref[...] = jnp.zeros_like(state_ref)

        _, dt, _, tri, a_cs, a_last = _ssd_common(dtr_ref, bias_ref, alog_ref, acs_t_ref)
        lo_half = _iota((chunk, LANES), 1) < HEAD_DIM
        lo_lane = _iota((1, LANES), 1) < HEAD_DIM
        lo_row = _iota((LANES, 1), 0) < HEAD_DIM
        dsk = dsk_ref[...]
        for g in range(2):
            gsl = slice(g * LANES, (g + 1) * LANES)
            bg, cg = b_ref[:, gsl], c_ref[:, gsl]
            gmat = _dot_nt(cg, bg)
            for pr in range(n_pairs // 2):
                pair = g * (n_pairs // 2) + pr
                h0, h1 = 2 * pair, 2 * pair + 1
                psl = slice(pair * LANES, (pair + 1) * LANES)
                x2 = xs_ref[:, psl]
                acs0, acs1 = _col(a_cs, h0), _col(a_cs, h1)
                xdt = x2 * _pair_cols(_col(dt, h0), _col(dt, h1), lo_half)
                y2 = jnp.zeros((chunk, LANES), F32)
                for h, acs_c, hmask in ((h0, acs0, lo_half), (h1, acs1, ~lo_half)):
                    lam = jnp.where(tri, jnp.exp(jnp.minimum(acs_c - acs_t_ref[h:h + 1, :], 0.0)), 0.0)
                    y2 = y2 + _dot(gmat * lam, jnp.where(hmask, xdt, 0.0))
                s2 = state_ref[pair]
                st_ref[0, pair] = s2
                y2 = y2 + _pair_cols(jnp.exp(acs0), jnp.exp(acs1), lo_half) * _dot_nt(cg, s2)
                y_ref[:, psl] = y2 + _pair_cols(_col(dsk, h0), _col(dsk, h1), lo_lane) * x2
                al0, al1 = _col(a_last, h0), _col(a_last, h1)
                wl2 = _pair_cols(jnp.exp(al0 - acs0), jnp.exp(al1 - acs1), lo_half)
                state_ref[pair] = _pair_cols(jnp.exp(al0), jnp.exp(al1), lo_row) * s2 + _dot((xdt * wl2).T, bg)

    xs, bm, cm, lanes = _ssd_specs(chunk, lambda c: c)
    vec = _const_spec((1, LANES))
    return pl.pallas_call(
        body, name=name, grid=(nc,),
        in_specs=[xs, bm, cm, lanes, vec, vec, vec],
        out_specs=[xs, pl.BlockSpec((1, n_pairs, LANES, LANES), lambda c: (c, 0, 0, 0))],
        out_shape=[jax.ShapeDtypeStruct((s_len, 1024), F32), jax.ShapeDtypeStruct((nc, n_pairs, LANES, LANES), F32)],
        scratch_shapes=[pltpu.VMEM((n_pairs, LANES, LANES), F32), pltpu.VMEM((LANES, chunk), F32)],
        compiler_params=_cparams(("arbitrary",)),
    )(xbc, xbc, xbc, dtr, dt_bias, a_log, d_skip)


def _ssd_bwd(name, xbc, dtr, states, dy, dt_bias, a_log, d_skip):
    s_len = xbc.shape[0]
    chunk = SSD_CHUNK
    nc = s_len // chunk
    n_pairs = SSD_HEADS // 2
    rev = lambda c: nc - 1 - c

    def body(xs_ref, b_ref, c_ref, dtr_ref, dy_ref, sin_ref, bias_ref, alog_ref, dsk_ref,
             dxs_ref, db_ref, dc_ref, ddtr_ref, dbias_ref, dalog_ref, ddsk_ref, dstate_ref, acs_t_ref):
        @pl.when(pl.program_id(0) == 0)
        def _():
            dstate_ref[...] = jnp.zeros_like(dstate_ref)
            dbias_ref[...] = jnp.zeros_like(dbias_ref)
            dalog_ref[...] = jnp.zeros_like(dalog_ref)
            ddsk_ref[...] = jnp.zeros_like(ddsk_ref)

        xpre, dt, a_neg, tri, a_cs, a_last = _ssd_common(dtr_ref, bias_ref, alog_ref, acs_t_ref)
        lane = _iota((chunk, LANES), 1)
        lo_half = lane < HEAD_DIM
        lane1 = _iota((1, LANES), 1)
        lo_lane = lane1 < HEAD_DIM
        lo_row = _iota((LANES, 1), 0) < HEAD_DIM
        head_row = _iota((LANES, chunk), 0)
        sq_row, sq_col = _iota((chunk, chunk), 0), _iota((chunk, chunk), 1)
        before = (sq_row < sq_col).astype(BF16)
        dsk = dsk_ref[...]
        da_rows = jnp.zeros((LANES, chunk), F32)
        yo = jnp.zeros((chunk, LANES), F32)
        to = jnp.zeros((chunk, LANES), F32)
        vs = jnp.zeros((1, LANES), F32)
        ddt = jnp.zeros((chunk, LANES), F32)
        ddsk = jnp.zeros((1, LANES), F32)

        def half_sums(v):
            lo = jnp.sum(jnp.where(lo_half, v, 0.0), axis=1, keepdims=True)
            return lo, jnp.sum(v, axis=1, keepdims=True) - lo

        for g in range(2):
            gsl = slice(g * LANES, (g + 1) * LANES)
            bg, cg = b_ref[:, gsl], c_ref[:, gsl]
            gmat = _dot_nt(cg, bg)
            dgm = jnp.zeros((chunk, chunk), F32)
            dbg = jnp.zeros((chunk, LANES), F32)
            dcg = jnp.zeros((chunk, LANES), F32)
            for pr in range(n_pairs // 2):
                pair = g * (n_pairs // 2) + pr
                h0, h1 = 2 * pair, 2 * pair + 1
                psl = slice(pair * LANES, (pair + 1) * LANES)
                x2, dy2 = xs_ref[:, psl], dy_ref[:, psl]
                acs0, acs1 = _col(a_cs, h0), _col(a_cs, h1)
                dt2 = _pair_cols(_col(dt, h0), _col(dt, h1), lo_half)
                xdt = x2 * dt2
                al0, al1 = _col(a_last, h0), _col(a_last, h1)
                v2 = _pair_cols(jnp.exp(acs0), jnp.exp(acs1), lo_half)
                wl2 = _pair_cols(jnp.exp(al0 - acs0), jnp.exp(al1 - acs1), lo_half)
                s_in, ds2 = sin_ref[0, pair], dstate_ref[pair]
                y_off = v2 * _dot_nt(cg, s_in)
                dx_state = wl2 * _dot_nt(bg, ds2)
                dx2 = dx_state
                for h, acs_c, hmask in ((h0, acs0, lo_half), (h1, acs1, ~lo_half)):
                    lam = jnp.where(tri, jnp.exp(jnp.minimum(acs_c - acs_t_ref[h:h + 1, :], 0.0)), 0.0)
                    m = gmat * lam
                    dyh = jnp.where(hmask, dy2, 0.0)
                    dx2 = dx2 + _dot(m.T, dyh)
                    dml = _dot_nt(dyh, xdt) * lam
                    dgm = dgm + dml
                    crossed = jnp.where(sq_row >= sq_col, _dot3(dml * gmat, before), 0.0)
                    da_rows = jnp.where(head_row == h, jnp.sum(crossed, axis=0, keepdims=True), da_rows)
                vdy = v2 * dy2
                dcg = dcg + _dot(vdy, s_in)
                dbg = dbg + _dot(wl2 * xdt, ds2)
                yo0, yo1 = half_sums(dy2 * y_off)
                yo = jnp.where(lane == h0, yo0, jnp.where(lane == h1, yo1, yo))
                to0, to1 = half_sums(dx_state * xdt)
                to = jnp.where(lane == h0, to0, jnp.where(lane == h1, to1, to))
                prod = jnp.sum(ds2 * s_in, axis=1, keepdims=True)
                e0 = jnp.sum(jnp.where(lo_row, prod, 0.0), axis=0, keepdims=True)
                e1 = jnp.sum(prod, axis=0, keepdims=True) - e0
                vs = jnp.where(lane1 == h0, jnp.exp(al0) * e0, jnp.where(lane1 == h1, jnp.exp(al1) * e1, vs))
                q0, q1 = half_sums(dx2 * x2)
                ddt = jnp.where(lane == h0, q0, jnp.where(lane == h1, q1, ddt))
                dxs_ref[:, psl] = dx2 * dt2 + _pair_cols(_col(dsk, h0), _col(dsk, h1), lo_lane) * dy2
                s0, s1 = half_sums(dy2 * x2)
                ddsk = jnp.where(lane1 == h0, jnp.sum(s0, axis=0, keepdims=True),
                                 jnp.where(lane1 == h1, jnp.sum(s1, axis=0, keepdims=True), ddsk))
                dstate_ref[pair] = _pair_cols(jnp.exp(al0), jnp.exp(al1), lo_row) * ds2 + _dot(vdy.T, cg)
            dc_ref[:, gsl] = dcg + _dot(dgm, bg)
            db_ref[:, gsl] = dbg + _dot(dgm.T, cg)

        da = (da_rows.T + _dot3_left((sq_col >= sq_row).astype(BF16), yo)
              + _dot3_left((sq_col < sq_row).astype(BF16), to) + vs)
        ddt = ddt + da * a_neg
        dalog_ref[...] += jnp.sum(da * dt, axis=0, keepdims=True) * a_neg
        ddtr = jnp.where(lane < SSD_HEADS, ddt * jax.nn.sigmoid(xpre), 0.0)
        ddtr_ref[...] = ddtr
        dbias_ref[...] += jnp.sum(ddtr, axis=0, keepdims=True)
        ddsk_ref[...] += ddsk

    xs, bm, cm, lanes = _ssd_specs(chunk, rev)
    vec = _const_spec((1, LANES))
    st_in = pl.BlockSpec((1, n_pairs, LANES, LANES), lambda c: (rev(c), 0, 0, 0))
    bc_out = pl.BlockSpec((chunk, 256), lambda c: (rev(c), 0))
    return pl.pallas_call(
        body, name=name, grid=(nc,),
        in_specs=[xs, bm, cm, lanes, xs, st_in, vec, vec, vec],
        out_specs=[xs, bc_out, bc_out, lanes, vec, vec, vec],
        out_shape=[jax.ShapeDtypeStruct((s_len, 1024), F32), jax.ShapeDtypeStruct((s_len, 256), F32),
                   jax.ShapeDtypeStruct((s_len, 256), F32), jax.ShapeDtypeStruct((s_len, LANES), F32),
                   jax.ShapeDtypeStruct((1, LANES), F32), jax.ShapeDtypeStruct((1, LANES), F32),
                   jax.ShapeDtypeStruct((1, LANES), F32)],
        scratch_shapes=[pltpu.VMEM((n_pairs, LANES, LANES), F32), pltpu.VMEM((LANES, chunk), F32)],
        compiler_params=_cparams(("arbitrary",)),
    )(xbc, xbc, xbc, dtr, dy, states, dt_bias, a_log, d_skip)


def _gatenorm_fwd(name, y, proj, g):
    s_len = y.shape[0]
    ts = _tile(s_len, 512, 8)
    gw = 512

    def body(y_ref, z_ref, g_ref, o_ref):
        yg = y_ref[...] * _silu(z_ref[...])
        rstd = lax.rsqrt(jnp.mean(yg * yg, axis=-1, keepdims=True) + RMS_EPS)
        o_ref[...] = (yg * rstd * g_ref[...]).astype(o_ref.dtype)

    return pl.pallas_call(
        body, name=name, grid=(2, s_len // ts),
        in_specs=[pl.BlockSpec((ts, gw), lambda gi, i: (i, gi)), pl.BlockSpec((ts, gw), lambda gi, i: (i, C_Z // gw + gi)),
                  pl.BlockSpec((1, gw), lambda gi, i: (0, gi))],
        out_specs=pl.BlockSpec((ts, gw), lambda gi, i: (i, gi)),
        out_shape=jax.ShapeDtypeStruct((s_len, 2 * gw), BF16),
        compiler_params=_cparams(("parallel", "parallel")),
    )(y, proj, g)


def _gatenorm_bwd(name, y, proj, g, dmixed):
    s_len = y.shape[0]
    ts = _tile(s_len, 512, 8)
    gw = 512

    def body(y_ref, z_ref, g_ref, d_ref, dy_ref, dz_ref, dg_ref):
        yv, z = y_ref[...], z_ref[...]
        sz = _silu(z)
        yg = yv * sz
        rstd = lax.rsqrt(jnp.mean(yg * yg, axis=-1, keepdims=True) + RMS_EPS)
        n = yg * rstd
        dn = d_ref[...] * g_ref[...]
        dyg = rstd * (dn - n * jnp.mean(dn * n, axis=-1, keepdims=True))
        dy_ref[...] = dyg * sz
        dz_ref[...] = (dyg * yv * _dsilu(z)).astype(dz_ref.dtype)

        @pl.when(pl.program_id(1) == 0)
        def _():
            dg_ref[...] = jnp.zeros_like(dg_ref)

        dg_ref[...] += jnp.sum(d_ref[...] * n, axis=0, keepdims=True)

    grp = pl.BlockSpec((ts, gw), lambda gi, i: (i, gi))
    vec = pl.BlockSpec((1, gw), lambda gi, i: (0, gi))
    return pl.pallas_call(
        body, name=name, grid=(2, s_len // ts),
        in_specs=[grp, pl.BlockSpec((ts, gw), lambda gi, i: (i, C_Z // gw + gi)), vec,
                  pl.BlockSpec((ts, gw), lambda gi, i: (i, 1 + gi))],
        out_specs=[grp, grp, vec],
        out_shape=[jax.ShapeDtypeStruct((s_len, 2 * gw), F32), jax.ShapeDtypeStruct((s_len, 2 * gw), BF16),
                   jax.ShapeDtypeStruct((1, 2 * gw), F32)],
        compiler_params=_cparams(("parallel", "arbitrary")),
    )(y, proj, g, dmixed)


def _attn_scores(qh, kblk, mask, ustrict, r):
    z = _dot_nt(qh, kblk)
    sp = _softplus(-jnp.abs(z))
    ls = jnp.minimum(z, 0.0) - sp
    lm_raw = jnp.minimum(-z, 0.0) - sp
    lm = jnp.where(mask, lm_raw, 0.0)
    suffix = _dot3(lm, ustrict)
    w = jnp.where(mask, jnp.exp(ls + suffix + r), 0.0)
    return ls, lm_raw, lm, w


def _attn_tiles(s_len):
    tk = ATTN_BLOCK
    return (2 * tk if s_len % (2 * tk) == 0 else tk), tk


def _attn_specs(tq, s_len):
    qcol, kcol, vcol = C_Q // LANES, C_K // LANES, C_V // LANES
    q = pl.BlockSpec((tq, LANES), lambda p, i: (i, qcol + p))
    k = pl.BlockSpec((s_len, LANES), lambda p, i: (0, kcol + p))
    v = pl.BlockSpec((s_len, LANES), lambda p, i: (0, vcol + p))
    return q, k, v


def _attn_fwd(name, proj):
    s_len = proj.shape[0]
    tq, tk = _attn_tiles(s_len)
    n_slabs = 4

    def body(q_ref, k_ref, v_ref, o_ref):
        qi = pl.program_id(1)
        q2 = q_ref[...] * (HEAD_DIM ** -0.5)
        lo = _iota((tq, LANES), 1) < HEAD_DIM
        lo_k = _iota((tk, LANES), 1) < HEAD_DIM
        heads = ((jnp.where(lo, q2, 0.0).astype(BF16), lo_k), (jnp.where(lo, 0.0, q2).astype(BF16), ~lo_k))
        ustrict = (_iota((tk, tk), 0) > _iota((tk, tk), 1)).astype(BF16)
        q_pos = qi * tq + _iota((tq, tk), 0)
        k_off = _iota((tq, tk), 1)

        def step(carry):
            kb, _, r0, r1, acc = carry
            rows = pl.ds(pl.multiple_of(kb * tk, tk), tk)
            kblk, vblk = k_ref[rows, :].astype(BF16), v_ref[rows, :]
            mask = kb * tk + k_off < q_pos
            new_r = []
            for (qh, hmask), r in zip(heads, (r0, r1)):
                _, _, lm, w = _attn_scores(qh, kblk, mask, ustrict, r)
                acc = acc + _dot(w, jnp.where(hmask, vblk, 0.0))
                new_r.append(r + jnp.sum(lm, axis=1, keepdims=True))
            go = (jnp.maximum(jnp.max(new_r[0]), jnp.max(new_r[1])) > EXP_UNDERFLOW).astype(jnp.int32)
            return kb - 1, go, new_r[0], new_r[1], acc

        zero = jnp.zeros((tq, 1), F32)
        init = ((qi + 1) * (tq // tk) - 1, jnp.int32(1), zero, zero, jnp.zeros((tq, LANES), F32))
        o_ref[...] = lax.while_loop(lambda c: (c[0] >= 0) & (c[1] > 0), step, init)[4]

    q, k, v = _attn_specs(tq, s_len)
    return pl.pallas_call(
        body, name=name, grid=(n_slabs, s_len // tq),
        in_specs=[q, k, v], out_specs=pl.BlockSpec((tq, LANES), lambda p, i: (i, p)),
        out_shape=jax.ShapeDtypeStruct((s_len, n_slabs * LANES), F32),
        compiler_params=_cparams(("parallel", "arbitrary")),
    )(proj, proj, proj)


def _attn_bwd(name, proj, dmixed):
    s_len = proj.shape[0]
    tq, tk = _attn_tiles(s_len)
    n_slabs = 4
    scale = HEAD_DIM ** -0.5

    def body(q_ref, k_ref, v_ref, do_ref, dq_ref, dk_ref, dv_ref, dk_acc, dv_acc, r_hist):
        qi = pl.program_id(1)

        @pl.when(qi == 0)
        def _():
            dk_acc[...] = jnp.zeros_like(dk_acc)
            dv_acc[...] = jnp.zeros_like(dv_acc)

        q2 = q_ref[...] * scale
        do2 = do_ref[...]
        lo = _iota((tq, LANES), 1) < HEAD_DIM
        lo_k = _iota((tk, LANES), 1) < HEAD_DIM
        heads = ((jnp.where(lo, q2, 0.0).astype(BF16), jnp.where(lo, do2, 0.0).astype(BF16), lo_k),
                 (jnp.where(lo, 0.0, q2).astype(BF16), jnp.where(lo, 0.0, do2).astype(BF16), ~lo_k))
        row, col = _iota((tk, tk), 0), _iota((tk, tk), 1)
        ustrict = (row > col).astype(BF16)
        earlier = (row < col).astype(BF16)
        q_pos = qi * tq + _iota((tq, tk), 0)
        k_off = _iota((tq, tk), 1)
        top = (qi + 1) * (tq // tk) - 1
        zero = jnp.zeros((tq, 1), F32)

        def scan(carry):
            kb, _, r0, r1 = carry
            kblk = k_ref[pl.ds(pl.multiple_of(kb * tk, tk), tk), :].astype(BF16)
            mask = kb * tk + k_off < q_pos
            r_hist[kb] = jnp.where(lo, r0, r1)
            new_r = []
            for (qh, _, _), r in zip(heads, (r0, r1)):
                z = _dot_nt(qh, kblk)
                lm = jnp.where(mask, jnp.minimum(-z, 0.0) - _softplus(-jnp.abs(z)), 0.0)
                new_r.append(r + jnp.sum(lm, axis=1, keepdims=True))
            go = (jnp.maximum(jnp.max(new_r[0]), jnp.max(new_r[1])) > EXP_UNDERFLOW).astype(jnp.int32)
            return kb - 1, go, new_r[0], new_r[1]

        first = lax.while_loop(lambda c: (c[0] >= 0) & (c[1] > 0), scan, (top, jnp.int32(1), zero, zero))[0] + 1

        def step(carry):
            kb, p0, p1, dq = carry
            rows = pl.ds(pl.multiple_of(kb * tk, tk), tk)
            kf, vblk = k_ref[rows, :], v_ref[rows, :].astype(BF16)
            kblk = kf.astype(BF16)
            mask = kb * tk + k_off < q_pos
            rr = r_hist[kb]
            dk_blk = jnp.zeros((tk, LANES), F32)
            dv_blk = jnp.zeros((tk, LANES), F32)
            new_p = []
            for (qh, doh, hmask), r, p in zip(heads, (_col(rr, 0), _col(rr, HEAD_DIM)), (p0, p1)):
                ls, lm_raw, _, w = _attn_scores(qh, kblk, mask, ustrict, r)
                ew = _dot_nt(doh, vblk) * w
                before = p + _dot3(ew, earlier)
                dz = jnp.where(mask, ew * jnp.exp(lm_raw) - jnp.exp(ls) * before, 0.0)
                dq = dq + _dot(dz, jnp.where(hmask, kf, 0.0))
                dk_blk = dk_blk + _dot(dz.T, qh)
                dv_blk = dv_blk + _dot(w.T, doh)
                new_p.append(p + jnp.sum(ew, axis=1, keepdims=True))
            dk_acc[rows, :] += dk_blk
            dv_acc[rows, :] += dv_blk
            return kb + 1, new_p[0], new_p[1], dq

        dq = lax.while_loop(lambda c: c[0] <= top, step, (first, zero, zero, jnp.zeros((tq, LANES), F32)))[3]
        dq_ref[...] = (dq * scale).astype(dq_ref.dtype)

        @pl.when(qi == pl.num_programs(1) - 1)
        def _():
            dk_ref[...] = dk_acc[...].astype(dk_ref.dtype)
            dv_ref[...] = dv_acc[...].astype(dv_ref.dtype)

    q, k, v = _attn_specs(tq, s_len)
    blk = pl.BlockSpec((tq, LANES), lambda p, i: (i, p))
    full = pl.BlockSpec((s_len, LANES), lambda p, i: (0, p))
    shape = jax.ShapeDtypeStruct((s_len, n_slabs * LANES), BF16)
    return pl.pallas_call(
        body, name=name, grid=(n_slabs, s_len // tq),
        in_specs=[q, k, v, pl.BlockSpec((tq, LANES), lambda p, i: (i, 1536 // LANES + p))],
        out_specs=[blk, full, full], out_shape=[shape, shape, shape],
        scratch_shapes=[pltpu.VMEM((s_len, LANES), F32), pltpu.VMEM((s_len, LANES), F32),
                        pltpu.VMEM((s_len // tk, tq, LANES), F32)],
        compiler_params=_cparams(("parallel", "arbitrary")),
    )(proj, proj, proj, dmixed)


def _ident(accs, _):
    return accs


def _mixer_fwd(tag, h, p, nxt=()):
    u = _rms_fwd(tag + "_norm", h, p["mix_norm"])
    proj, *got_in = _mm(tag + "_in", [u], [p["w_main"]], nt=False, epilogue=_ident, out_dtypes=[F32], gather=nxt[:1],
                        tk=2048)
    (dtr,) = _mm(tag + "_indt", [u], [p["w_dt"]], nt=False, epilogue=_ident, out_dtypes=[F32], tk=2048)
    pool_out = _pool_fwd(tag + "_pool", proj, p["pool_w"], p["pool_scale"])
    xbc = _conv_fwd(tag + "_conv", proj, p["conv_w"], p["conv_b"])
    y, states = _ssd_fwd(tag + "_ssd", xbc, dtr, p["dt_bias"], p["a_log"], p["d_skip"])
    ssd_out = _gatenorm_fwd(tag + "_gate", y, proj, p["ssd_norm"])
    attn = _attn_fwd(tag + "_attn", proj)
    mixed = jnp.concatenate([pool_out, ssd_out, attn.astype(BF16)], axis=1)
    h2, *got_out = _mm(tag + "_out", [mixed], [p["w_out"]], nt=False, extras=[h],
                       epilogue=lambda accs, ex: [ex[0] + accs[0]], out_dtypes=[F32], gather=nxt[1:], tk=2048)
    return h2, (h, u, proj, dtr, xbc, y, states, mixed), got_in + got_out


def _mixer_bwd(tag, saved, p, dh2, dh2_bf, stack=None):
    h, u, proj, dtr, xbc, y, states, mixed = saved
    (dmixed,) = _mm(tag + "_dmix", [dh2_bf], [p["w_out"]], nt=True, epilogue=_ident, out_dtypes=[F32], tk=2048)
    (dw_out,) = _mm(tag + "_dwout", [mixed], [dh2_bf], nt=False, ta=True, epilogue=_ident, out_dtypes=[F32], stack=stack)
    dpool_in, dpool_w, dpool_scale = _pool_bwd(tag + "_dpool", proj, dmixed, p["pool_w"], p["pool_scale"])
    dy, dz, dssd_norm = _gatenorm_bwd(tag + "_dgate", y, proj, p["ssd_norm"], dmixed)
    dxs, dbm, dcm, ddtr, ddt_bias, da_log, dd_skip = _ssd_bwd(tag + "_dssd", xbc, dtr, states, dy, p["dt_bias"],
                                                             p["a_log"], p["d_skip"])
    dxbc, dconv_w, dconv_b = _conv_bwd(tag + "_dconv", proj, jnp.concatenate([dxs, dbm, dcm], axis=1), p["conv_w"],
                                       p["conv_b"])
    dq, dk, dv = _attn_bwd(tag + "_dattn", proj, dmixed)
    dproj = jnp.concatenate([dpool_in, dz, dxbc, dq, dk, dv], axis=1)
    ddtr_bf = ddtr.astype(BF16)
    (dw_main,) = _mm(tag + "_dwin", [u], [dproj], nt=False, ta=True, epilogue=_ident, out_dtypes=[F32])
    (dw_dt,) = _mm(tag + "_dwdt", [u], [ddtr_bf], nt=False, ta=True, epilogue=_ident, out_dtypes=[F32])
    (du_dt,) = _mm(tag + "_dudt", [ddtr_bf], [p["w_dt"]], nt=True, epilogue=_ident, out_dtypes=[F32], tn=1024)
    (du,) = _mm(tag + "_du", [dproj], [p["w_main"]], nt=True, extras=[du_dt],
                epilogue=lambda accs, ex: [accs[0] + ex[0]], out_dtypes=[F32], tm=1024, tn=1024, tk=1536)
    dh, dh_bf, dg = _rms_bwd(tag + "_dnorm", h, du, dh2, p["mix_norm"])
    grads = dict(mix_norm=dg, w_main=dw_main, w_dt=dw_dt, pool_w=dpool_w, pool_scale=dpool_scale, conv_w=dconv_w,
                 conv_b=dconv_b, dt_bias=ddt_bias, a_log=da_log, d_skip=dd_skip, ssd_norm=dssd_norm, w_out=dw_out)
    return dh, dh_bf, grads


def _axes():
    return lax.axis_index("x"), lax.axis_index("y"), lax.axis_index("c")


def _any_specs(n):
    return [pl.BlockSpec(memory_space=pl.ANY) for _ in range(n)]


def _remote(src, dst, send, recv, k, dev):
    return pltpu.make_async_remote_copy(src_ref=src, dst_ref=dst, send_sem=send.at[k], recv_sem=recv.at[k],
                                        device_id=dev, device_id_type=MESH_ID)


def _gather_copies(srcs, outs, send, recv, loc):
    n = len(srcs)
    x, y, c = _axes()
    me = 2 * x + y
    peers = [(1 - x, y), (x, 1 - y), (1 - x, 1 - y)]
    local = [pltpu.make_async_copy(srcs[a], outs[a].at[me], loc.at[a]) for a in range(n)]
    sent = [_remote(srcs[a], outs[a].at[me], send, recv, 3 * a + k, (px, py, c))
            for a in range(n) for k, (px, py) in enumerate(peers)]
    received = [_remote(srcs[a], outs[a].at[2 * px + py], send, recv, 3 * a + k, (px, py, c))
                for a in range(n) for k, (px, py) in enumerate(peers)]
    return local, sent, received


def _gather_start(local, sent, received):
    for cp in local + sent:
        cp.start()


def _gather_finish(local, sent, received):
    for cp in received:
        cp.wait_recv()
    for cp in sent:
        cp.wait_send()
    for cp in local:
        cp.wait()


def _gather_chips(name, arrs):
    n = len(arrs)

    def body(*refs):
        copies = _gather_copies(refs[:n], refs[n:2 * n], *refs[2 * n:])
        _gather_start(*copies)
        _gather_finish(*copies)

    return pl.pallas_call(
        body, name=name, in_specs=_any_specs(n), out_specs=_any_specs(n),
        out_shape=[jax.ShapeDtypeStruct((4,) + a.shape, a.dtype) for a in arrs],
        scratch_shapes=[pltpu.SemaphoreType.DMA((3 * n,)), pltpu.SemaphoreType.DMA((3 * n,)), pltpu.SemaphoreType.DMA((n,))],
    )(*arrs)


def _half_view(kind, ref, hc):
    return ref.at[:, pl.ds(2 * hc, 2)] if kind == "win" else ref.at[pl.ds(2 * hc, 2)]


def _half_shape(kind, shape):
    return (shape[0], 2) + tuple(shape[2:]) if kind == "win" else (2,) + tuple(shape[1:])


def _shard_view(kind, ref, j):
    if kind == "col":
        w = ref.shape[2] // 4
        return ref.at[:, :, pl.ds(pl.multiple_of(j * w, LANES), w)]
    if kind == "row":
        r = ref.shape[1] // 4
        return ref.at[:, pl.ds(pl.multiple_of(j * r, 16), r), :]
    return ref.at[j]


def _shard_shape(kind, hshape):
    if kind == "col":
        return (2, hshape[1], hshape[2] // 4)
    if kind == "row":
        return (2, hshape[1] // 4, hshape[2])
    return tuple(hshape[1:])


def _rs_pair(name, kinds, grads):
    n = len(grads)

    def body(*refs):
        g_refs, o_refs = refs[:n], refs[n:2 * n]
        send, recv = refs[2 * n:]
        x, y, c = _axes()
        cps = [_remote(_half_view(kinds[a], g_refs[a], 1 - c), o_refs[a], send, recv, a, (x, y, 1 - c)) for a in range(n)]
        for cp in cps:
            cp.start()
        for cp in cps:
            cp.wait()

    return pl.pallas_call(
        body, name=name, in_specs=_any_specs(n), out_specs=_any_specs(n),
        out_shape=[jax.ShapeDtypeStruct(_half_shape(k, g.shape), g.dtype) for k, g in zip(kinds, grads)],
        scratch_shapes=[pltpu.SemaphoreType.DMA((n,)), pltpu.SemaphoreType.DMA((n,))],
    )(*grads)


def _rs_chips(name, kinds, halves):
    n = len(halves)

    def body(*refs):
        h_refs, o_refs = refs[:n], refs[n:2 * n]
        send, recv = refs[2 * n:]
        x, y, c = _axes()
        peers = [(1 - x, y), (x, 1 - y), (1 - x, 1 - y)]
        cps = [_remote(_shard_view(kinds[a], h_refs[a], 2 * px + py), o_refs[a].at[k], send, recv, 3 * a + k, (px, py, c))
               for a in range(n) for k, (px, py) in enumerate(peers)]
        for cp in cps:
            cp.start()
        for cp in cps:
            cp.wait()

    return pl.pallas_call(
        body, name=name, in_specs=_any_specs(n), out_specs=_any_specs(n),
        out_shape=[jax.ShapeDtypeStruct((3,) + _shard_shape(k, h.shape), h.dtype) for k, h in zip(kinds, halves)],
        scratch_shapes=[pltpu.SemaphoreType.DMA((3 * n,)), pltpu.SemaphoreType.DMA((3 * n,))],
    )(*halves)


def _swap_pair(name, parts):
    n = len(parts)

    def body(*refs):
        t_refs, o_refs = refs[:n], refs[n:2 * n]
        send, recv = refs[2 * n:]
        x, y, c = _axes()
        cps = [_remote(t_refs[a], o_refs[a], send, recv, a, (x, y, 1 - c)) for a in range(n)]
        for cp in cps:
            cp.start()
        for cp in cps:
            cp.wait()

    return pl.pallas_call(
        body, name=name, in_specs=_any_specs(n), out_specs=_any_specs(n),
        out_shape=[jax.ShapeDtypeStruct(t.shape, t.dtype) for t in parts],
        scratch_shapes=[pltpu.SemaphoreType.DMA((n,)), pltpu.SemaphoreType.DMA((n,))],
    )(*parts)


def _esum(name, grid, block, ins, outs, where):
    n_in = len(ins)

    def body(s_ref, *refs):
        tot = refs[0][...].astype(F32)
        for r in refs[1:n_in]:
            tot = tot + r[...].astype(F32)
        for o in refs[n_in:]:
            o[...] = tot.astype(o.dtype)

    spec = lambda nd, imap: pl.BlockSpec((None,) * (nd - 2) + tuple(block), imap)
    return pl.pallas_call(
        body, name=name,
        grid_spec=pltpu.PrefetchScalarGridSpec(
            num_scalar_prefetch=1, grid=grid,
            in_specs=[spec(a.ndim, m) for a, m in ins], out_specs=[spec(len(s), m) for s, _, m in outs]),
        out_shape=[jax.ShapeDtypeStruct(s, dt) for s, dt, _ in outs],
        compiler_params=_cparams(("parallel",) * len(grid)),
    )(where, *[a for a, _ in ins])


def _sum_pair(name, kind, g, r1, where):
    hshape = _half_shape(kind, g.shape)
    if kind == "win":
        block = (_tile(g.shape[2], 512, 16), g.shape[3])
        grid = (g.shape[0], 2, g.shape[2] // block[0])
        gmap = lambda s4, a, i, s: (s4, 2 * s[0] + a, i, 0)
        hmap = lambda s4, a, i, s: (s4, a, i, 0)
    elif kind == "col":
        block = (_tile(g.shape[1], 512, 16), g.shape[2] // N_CHIPS)
        grid = (2, g.shape[1] // block[0], N_CHIPS)
        gmap = lambda a, i, j, s: (2 * s[0] + a, i, j)
        hmap = lambda a, i, j, s: (a, i, j)
        wire = (N_CHIPS, 2, g.shape[1], block[1])
        return _esum(name, grid, block, [(g, gmap), (r1, hmap)],
                     [(hshape, F32, hmap), (wire, BF16, lambda a, i, j, s: (j, a, i, 0))], where)
    else:
        block = (_tile(g.shape[1], 512, 16), _tile(g.shape[2], 2048))
        grid = (2, g.shape[1] // block[0], g.shape[2] // block[1])
        gmap = lambda a, i, j, s: (2 * s[0] + a, i, j)
        hmap = lambda a, i, j, s: (a, i, j)
    return _esum(name, grid, block, [(g, gmap), (r1, hmap)], [(hshape, F32, hmap), (hshape, BF16, hmap)], where)


def _sum_chips(name, kind, h32, r2, where):
    tshape = _shard_shape(kind, h32.shape)
    if kind == "col":
        block = (_tile(tshape[1], 512, 16), tshape[2])
        hmap = lambda a, i, s: (a, i, s[1])
    elif kind == "row":
        block = (_tile(tshape[1], 512, 16), tshape[2])
        nb = tshape[1] // block[0]
        hmap = lambda a, i, s: (a, s[1] * nb + i, 0)
    else:
        block = (_tile(tshape[1], 512, 16), tshape[2])
        hmap = lambda a, i, s: (s[1], a, i, 0)
    grid = (2, tshape[1] // block[0])
    tmap = lambda a, i, s: (a, i, 0)
    rmap = lambda k: (lambda a, i, s: (k, a, i, 0))
    return _esum(name, grid, block, [(h32, hmap)] + [(r2, rmap(k)) for k in range(3)], [(tshape, F32, tmap)], where)[0]


def _allreduce_small(name, vec):
    rows_n = vec.shape[0]

    def body(x_ref, sum_ref, all_ref, send, recv, local_sem):
        x, y, c = _axes()
        me, sibling = (x, y, c), (x, y, 1 - c)
        chips = [(1 - x, y), (x, 1 - y), (1 - x, 1 - y)]

        def rows(px, py, pc):
            return all_ref.at[pl.ds(pl.multiple_of((4 * px + 2 * py + pc) * rows_n, 8), rows_n), :]

        def copy(k, block, to, src=None):
            return _remote(rows(*block) if src is None else src, rows(*block), send, recv, k, to)

        mine = pltpu.make_async_copy(x_ref, rows(*me), local_sem)
        mine.start()
        first = [copy(0, me, sibling, src=x_ref)] + [copy(1 + j, me, (*chip, c), src=x_ref) for j, chip in enumerate(chips)]
        for cp in first:
            cp.start()
        passed = [copy(4 + j, (*chip, c), sibling) for j, chip in enumerate(chips)]
        for j, chip in enumerate(chips):
            copy(1 + j, (*chip, c), me).wait_recv()
            passed[j].start()
        copy(0, sibling, me).wait_recv()
        for j, chip in enumerate(chips):
            copy(4 + j, (*chip, 1 - c), me).wait_recv()
        for cp in first + passed:
            cp.wait_send()
        mine.wait()
        tot = all_ref[0:rows_n, :]
        for d in range(1, 8):
            tot = tot + all_ref[d * rows_n:(d + 1) * rows_n, :]
        sum_ref[...] = tot

    vm = pl.BlockSpec(memory_space=pltpu.VMEM)
    return pl.pallas_call(
        body, name=name, in_specs=[vm], out_specs=[vm, vm],
        out_shape=[jax.ShapeDtypeStruct(vec.shape, F32), jax.ShapeDtypeStruct((8 * rows_n, LANES), F32)],
        scratch_shapes=[pltpu.SemaphoreType.DMA((7,)), pltpu.SemaphoreType.DMA((7,)), pltpu.SemaphoreType.DMA],
        compiler_params=pltpu.CompilerParams(vmem_limit_bytes=VMEM_LIMIT),
    )(vec)[0]


def _adamw(name, w, g, m, v):
    shape = w.shape
    rows_n, cols = shape[-2], shape[-1]
    lead = math.prod(shape[:-2])
    tr = _tile(rows_n, 256, 8)

    def body(w_ref, g_ref, m_ref, v_ref, d_ref, m2_ref, v2_ref):
        gv = g_ref[...]
        m2 = ADAM_B1 * m_ref[...] + (1.0 - ADAM_B1) * gv
        v2 = ADAM_B2 * v_ref[...] + (1.0 - ADAM_B2) * jnp.square(gv)
        m_hat = m2 / (1.0 - ADAM_B1 ** ADAM_STEP)
        v_hat = v2 / (1.0 - ADAM_B2 ** ADAM_STEP)
        d_ref[...] = -ADAM_LR * (m_hat / (jnp.sqrt(v_hat) + ADAM_EPS) + ADAM_WD * w_ref[...])
        m2_ref[...] = m2
        v2_ref[...] = v2

    spec = pl.BlockSpec((None, tr, cols), lambda l, i: (l, i, 0))
    flat = (lead, rows_n, cols)
    outs = pl.pallas_call(
        body, name=name, grid=(lead, rows_n // tr), in_specs=[spec] * 4, out_specs=[spec] * 3,
        out_shape=[jax.ShapeDtypeStruct(flat, F32)] * 3,
        compiler_params=_cparams(("parallel", "parallel")),
    )(*[t.reshape(flat) for t in (w, g, m, v)])
    return [o.reshape(shape) for o in outs]


WEIGHTS = ("ffn1_norm", "ffn1_w_gate", "ffn1_w_up", "ffn1_w_down", "mix_norm", "w_in", "pool_w", "pool_scale", "conv_w",
           "conv_b", "dt_bias", "a_log", "d_skip", "ssd_norm", "w_out", "ffn2_norm", "ffn2_w_gate", "ffn2_w_up",
           "ffn2_w_down", "final_norm")
BIG = {"ffn1_w_gate": "col", "ffn1_w_up": "col", "ffn1_w_down": "row", "w_in": "win", "w_out": "row",
       "ffn2_w_gate": "col", "ffn2_w_up": "col", "ffn2_w_down": "row"}
SMALL = tuple(n for n in WEIGHTS if n not in BIG and n != "conv_w")
REF_DT = 3072


def _pack(parts):
    flat = jnp.concatenate([p.reshape(-1) for p in parts])
    rows_n = -(-flat.shape[0] // (8 * LANES)) * 8
    return jnp.pad(flat, (0, rows_n * LANES - flat.shape[0])).reshape(rows_n, LANES)


def _unpack(block, shapes):
    flat, out, at = block.reshape(-1), [], 0
    for s in shapes:
        n = math.prod(s)
        out.append(flat[at:at + n].reshape(s))
        at += n
    return out


def _train_step(a):
    depth = a["ffn1_norm"].shape[0]
    x_id, y_id, c_id = _axes()
    chip = 2 * x_id + y_id
    where = jnp.stack([c_id, chip]).astype(jnp.int32)

    big = list(BIG)
    shards = lambda l: [a[n][l].astype(BF16) for n in big]
    join = lambda n, g: jnp.concatenate([g[s] for s in range(N_CHIPS)], axis=0 if BIG[n] == "row" else 1)
    *first, conv4 = _gather_chips("gather_layer0", shards(0) + [a["conv_w"]])
    conv_w = jnp.concatenate([conv4[s] for s in range(N_CHIPS)], axis=2)
    full = [None] * depth
    full[0] = {n: join(n, g) for n, g in zip(big, first)}
    heads128 = lambda v: jnp.pad(v, ((0, 0), (0, LANES - SSD_HEADS)))
    dt_bias, a_log, d_skip = heads128(a["dt_bias"]), heads128(a["a_log"]), heads128(a["d_skip"])

    def mixer_params(l):
        w_in = full[l]["w_in"]
        w_main = jnp.concatenate([w_in[:, :REF_DT], w_in[:, REF_DT + SSD_HEADS:]], axis=1)
        w_dt = jnp.pad(w_in[:, REF_DT:REF_DT + SSD_HEADS], ((0, 0), (0, LANES - SSD_HEADS)))
        return dict(mix_norm=a["mix_norm"][l][None], w_main=w_main, w_dt=w_dt, pool_w=a["pool_w"][l],
                    pool_scale=a["pool_scale"][l][None], conv_w=conv_w[l], conv_b=a["conv_b"][l][None],
                    dt_bias=dt_bias[l][None], a_log=a_log[l][None], d_skip=d_skip[l][None],
                    ssd_norm=a["ssd_norm"][l][None], w_out=full[l]["w_out"])

    def ffn_params(l, which):
        return (a[which + "_norm"][l][None], full[l][which + "_w_gate"], full[l][which + "_w_up"], full[l][which + "_w_down"])

    h = a["x"][0]
    saved, mixer_p = [], []
    for l in range(depth):
        nxt = dict(zip(big, shards(l + 1))) if l + 1 < depth else None
        pick = lambda *names: tuple(nxt[n] for n in names) if nxt else ()
        ffn1, mix, ffn2 = ("ffn1_w_gate", "ffn1_w_up", "ffn1_w_down"), ("w_in", "w_out"), ("ffn2_w_gate", "ffn2_w_up", "ffn2_w_down")
        mixer_p.append(mixer_params(l))
        h, s1, g1 = _ffn_fwd(f"l{l}_ffn1", h, *ffn_params(l, "ffn1"), nxt=pick(*ffn1))
        h, sm, gm = _mixer_fwd(f"l{l}_mix", h, mixer_p[l], nxt=pick(*mix))
        h, s2, g2 = _ffn_fwd(f"l{l}_ffn2", h, *ffn_params(l, "ffn2"), nxt=pick(*ffn2))
        saved.append((s1, sm, s2))
        if nxt:
            full[l + 1] = {n: join(n, g) for n, g in zip(ffn1 + mix + ffn2, g1 + gm + g2)}
    loss_part, dh, dh_bf, dfinal = _loss_head("loss_head", h, a["final_norm"][None], a["loss_target"][0])

    bufs = {n: lax.empty((depth,) + tuple(full[0][n].shape), F32) for n in big if n != "w_in"}
    small = {n: [None] * depth for n in SMALL if n != "final_norm"}
    small["conv_w"] = [None] * depth
    dw_in = [None] * depth
    for l in reversed(range(depth)):
        s1, sm, s2 = saved[l]
        names = ["ffn2_w_gate", "ffn2_w_up", "ffn2_w_down"]
        dh, dh_bf, small["ffn2_norm"][l], *new = _ffn_bwd(f"l{l}_ffn2", s2, *ffn_params(l, "ffn2"), dh, dh_bf,
                                                          stack=([bufs[n] for n in names], l))
        bufs.update(zip(names, new))
        dh, dh_bf, g = _mixer_bwd(f"l{l}_mix", sm, mixer_p[l], dh, dh_bf, stack=([bufs["w_out"]], l))
        bufs["w_out"] = g["w_out"]
        for n in ("mix_norm", "pool_w", "pool_scale", "conv_w", "conv_b", "ssd_norm"):
            small[n][l] = g[n]
        for n in ("dt_bias", "a_log", "d_skip"):
            small[n][l] = g[n][:, :SSD_HEADS]
        dw_in[l] = jnp.concatenate([g["w_main"][:, :REF_DT], g["w_dt"][:, :SSD_HEADS], g["w_main"][:, REF_DT:]], axis=1)
        names = ["ffn1_w_gate", "ffn1_w_up", "ffn1_w_down"]
        dh, dh_bf, small["ffn1_norm"][l], *new = _ffn_bwd(f"l{l}_ffn1", s1, *ffn_params(l, "ffn1"), dh, dh_bf,
                                                          stack=([bufs[n] for n in names], l))
        bufs.update(zip(names, new))
    grad_x = dh[None]

    win = jnp.stack(dw_in)
    win = win.reshape(depth, win.shape[1], N_CHIPS, win.shape[2] // N_CHIPS).transpose(2, 0, 1, 3)
    kinds = [BIG[n] for n in big]
    grads = [win if n == "w_in" else bufs[n] for n in big]
    from_sibling = _rs_pair("reduce_pair", kinds, grads)
    halves = [_sum_pair(f"sum_pair_{n}", k, g, r, where) for n, k, g, r in zip(big, kinds, grads, from_sibling)]
    wire_kinds = ["win" if k == "col" else k for k in kinds]
    from_chips = _rs_chips("reduce_chips", wire_kinds, [h16 for _, h16 in halves])
    parts = [_sum_chips(f"sum_chips_{n}", k, h32, r, where) for n, k, (h32, _), r in zip(big, kinds, halves, from_chips)]
    theirs = _swap_pair("share_pair", parts)
    south = c_id == 0
    grad = {n: jnp.concatenate([jnp.where(south, mine, other), jnp.where(south, other, mine)])
            for n, mine, other in zip(big, parts, theirs)}

    small_full = {n: jnp.stack([t.reshape(a[n].shape[1:]) for t in small[n]]) for n in SMALL if n != "final_norm"}
    small_full["final_norm"] = dfinal.reshape(a["final_norm"].shape)
    conv_full = jnp.stack(small["conv_w"])
    shapes = [a[n].shape for n in SMALL] + [conv_full.shape]
    reduced = _unpack(_allreduce_small("allreduce_small", _pack([small_full[n] for n in SMALL] + [conv_full])), shapes)
    grad.update(zip(SMALL, reduced[:-1]))
    shard = a["conv_w"].shape[2]
    grad["conv_w"] = lax.dynamic_slice_in_dim(reduced[-1], chip * shard, shard, axis=2)

    delta, new_m, new_v = {}, {}, {}
    for n in big + ["conv_w"]:
        delta[n], new_m[n], new_v[n] = _adamw(f"adamw_{n}", a[n], grad[n], a["m_" + n], a["v_" + n])
    packed = [_pack([a[pre + n] for n in SMALL]) for pre in ("", "m_", "v_")]
    outs = _adamw("adamw_small", packed[0], _pack([grad[n] for n in SMALL]), packed[1], packed[2])
    for store, block in zip((delta, new_m, new_v), outs):
        store.update(zip(SMALL, _unpack(block, [a[n].shape for n in SMALL])))

    loss = lax.psum(loss_part[0, 0], ("x", "y", "c"))
    return (loss, grad_x, *[grad[n] for n in WEIGHTS], *[delta[n] for n in WEIGHTS], *[new_m[n] for n in WEIGHTS],
            *[new_v[n] for n in WEIGHTS])


def kernel(x, ffn1_norm, ffn1_w_gate, ffn1_w_up, ffn1_w_down, mix_norm, w_in, pool_w, pool_scale, conv_w, conv_b, dt_bias, a_log, d_skip, ssd_norm, w_out, ffn2_norm, ffn2_w_gate, ffn2_w_up, ffn2_w_down, final_norm, loss_target, m_ffn1_norm, m_ffn1_w_gate, m_ffn1_w_up, m_ffn1_w_down, m_mix_norm, m_w_in, m_pool_w, m_pool_scale, m_conv_w, m_conv_b, m_dt_bias, m_a_log, m_d_skip, m_ssd_norm, m_w_out, m_ffn2_norm, m_ffn2_w_gate, m_ffn2_w_up, m_ffn2_w_down, m_final_norm, v_ffn1_norm, v_ffn1_w_gate, v_ffn1_w_up, v_ffn1_w_down, v_mix_norm, v_w_in, v_pool_w, v_pool_scale, v_conv_w, v_conv_b, v_dt_bias, v_a_log, v_d_skip, v_ssd_norm, v_w_out, v_ffn2_norm, v_ffn2_w_gate, v_ffn2_w_up, v_ffn2_w_down, v_final_norm):
    return _train_step(dict(locals()))
```

```python
import functools
import math

import jax
import jax.numpy as jnp
from jax import lax
from jax.experimental import pallas as pl
from jax.experimental.pallas import tpu as pltpu

F32 = jnp.float32
BF16 = jnp.bfloat16
MESH_ID = pl.DeviceIdType.MESH

RMS_EPS = 1e-6
POOL_WINDOWS = (2, 4, 8, 16)
LANES = 128
HEAD_DIM = 64
SSD_HEADS = 16
SSD_CHUNK = 256
ATTN_BLOCK = 128
HALO = 16
EXP_UNDERFLOW = -105.0
VMEM_LIMIT = 56 * 1024 * 1024
MM_SUB = 256
N_CHIPS = 4

ADAM_LR = 0.001
ADAM_B1 = 0.9
ADAM_B2 = 0.999
ADAM_EPS = 1e-08
ADAM_WD = 0.01
ADAM_STEP = 10

C_POOL, C_Z, C_XBC, C_Q, C_K, C_V, C_END = 0, 512, 1536, 3072, 3584, 4096, 4608


def _cparams(sem):
    return pltpu.CompilerParams(dimension_semantics=sem, vmem_limit_bytes=VMEM_LIMIT)


def _tile(dim, pref, unit=LANES):
    if dim <= pref:
        return dim
    t = (pref // unit) * unit
    while t > unit and dim % t:
        t -= unit
    assert dim % t == 0, (dim, pref)
    return t


def _sigmoid(x):
    return 0.5 * jnp.tanh(0.5 * x) + 0.5


def _silu(x):
    return x * _sigmoid(x)


def _dsilu(x):
    s = _sigmoid(x)
    return s * (1.0 + x * (1.0 - s))


def _dot(a, b):
    return jnp.dot(a.astype(BF16), b.astype(BF16), preferred_element_type=F32)


def _dot_nt(a, b):
    return lax.dot_general(a.astype(BF16), b.astype(BF16), (((1,), (1,)), ((), ())), preferred_element_type=F32)


def _dot_tn(a, b):
    return lax.dot_general(a.astype(BF16), b.astype(BF16), (((0,), (0,)), ((), ())), preferred_element_type=F32)


def _split3(x):
    hi = x.astype(BF16)
    r = x - hi.astype(F32)
    mid = r.astype(BF16)
    lo = (r - mid.astype(F32)).astype(BF16)
    return hi, mid, lo


def _dot3(x, m):
    hi, mid, lo = _split3(x)
    dn = (((1,), (0,)), ((), ()))
    f = lambda p: lax.dot_general(p, m, dn, preferred_element_type=F32)
    return f(hi) + f(mid) + f(lo)


def _dot3_left(m, x):
    hi, mid, lo = _split3(x)
    dn = (((1,), (0,)), ((), ()))
    f = lambda p: lax.dot_general(m, p, dn, preferred_element_type=F32)
    return f(hi) + f(mid) + f(lo)


def _iota(shape, axis):
    return lax.broadcasted_iota(jnp.int32, shape, axis)


def _col(x, h):
    return jnp.sum(jnp.where(_iota(x.shape, 1) == h, x, 0.0), axis=1, keepdims=True)


def _roll_down(x, k):
    return x if k == 0 else pltpu.roll(x, k, 0)


def _roll_up(x, k):
    return x if k == 0 else pltpu.roll(x, x.shape[0] - k, 0)


class _Comm:
    def __init__(self, ins, outs, n_remote, n_local, make, done):
        self.ins, self.outs, self.n_remote, self.n_local, self.make, self.done = list(ins), list(outs), n_remote, n_local, make, done


def _comm_plan(hosted):
    ins = [a for cm in hosted for a in cm.ins]
    outs = [o for cm in hosted for o in cm.outs]
    n_remote = sum(cm.n_remote for cm in hosted)
    n_local = sum(cm.n_local for cm in hosted)

    def build(in_refs, out_refs, send, recv, loc):
        starts, waits, i0, o0, r0, l0 = [], [], 0, 0, 0, 0
        for cm in hosted:
            s, w = cm.make(in_refs[i0:i0 + len(cm.ins)], out_refs[o0:o0 + len(cm.outs)], send, recv, loc, r0, l0)
            starts, waits = starts + s, waits + w
            i0, o0, r0, l0 = i0 + len(cm.ins), o0 + len(cm.outs), r0 + cm.n_remote, l0 + cm.n_local
        return starts, waits

    def deliver(results):
        o0 = 0
        for cm in hosted:
            cm.done(results[o0:o0 + len(cm.outs)])
            o0 += len(cm.outs)

    sems = [pltpu.SemaphoreType.DMA((max(n_remote, 1),)), pltpu.SemaphoreType.DMA((max(n_remote, 1),)),
            pltpu.SemaphoreType.DMA((max(n_local, 1),))]
    return ins, outs, sems, build, deliver


def _mm(name, a_list, b_list, *, nt, epilogue, out_dtypes, acc_of=None, extras=(), hosted=(), ta=False,
        tm=1024, tn=512, tk=1024):
    n_pairs = len(a_list)
    acc_of = list(acc_of) if acc_of is not None else [0] * n_pairs
    n_acc = max(acc_of) + 1
    m_dim, k_dim = a_list[0].shape[::-1] if ta else a_list[0].shape
    n_dim = b_list[0].shape[0] if nt else b_list[0].shape[1]
    tm, tn, tk = _tile(m_dim, tm, 8), _tile(n_dim, tn), _tile(k_dim, tk)
    nk = k_dim // tk
    n_ex, n_out = len(extras), len(out_dtypes)
    hosted = list(hosted)
    c_ins, c_outs, c_sems, c_build, c_deliver = _comm_plan(hosted)
    n_ci, n_co = len(c_ins), len(c_outs)
    n_scr = 0 if nk == 1 else n_acc
    grid = (m_dim // tm, n_dim // tn, nk)
    sub = MM_SUB if (nk == 1 and tn > MM_SUB and tn % MM_SUB == 0) else tn

    def body(*refs):
        a_refs = refs[:n_pairs]
        b_refs = refs[n_pairs:2 * n_pairs]
        e_refs = refs[2 * n_pairs:2 * n_pairs + n_ex]
        ci_refs = refs[2 * n_pairs + n_ex:2 * n_pairs + n_ex + n_ci]
        first_out = 2 * n_pairs + n_ex + n_ci
        o_refs = refs[first_out:first_out + n_out]
        co_refs = refs[first_out + n_out:first_out + n_out + n_co]
        acc_refs = refs[first_out + n_out + n_co:first_out + n_out + n_co + n_scr]
        sems = refs[first_out + n_out + n_co + n_scr:]
        if hosted:
            at = [pl.program_id(d) for d in range(3)]
            starts, waits = c_build(ci_refs, co_refs, *sems)

            @pl.when((at[0] == 0) & (at[1] == 0) & (at[2] == 0))
            def _():
                for cp in starts:
                    cp.start()

        if nk == 1:
            for s in range(tn // sub):
                cs = slice(s * sub, (s + 1) * sub)
                accs = [None] * n_acc
                for p in range(n_pairs):
                    a = a_refs[p][...]
                    d = _dot_tn(a, b_refs[p][:, cs]) if ta else _dot_nt(a, b_refs[p][cs, :]) if nt else _dot(a, b_refs[p][:, cs])
                    accs[acc_of[p]] = d if accs[acc_of[p]] is None else accs[acc_of[p]] + d
                outs = epilogue(accs, [e[:, cs] for e in e_refs])
                for o_ref, o in zip(o_refs, outs):
                    o_ref[:, cs] = o.astype(o_ref.dtype)
        else:
            k = pl.program_id(2)

            @pl.when(k == 0)
            def _():
                for acc in acc_refs:
                    acc[...] = jnp.zeros_like(acc)

            for p in range(n_pairs):
                a, b = a_refs[p][...], b_refs[p][...]
                acc_refs[acc_of[p]][...] += _dot_tn(a, b) if ta else _dot_nt(a, b) if nt else _dot(a, b)

            @pl.when(k == nk - 1)
            def _():
                outs = epilogue([acc[...] for acc in acc_refs], [e[...] for e in e_refs])
                for o_ref, o in zip(o_refs, outs):
                    o_ref[...] = o.astype(o_ref.dtype)

        if hosted:
            @pl.when((at[0] == grid[0] - 1) & (at[1] == grid[1] - 1) & (at[2] == grid[2] - 1))
            def _():
                for wait in waits:
                    wait()

    a_spec = pl.BlockSpec((tk, tm), lambda i, j, k: (k, i)) if ta else pl.BlockSpec((tm, tk), lambda i, j, k: (i, k))
    b_spec = pl.BlockSpec((tn, tk), lambda i, j, k: (j, k)) if nt else pl.BlockSpec((tk, tn), lambda i, j, k: (k, j))
    t_spec = pl.BlockSpec((tm, tn), lambda i, j, k: (i, j))
    results = pl.pallas_call(
        body, name=name, grid=grid,
        in_specs=[a_spec] * n_pairs + [b_spec] * n_pairs + [t_spec] * n_ex + _any_specs(n_ci),
        out_specs=[t_spec] * n_out + _any_specs(n_co),
        out_shape=[jax.ShapeDtypeStruct((m_dim, n_dim), dt) for dt in out_dtypes] + c_outs,
        scratch_shapes=[pltpu.VMEM((tm, tn), F32)] * n_scr + (c_sems if hosted else []),
        compiler_params=_cparams(("arbitrary",) * 3 if hosted else ("parallel", "parallel", "arbitrary")),
    )(*a_list, *b_list, *extras, *c_ins)
    c_deliver(results[n_out:])
    return results[:n_out]


def _rms_fwd(name, h, g):
    s_len, d = h.shape
    ts = _tile(s_len, 512, 8)

    def body(h_ref, g_ref, u_ref):
        x = h_ref[...]
        rstd = lax.rsqrt(jnp.mean(x * x, axis=-1, keepdims=True) + RMS_EPS)
        u_ref[...] = (x * rstd * g_ref[...]).astype(BF16)

    return pl.pallas_call(
        body, name=name, grid=(s_len // ts,),
        in_specs=[pl.BlockSpec((ts, d), lambda i: (i, 0)), pl.BlockSpec((1, d), lambda i: (0, 0))],
        out_specs=pl.BlockSpec((ts, d), lambda i: (i, 0)),
        out_shape=jax.ShapeDtypeStruct((s_len, d), BF16),
        compiler_params=_cparams(("parallel",)),
    )(h, g)


def _rms_bwd(name, h, du, dres, g):
    s_len, d = h.shape
    ts = _tile(s_len, 256, 8)

    def body(h_ref, du_ref, dres_ref, g_ref, dh_ref, dhb_ref, dg_ref):
        x = h_ref[...]
        rstd = lax.rsqrt(jnp.mean(x * x, axis=-1, keepdims=True) + RMS_EPS)
        n = x * rstd
        dn = du_ref[...] * g_ref[...]
        dh = dres_ref[...] + rstd * (dn - n * jnp.mean(dn * n, axis=-1, keepdims=True))
        dh_ref[...] = dh
        dhb_ref[...] = dh.astype(BF16)

        @pl.when(pl.program_id(0) == 0)
        def _():
            dg_ref[...] = jnp.zeros_like(dg_ref)

        dg_ref[...] += jnp.sum(du_ref[...] * n, axis=0, keepdims=True)

    row = pl.BlockSpec((ts, d), lambda i: (i, 0))
    vec = pl.BlockSpec((1, d), lambda i: (0, 0))
    return pl.pallas_call(
        body, name=name, grid=(s_len // ts,),
        in_specs=[row, row, row, vec], out_specs=[row, row, vec],
        out_shape=[jax.ShapeDtypeStruct((s_len, d), F32), jax.ShapeDtypeStruct((s_len, d), BF16),
                   jax.ShapeDtypeStruct((1, d), F32)],
        compiler_params=_cparams(("arbitrary",)),
    )(h, du, dres, g)


def _loss_head(name, h, g, target):
    s_len, d = h.shape
    ts = _tile(s_len, 256, 8)

    def body(h_ref, g_ref, t_ref, loss_ref, dh_ref, dhb_ref, dg_ref):
        x = h_ref[...]
        rstd = lax.rsqrt(jnp.mean(x * x, axis=-1, keepdims=True) + RMS_EPS)
        n = x * rstd
        err = n * g_ref[...] - t_ref[...]
        dy = err * (1.0 / d)
        dn = dy * g_ref[...]
        dh = rstd * (dn - n * jnp.mean(dn * n, axis=-1, keepdims=True))
        dh_ref[...] = dh
        dhb_ref[...] = dh.astype(BF16)

        @pl.when(pl.program_id(0) == 0)
        def _():
            dg_ref[...] = jnp.zeros_like(dg_ref)
            loss_ref[...] = jnp.zeros_like(loss_ref)

        dg_ref[...] += jnp.sum(dy * n, axis=0, keepdims=True)
        part = jnp.sum(jnp.sum(err * err, axis=1, keepdims=True), axis=0, keepdims=True) * (0.5 / d)
        loss_ref[...] += jnp.broadcast_to(part, loss_ref.shape)

    row = pl.BlockSpec((ts, d), lambda i: (i, 0))
    vec = pl.BlockSpec((1, d), lambda i: (0, 0))
    lspec = pl.BlockSpec((1, LANES), lambda i: (0, 0))
    return pl.pallas_call(
        body, name=name, grid=(s_len // ts,),
        in_specs=[row, vec, row], out_specs=[lspec, row, row, vec],
        out_shape=[jax.ShapeDtypeStruct((1, LANES), F32), jax.ShapeDtypeStruct((s_len, d), F32),
                   jax.ShapeDtypeStruct((s_len, d), BF16), jax.ShapeDtypeStruct((1, d), F32)],
        compiler_params=_cparams(("arbitrary",)),
    )(h, g, target)


def _no_hooks(_):
    return []


def _ffn_fwd(tag, h, g, wg, wu, wd, hooks=_no_hooks):
    u = _rms_fwd(tag + "_norm", h, g)

    def up(accs, _):
        a, b = accs
        return a, b, _silu(a) * b

    a, b, hm = _mm(tag + "_up", [u, u], [wg, wu], nt=False, acc_of=[0, 1], epilogue=up, out_dtypes=[BF16, BF16, BF16],
                   hosted=hooks("up"))
    (h2,) = _mm(tag + "_down", [hm], [wd], nt=False, extras=[h], epilogue=lambda accs, ex: [ex[0] + 0.5 * accs[0]],
                out_dtypes=[F32], hosted=hooks("down"), tm=1024, tn=1024, tk=1408)
    return h2, (h, u, a, b, hm)


def _ffn_bwd(tag, saved, g, wg, wu, wd, dh2, dh2_bf, hooks=_no_hooks, group=None):
    h, u, a, b, hm = saved

    def dact(accs, ex):
        af, bf = ex[0].astype(F32), ex[1].astype(F32)
        dhm = 0.5 * accs[0]
        return dhm * bf * _dsilu(af), dhm * _silu(af)

    da, db = _mm(tag + "_dhm", [dh2_bf], [wd], nt=True, extras=[a, b], epilogue=dact, out_dtypes=[BF16, BF16],
                 hosted=hooks("dhm"), tk=2048)
    (dwd,) = _mm(tag + "_dwd", [hm], [dh2_bf], nt=False, ta=True, epilogue=lambda accs, _: [0.5 * accs[0]],
                 out_dtypes=[F32], hosted=hooks("dwd"), tm=1408, tn=1024, tk=512)
    dwg, dwu = _mm(tag + "_dwgu", [u, u], [da, db], nt=False, ta=True, acc_of=[0, 1], epilogue=lambda accs, _: accs,
                   out_dtypes=[F32, F32], hosted=hooks("dwgu"), tm=2048, tn=512, tk=1024)
    mine = group(dwg, dwu, dwd) if group else (dwg, dwu, dwd)
    (du,) = _mm(tag + "_du", [da, db], [wg, wu], nt=True, epilogue=lambda accs, _: accs, out_dtypes=[F32],
                hosted=[mine.swap()] if group else [], tm=1024, tn=1024, tk=1408)
    dh, dh_bf, dg = _rms_bwd(tag + "_dnorm", h, du, dh2, g)
    return dh, dh_bf, dg, mine


def _softplus(x):
    e = jnp.exp(-jnp.abs(x))
    u = 1.0 + e
    log1p_e = jnp.where(u == 1.0, e, jnp.log(u) * (e / jnp.where(u == 1.0, 1.0, u - 1.0)))
    return jnp.maximum(x, 0.0) + log1p_e


def _row_spec(ts, width, colblock):
    return pl.BlockSpec((ts, width), lambda i: (i, colblock))


def _halo_before_spec(ts, width, colblock):
    r = ts // HALO
    return pl.BlockSpec((HALO, width), lambda i: (jnp.maximum(i * r - 1, 0), colblock))


def _halo_after_spec(ts, width, colblock, s_len):
    r = ts // HALO
    return pl.BlockSpec((HALO, width), lambda i: (jnp.minimum((i + 1) * r, s_len // HALO - 1), colblock))


def _const_spec(shape):
    nd = len(shape)
    return pl.BlockSpec(shape, lambda *_: (0,) * nd)


def _window_sum(e, win, roll):
    s, sh = e, 1
    while sh < win:
        s = s + roll(s, sh)
        sh *= 2
    return s


def _pool_center(ext, x, t, gi, win):
    sl = slice(gi * LANES, (gi + 1) * LANES)
    s = _window_sum(ext[:, sl], win, _roll_down)
    cnt = jnp.minimum(t + 1, win).astype(F32)
    return s[HALO:] / cnt - x[:, sl]


def _pool_fwd(name, proj, pw, scale):
    s_len = proj.shape[0]
    ts = _tile(s_len, 512, 8)
    width = len(POOL_WINDOWS) * LANES

    def body(x_ref, hb_ref, pw_ref, sc_ref, o_ref):
        i = pl.program_id(0)
        x = x_ref[...]
        ext = jnp.concatenate([jnp.where(i == 0, 0.0, hb_ref[...]), x], axis=0)
        t = i * ts + _iota((ts, 1), 0)
        for gi, win in enumerate(POOL_WINDOWS):
            sl = slice(gi * LANES, (gi + 1) * LANES)
            c = _pool_center(ext, x, t, gi, win)
            o_ref[:, sl] = (_dot(c, pw_ref[gi]) * sc_ref[:, sl]).astype(o_ref.dtype)

    return pl.pallas_call(
        body, name=name, grid=(s_len // ts,),
        in_specs=[_row_spec(ts, width, 0), _halo_before_spec(ts, width, 0), _const_spec(pw.shape), _const_spec(scale.shape)],
        out_specs=_row_spec(ts, width, 0),
        out_shape=jax.ShapeDtypeStruct((s_len, width), BF16),
        compiler_params=_cparams(("parallel",)),
    )(proj, proj, pw, scale)


def _pool_bwd(name, proj, dmixed, pw, scale):
    s_len = proj.shape[0]
    ts = _tile(s_len, 512, 8)
    n_tiles = s_len // ts
    width = len(POOL_WINDOWS) * LANES

    def body(x_ref, hb_ref, d_ref, da_ref, pw_ref, sc_ref, dx_ref, dpw_ref, dsc_ref):
        i = pl.program_id(0)
        x = x_ref[...]
        ext = jnp.concatenate([jnp.where(i == 0, 0.0, hb_ref[...]), x], axis=0)
        dout = d_ref[...]
        dext = jnp.concatenate([dout, jnp.where(i == n_tiles - 1, 0.0, da_ref[...])], axis=0)
        t = i * ts + _iota((ts, 1), 0)
        te = i * ts + _iota((ts + HALO, 1), 0)

        @pl.when(i == 0)
        def _():
            dpw_ref[...] = jnp.zeros_like(dpw_ref)
            dsc_ref[...] = jnp.zeros_like(dsc_ref)

        for gi, win in enumerate(POOL_WINDOWS):
            sl = slice(gi * LANES, (gi + 1) * LANES)
            c = _pool_center(ext, x, t, gi, win)
            o = _dot(c, pw_ref[gi])
            dsc_ref[:, sl] += jnp.sum(dout[:, sl] * o, axis=0, keepdims=True)
            do_ext = dext[:, sl] * sc_ref[:, sl]
            dc = _dot_nt(do_ext, pw_ref[gi])
            e = dc / jnp.minimum(te + 1, win).astype(F32)
            back = _window_sum(e, win, _roll_up)
            dx_ref[:, sl] = (back[:ts] - dc[:ts]).astype(dx_ref.dtype)
            dpw_ref[gi] += _dot(c.T, do_ext[:ts])

    return pl.pallas_call(
        body, name=name, grid=(n_tiles,),
        in_specs=[_row_spec(ts, width, 0), _halo_before_spec(ts, width, 0), _row_spec(ts, width, 0),
                  _halo_after_spec(ts, width, 0, s_len), _const_spec(pw.shape), _const_spec(scale.shape)],
        out_specs=[_row_spec(ts, width, 0), _const_spec(pw.shape), _const_spec(scale.shape)],
        out_shape=[jax.ShapeDtypeStruct((s_len, width), BF16), jax.ShapeDtypeStruct(pw.shape, F32),
                   jax.ShapeDtypeStruct(scale.shape, F32)],
        compiler_params=_cparams(("arbitrary",)),
    )(proj, proj, dmixed, dmixed, pw, scale)


def _conv_pre(ext, w_ref, b_ref):
    k_len = w_ref.shape[0]
    y = _roll_down(ext, k_len - 1) * w_ref[0:1, :]
    for k in range(1, k_len):
        y = y + _roll_down(ext, k_len - 1 - k) * w_ref[k:k + 1, :]
    return y + b_ref[...]


def _conv_fwd(name, proj, w, b):
    s_len = proj.shape[0]
    width = w.shape[1]
    ts = _tile(s_len, 512, 8)
    cb = C_XBC // width

    def body(x_ref, hb_ref, w_ref, b_ref, o_ref):
        i = pl.program_id(0)
        ext = jnp.concatenate([jnp.where(i == 0, 0.0, hb_ref[...]), x_ref[...]], axis=0)
        o_ref[...] = _silu(_conv_pre(ext, w_ref, b_ref)[HALO:])

    return pl.pallas_call(
        body, name=name, grid=(s_len // ts,),
        in_specs=[_row_spec(ts, width, cb), _halo_before_spec(ts, width, cb), _const_spec(w.shape), _const_spec(b.shape)],
        out_specs=_row_spec(ts, width, 0),
        out_shape=jax.ShapeDtypeStruct((s_len, width), F32),
        compiler_params=_cparams(("parallel",)),
    )(proj, proj, w, b)


def _conv_bwd(name, proj, dact, w, b):
    s_len = proj.shape[0]
    width = w.shape[1]
    k_len = w.shape[0]
    ts = _tile(s_len, 512, 8)
    n_tiles = s_len // ts
    cb = C_XBC // width

    def body(x_ref, hb_ref, ha_ref, d_ref, da_ref, w_ref, b_ref, dx_ref, dw_ref, db_ref):
        i = pl.program_id(0)
        last = i == n_tiles - 1
        ext = jnp.concatenate([jnp.where(i == 0, 0.0, hb_ref[...]), x_ref[...], jnp.where(last, 0.0, ha_ref[...])], axis=0)
        pre = _conv_pre(ext, w_ref, b_ref)[HALO:]
        dpre = jnp.concatenate([d_ref[...], jnp.where(last, 0.0, da_ref[...])], axis=0) * _dsilu(pre)

        @pl.when(i == 0)
        def _():
            dw_ref[...] = jnp.zeros_like(dw_ref)
            db_ref[...] = jnp.zeros_like(db_ref)

        dx = _roll_up(dpre, k_len - 1) * w_ref[0:1, :]
        for k in range(1, k_len):
            dx = dx + _roll_up(dpre, k_len - 1 - k) * w_ref[k:k + 1, :]
        dx_ref[...] = dx[:ts].astype(dx_ref.dtype)
        dtile = dpre[:ts]
        for k in range(k_len):
            xk = _roll_down(ext, k_len - 1 - k)[HALO:HALO + ts]
            dw_ref[k:k + 1, :] += jnp.sum(dtile * xk, axis=0, keepdims=True)
        db_ref[...] += jnp.sum(dtile, axis=0, keepdims=True)

    return pl.pallas_call(
        body, name=name, grid=(n_tiles,),
        in_specs=[_row_spec(ts, width, cb), _halo_before_spec(ts, width, cb), _halo_after_spec(ts, width, cb, s_len),
                  _row_spec(ts, width, 0), _halo_after_spec(ts, width, 0, s_len), _const_spec(w.shape), _const_spec(b.shape)],
        out_specs=[_row_spec(ts, width, 0), _const_spec(w.shape), _const_spec(b.shape)],
        out_shape=[jax.ShapeDtypeStruct((s_len, width), BF16), jax.ShapeDtypeStruct(w.shape, F32),
                   jax.ShapeDtypeStruct(b.shape, F32)],
        compiler_params=_cparams(("arbitrary",)),
    )(proj, proj, proj, dact, dact, w, b)


def _pair_cols(c0, c1, lo_half):
    return jnp.where(lo_half, c0, c1)


def _ssd_common(dtr_ref, bias_ref, alog_ref, acs_t_ref):
    chunk = dtr_ref.shape[0]
    xpre = dtr_ref[...] + bias_ref[...]
    dt = _softplus(xpre)
    a_neg = -jnp.exp(alog_ref[...])
    tri = _iota((chunk, chunk), 1) <= _iota((chunk, chunk), 0)
    a_cs = _dot3_left(tri.astype(BF16), dt * a_neg)
    acs_t_ref[...] = a_cs.T
    a_last = jnp.sum(jnp.where(_iota(a_cs.shape, 0) == chunk - 1, a_cs, 0.0), axis=0, keepdims=True)
    return xpre, dt, a_neg, tri, a_cs, a_last


def _ssd_specs(chunk, order):
    xs = pl.BlockSpec((chunk, 1024), lambda c: (order(c), 0))
    bm = pl.BlockSpec((chunk, 256), lambda c: (order(c), 4))
    cm = pl.BlockSpec((chunk, 256), lambda c: (order(c), 5))
    lanes = pl.BlockSpec((chunk, LANES), lambda c: (order(c), 0))
    return xs, bm, cm, lanes


def _ssd_fwd(name, xbc, dtr, dt_bias, a_log, d_skip):
    s_len = xbc.shape[0]
    chunk = SSD_CHUNK
    nc = s_len // chunk
    n_pairs = SSD_HEADS // 2

    def body(xs_ref, b_ref, c_ref, dtr_ref, bias_ref, alog_ref, dsk_ref, y_ref, st_ref, state_ref, acs_t_ref):
        @pl.when(pl.program_id(0) == 0)
        def _():
            state_ref[...] = jnp.zeros_like(state_ref)

        _, dt, _, tri, a_cs, a_last = _ssd_common(dtr_ref, bias_ref, alog_ref, acs_t_ref)
        lo_half = _iota((chunk, LANES), 1) < HEAD_DIM
        lo_lane = _iota((1, LANES), 1) < HEAD_DIM
        lo_row = _iota((LANES, 1), 0) < HEAD_DIM
        dsk = dsk_ref[...]
        for g in range(2):
            gsl = slice(g * LANES, (g + 1) * LANES)
            bg, cg = b_ref[:, gsl], c_ref[:, gsl]
            gmat = _dot_nt(cg, bg)
            for pr in range(n_pairs // 2):
                pair = g * (n_pairs // 2) + pr
                h0, h1 = 2 * pair, 2 * pair + 1
                psl = slice(pair * LANES, (pair + 1) * LANES)
                x2 = xs_ref[:, psl]
                acs0, acs1 = _col(a_cs, h0), _col(a_cs, h1)
                xdt = x2 * _pair_cols(_col(dt, h0), _col(dt, h1), lo_half)
                y2 = jnp.zeros((chunk, LANES), F32)
                for h, acs_c, hmask in ((h0, acs0, lo_half), (h1, acs1, ~lo_half)):
                    lam = jnp.where(tri, jnp.exp(jnp.minimum(acs_c - acs_t_ref[h:h + 1, :], 0.0)), 0.0)
                    y2 = y2 + _dot(gmat * lam, jnp.where(hmask, xdt, 0.0))
                s2 = state_ref[pair]
                st_ref[0, pair] = s2
                y2 = y2 + _pair_cols(jnp.exp(acs0), jnp.exp(acs1), lo_half) * _dot_nt(cg, s2)
                y_ref[:, psl] = y2 + _pair_cols(_col(dsk, h0), _col(dsk, h1), lo_lane) * x2
                al0, al1 = _col(a_last, h0), _col(a_last, h1)
                wl2 = _pair_cols(jnp.exp(al0 - acs0), jnp.exp(al1 - acs1), lo_half)
                state_ref[pair] = _pair_cols(jnp.exp(al0), jnp.exp(al1), lo_row) * s2 + _dot((xdt * wl2).T, bg)

    xs, bm, cm, lanes = _ssd_specs(chunk, lambda c: c)
    vec = _const_spec((1, LANES))
    return pl.pallas_call(
        body, name=name, grid=(nc,),
        in_specs=[xs, bm, cm, lanes, vec, vec, vec],
        out_specs=[xs, pl.BlockSpec((1, n_pairs, LANES, LANES), lambda c: (c, 0, 0, 0))],
        out_shape=[jax.ShapeDtypeStruct((s_len, 1024), F32), jax.ShapeDtypeStruct((nc, n_pairs, LANES, LANES), F32)],
        scratch_shapes=[pltpu.VMEM((n_pairs, LANES, LANES), F32), pltpu.VMEM((LANES, chunk), F32)],
        compiler_params=_cparams(("arbitrary",)),
    )(xbc, xbc, xbc, dtr, dt_bias, a_log, d_skip)


def _ssd_bwd(name, xbc, dtr, states, dy, dt_bias, a_log, d_skip):
    s_len = xbc.shape[0]
    chunk = SSD_CHUNK
    nc = s_len // chunk
    n_pairs = SSD_HEADS // 2
    rev = lambda c: nc - 1 - c

    def body(xs_ref, b_ref, c_ref, dtr_ref, dy_ref, sin_ref, bias_ref, alog_ref, dsk_ref,
             dxs_ref, db_ref, dc_ref, ddtr_ref, dbias_ref, dalog_ref, ddsk_ref, dstate_ref, acs_t_ref):
        @pl.when(pl.program_id(0) == 0)
        def _():
            dstate_ref[...] = jnp.zeros_like(dstate_ref)
            dbias_ref[...] = jnp.zeros_like(dbias_ref)
            dalog_ref[...] = jnp.zeros_like(dalog_ref)
            ddsk_ref[...] = jnp.zeros_like(ddsk_ref)

        xpre, dt, a_neg, tri, a_cs, a_last = _ssd_common(dtr_ref, bias_ref, alog_ref, acs_t_ref)
        lane = _iota((chunk, LANES), 1)
        lo_half = lane < HEAD_DIM
        lane1 = _iota((1, LANES), 1)
        lo_lane = lane1 < HEAD_DIM
        lo_row = _iota((LANES, 1), 0) < HEAD_DIM
        head_row = _iota((LANES, chunk), 0)
        sq_row, sq_col = _iota((chunk, chunk), 0), _iota((chunk, chunk), 1)
        before = (sq_row < sq_col).astype(BF16)
        dsk = dsk_ref[...]
        da_rows = jnp.zeros((LANES, chunk), F32)
        yo = jnp.zeros((chunk, LANES), F32)
        to = jnp.zeros((chunk, LANES), F32)
        vs = jnp.zeros((1, LANES), F32)
        ddt = jnp.zeros((chunk, LANES), F32)
        ddsk = jnp.zeros((1, LANES), F32)

        def half_sums(v):
            lo = jnp.sum(jnp.where(lo_half, v, 0.0), axis=1, keepdims=True)
            return lo, jnp.sum(v, axis=1, keepdims=True) - lo

        for g in range(2):
            gsl = slice(g * LANES, (g + 1) * LANES)
            bg, cg = b_ref[:, gsl], c_ref[:, gsl]
            gmat = _dot_nt(cg, bg)
            dgm = jnp.zeros((chunk, chunk), F32)
            dbg = jnp.zeros((chunk, LANES), F32)
            dcg = jnp.zeros((chunk, LANES), F32)
            for pr in range(n_pairs // 2):
                pair = g * (n_pairs // 2) + pr
                h0, h1 = 2 * pair, 2 * pair + 1
                psl = slice(pair * LANES, (pair + 1) * LANES)
                x2, dy2 = xs_ref[:, psl], dy_ref[:, psl]
                acs0, acs1 = _col(a_cs, h0), _col(a_cs, h1)
                dt2 = _pair_cols(_col(dt, h0), _col(dt, h1), lo_half)
                xdt = x2 * dt2
                al0, al1 = _col(a_last, h0), _col(a_last, h1)
                v2 = _pair_cols(jnp.exp(acs0), jnp.exp(acs1), lo_half)
                wl2 = _pair_cols(jnp.exp(al0 - acs0), jnp.exp(al1 - acs1), lo_half)
                s_in, ds2 = sin_ref[0, pair], dstate_ref[pair]
                y_off = v2 * _dot_nt(cg, s_in)
                dx_state = wl2 * _dot_nt(bg, ds2)
                dx2 = dx_state
                for h, acs_c, hmask in ((h0, acs0, lo_half), (h1, acs1, ~lo_half)):
                    lam = jnp.where(tri, jnp.exp(jnp.minimum(acs_c - acs_t_ref[h:h + 1, :], 0.0)), 0.0)
                    m = gmat * lam
                    dyh = jnp.where(hmask, dy2, 0.0)
                    dx2 = dx2 + _dot(m.T, dyh)
                    dml = _dot_nt(dyh, xdt) * lam
                    dgm = dgm + dml
                    crossed = jnp.where(sq_row >= sq_col, _dot3(dml * gmat, before), 0.0)
                    da_rows = jnp.where(head_row == h, jnp.sum(crossed, axis=0, keepdims=True), da_rows)
                vdy = v2 * dy2
                dcg = dcg + _dot(vdy, s_in)
                dbg = dbg + _dot(wl2 * xdt, ds2)
                yo0, yo1 = half_sums(dy2 * y_off)
                yo = jnp.where(lane == h0, yo0, jnp.where(lane == h1, yo1, yo))
                to0, to1 = half_sums(dx_state * xdt)
                to = jnp.where(lane == h0, to0, jnp.where(lane == h1, to1, to))
                prod = jnp.sum(ds2 * s_in, axis=1, keepdims=True)
                e0 = jnp.sum(jnp.where(lo_row, prod, 0.0), axis=0, keepdims=True)
                e1 = jnp.sum(prod, axis=0, keepdims=True) - e0
                vs = jnp.where(lane1 == h0, jnp.exp(al0) * e0, jnp.where(lane1 == h1, jnp.exp(al1) * e1, vs))
                q0, q1 = half_sums(dx2 * x2)
                ddt = jnp.where(lane == h0, q0, jnp.where(lane == h1, q1, ddt))
                dxs_ref[:, psl] = dx2 * dt2 + _pair_cols(_col(dsk, h0), _col(dsk, h1), lo_lane) * dy2
                s0, s1 = half_sums(dy2 * x2)
                ddsk = jnp.where(lane1 == h0, jnp.sum(s0, axis=0, keepdims=True),
                                 jnp.where(lane1 == h1, jnp.sum(s1, axis=0, keepdims=True), ddsk))
                dstate_ref[pair] = _pair_cols(jnp.exp(al0), jnp.exp(al1), lo_row) * ds2 + _dot(vdy.T, cg)
            dc_ref[:, gsl] = dcg + _dot(dgm, bg)
            db_ref[:, gsl] = dbg + _dot(dgm.T, cg)

        da = (da_rows.T + _dot3_left((sq_col >= sq_row).astype(BF16), yo)
              + _dot3_left((sq_col < sq_row).astype(BF16), to) + vs)
        ddt = ddt + da * a_neg
        dalog_ref[...] += jnp.sum(da * dt, axis=0, keepdims=True) * a_neg
        ddtr = jnp.where(lane < SSD_HEADS, ddt * jax.nn.sigmoid(xpre), 0.0)
        ddtr_ref[...] = ddtr
        dbias_ref[...] += jnp.sum(ddtr, axis=0, keepdims=True)
        ddsk_ref[...] += ddsk

    xs, bm, cm, lanes = _ssd_specs(chunk, rev)
    vec = _const_spec((1, LANES))
    st_in = pl.BlockSpec((1, n_pairs, LANES, LANES), lambda c: (rev(c), 0, 0, 0))
    bc_out = pl.BlockSpec((chunk, 256), lambda c: (rev(c), 0))
    return pl.pallas_call(
        body, name=name, grid=(nc,),
        in_specs=[xs, bm, cm, lanes, xs, st_in, vec, vec, vec],
        out_specs=[xs, bc_out, bc_out, lanes, vec, vec, vec],
        out_shape=[jax.ShapeDtypeStruct((s_len, 1024), F32), jax.ShapeDtypeStruct((s_len, 256), F32),
                   jax.ShapeDtypeStruct((s_len, 256), F32), jax.ShapeDtypeStruct((s_len, LANES), F32),
                   jax.ShapeDtypeStruct((1, LANES), F32), jax.ShapeDtypeStruct((1, LANES), F32),
                   jax.ShapeDtypeStruct((1, LANES), F32)],
        scratch_shapes=[pltpu.VMEM((n_pairs, LANES, LANES), F32), pltpu.VMEM((LANES, chunk), F32)],
        compiler_params=_cparams(("arbitrary",)),
    )(xbc, xbc, xbc, dtr, dy, states, dt_bias, a_log, d_skip)


def _gatenorm_fwd(name, y, proj, g):
    s_len = y.shape[0]
    ts = _tile(s_len, 512, 8)
    gw = 512

    def body(y_ref, z_ref, g_ref, o_ref):
        yg = y_ref[...] * _silu(z_ref[...])
        rstd = lax.rsqrt(jnp.mean(yg * yg, axis=-1, keepdims=True) + RMS_EPS)
        o_ref[...] = (yg * rstd * g_ref[...]).astype(o_ref.dtype)

    return pl.pallas_call(
        body, name=name, grid=(2, s_len // ts),
        in_specs=[pl.BlockSpec((ts, gw), lambda gi, i: (i, gi)), pl.BlockSpec((ts, gw), lambda gi, i: (i, C_Z // gw + gi)),
                  pl.BlockSpec((1, gw), lambda gi, i: (0, gi))],
        out_specs=pl.BlockSpec((ts, gw), lambda gi, i: (i, gi)),
        out_shape=jax.ShapeDtypeStruct((s_len, 2 * gw), BF16),
        compiler_params=_cparams(("parallel", "parallel")),
    )(y, proj, g)


def _gatenorm_bwd(name, y, proj, g, dmixed):
    s_len = y.shape[0]
    ts = _tile(s_len, 512, 8)
    gw = 512

    def body(y_ref, z_ref, g_ref, d_ref, dy_ref, dz_ref, dg_ref):
        yv, z = y_ref[...], z_ref[...]
        sz = _silu(z)
        yg = yv * sz
        rstd = lax.rsqrt(jnp.mean(yg * yg, axis=-1, keepdims=True) + RMS_EPS)
        n = yg * rstd
        dn = d_ref[...] * g_ref[...]
        dyg = rstd * (dn - n * jnp.mean(dn * n, axis=-1, keepdims=True))
        dy_ref[...] = dyg * sz
        dz_ref[...] = (dyg * yv * _dsilu(z)).astype(dz_ref.dtype)

        @pl.when(pl.program_id(1) == 0)
        def _():
            dg_ref[...] = jnp.zeros_like(dg_ref)

        dg_ref[...] += jnp.sum(d_ref[...] * n, axis=0, keepdims=True)

    grp = pl.BlockSpec((ts, gw), lambda gi, i: (i, gi))
    vec = pl.BlockSpec((1, gw), lambda gi, i: (0, gi))
    return pl.pallas_call(
        body, name=name, grid=(2, s_len // ts),
        in_specs=[grp, pl.BlockSpec((ts, gw), lambda gi, i: (i, C_Z // gw + gi)), vec,
                  pl.BlockSpec((ts, gw), lambda gi, i: (i, 1 + gi))],
        out_specs=[grp, grp, vec],
        out_shape=[jax.ShapeDtypeStruct((s_len, 2 * gw), F32), jax.ShapeDtypeStruct((s_len, 2 * gw), BF16),
                   jax.ShapeDtypeStruct((1, 2 * gw), F32)],
        compiler_params=_cparams(("parallel", "arbitrary")),
    )(y, proj, g, dmixed)


def _attn_scores(qh, kblk, mask, ustrict, r):
    z = _dot_nt(qh, kblk)
    sp = _softplus(-jnp.abs(z))
    ls = jnp.minimum(z, 0.0) - sp
    lm_raw = jnp.minimum(-z, 0.0) - sp
    lm = jnp.where(mask, lm_raw, 0.0)
    suffix = _dot3(lm, ustrict)
    w = jnp.where(mask, jnp.exp(ls + suffix + r), 0.0)
    return ls, lm_raw, lm, w


def _attn_tiles(s_len):
    tk = ATTN_BLOCK
    return next(m * tk for m in (4, 2, 1) if s_len % (m * tk) == 0), tk


def _attn_kv(proj):
    return proj[:, C_K:C_END].astype(BF16)


def _attn_specs(tq, s_len):
    qcol, vcol = C_Q // LANES, (C_V - C_K) // LANES
    q = pl.BlockSpec((tq, LANES), lambda p, i: (i, qcol + p))
    k = pl.BlockSpec((s_len, LANES), lambda p, i: (0, p))
    v = pl.BlockSpec((s_len, LANES), lambda p, i: (0, vcol + p))
    return q, k, v


def _attn_fwd(name, proj, kv):
    s_len = proj.shape[0]
    tq, tk = _attn_tiles(s_len)
    n_slabs = 4

    def body(q_ref, k_ref, v_ref, o_ref):
        qi = pl.program_id(1)
        q2 = q_ref[...] * (HEAD_DIM ** -0.5)
        lo = _iota((tq, LANES), 1) < HEAD_DIM
        lo_k = _iota((tk, LANES), 1) < HEAD_DIM
        heads = ((jnp.where(lo, q2, 0.0).astype(BF16), lo_k), (jnp.where(lo, 0.0, q2).astype(BF16), ~lo_k))
        ustrict = (_iota((tk, tk), 0) > _iota((tk, tk), 1)).astype(BF16)
        q_pos = qi * tq + _iota((tq, tk), 0)
        k_off = _iota((tq, tk), 1)

        def step(carry):
            kb, _, r0, r1, acc = carry
            rows = pl.ds(pl.multiple_of(kb * tk, tk), tk)
            kblk, vblk = k_ref[rows, :], v_ref[rows, :]
            mask = kb * tk + k_off < q_pos
            new_r = []
            for (qh, hmask), r in zip(heads, (r0, r1)):
                _, _, lm, w = _attn_scores(qh, kblk, mask, ustrict, r)
                acc = acc + _dot(w, jnp.where(hmask, vblk, 0.0))
                new_r.append(r + jnp.sum(lm, axis=1, keepdims=True))
            go = (jnp.maximum(jnp.max(new_r[0]), jnp.max(new_r[1])) > EXP_UNDERFLOW).astype(jnp.int32)
            return kb - 1, go, new_r[0], new_r[1], acc

        zero = jnp.zeros((tq, 1), F32)
        init = ((qi + 1) * (tq // tk) - 1, jnp.int32(1), zero, zero, jnp.zeros((tq, LANES), F32))
        o_ref[...] = lax.while_loop(lambda c: (c[0] >= 0) & (c[1] > 0), step, init)[4]

    q, k, v = _attn_specs(tq, s_len)
    return pl.pallas_call(
        body, name=name, grid=(n_slabs, s_len // tq),
        in_specs=[q, k, v], out_specs=pl.BlockSpec((tq, LANES), lambda p, i: (i, p)),
        out_shape=jax.ShapeDtypeStruct((s_len, n_slabs * LANES), F32),
        compiler_params=_cparams(("parallel", "arbitrary")),
    )(proj, kv, kv)


def _attn_bwd(name, proj, kv, dmixed):
    s_len = proj.shape[0]
    tq, tk = _attn_tiles(s_len)
    n_slabs = 4
    scale = HEAD_DIM ** -0.5

    def body(q_ref, k_ref, v_ref, do_ref, dq_ref, dk_ref, dv_ref, dk_acc, dv_acc, r_hist):
        qi = pl.program_id(1)

        @pl.when(qi == 0)
        def _():
            dk_acc[...] = jnp.zeros_like(dk_acc)
            dv_acc[...] = jnp.zeros_like(dv_acc)

        q2 = q_ref[...] * scale
        do2 = do_ref[...]
        lo = _iota((tq, LANES), 1) < HEAD_DIM
        lo_k = _iota((tk, LANES), 1) < HEAD_DIM
        heads = ((jnp.where(lo, q2, 0.0).astype(BF16), jnp.where(lo, do2, 0.0).astype(BF16), lo_k),
                 (jnp.where(lo, 0.0, q2).astype(BF16), jnp.where(lo, 0.0, do2).astype(BF16), ~lo_k))
        row, col = _iota((tk, tk), 0), _iota((tk, tk), 1)
        ustrict = (row > col).astype(BF16)
        earlier = (row < col).astype(BF16)
        q_pos = qi * tq + _iota((tq, tk), 0)
        k_off = _iota((tq, tk), 1)
        top = (qi + 1) * (tq // tk) - 1
        zero = jnp.zeros((tq, 1), F32)

        def scan(carry):
            kb, _, r0, r1 = carry
            kblk = k_ref[pl.ds(pl.multiple_of(kb * tk, tk), tk), :]
            mask = kb * tk + k_off < q_pos
            r_hist[kb] = jnp.where(lo, r0, r1)
            new_r = []
            for (qh, _, _), r in zip(heads, (r0, r1)):
                z = _dot_nt(qh, kblk)
                lm = jnp.where(mask, jnp.minimum(-z, 0.0) - _softplus(-jnp.abs(z)), 0.0)
                new_r.append(r + jnp.sum(lm, axis=1, keepdims=True))
            go = (jnp.maximum(jnp.max(new_r[0]), jnp.max(new_r[1])) > EXP_UNDERFLOW).astype(jnp.int32)
            return kb - 1, go, new_r[0], new_r[1]

        first = lax.while_loop(lambda c: (c[0] >= 0) & (c[1] > 0), scan, (top, jnp.int32(1), zero, zero))[0] + 1

        def step(carry):
            kb, p0, p1, dq = carry
            rows = pl.ds(pl.multiple_of(kb * tk, tk), tk)
            kblk, vblk = k_ref[rows, :], v_ref[rows, :]
            mask = kb * tk + k_off < q_pos
            rr = r_hist[kb]
            dk_blk = jnp.zeros((tk, LANES), F32)
            dv_blk = jnp.zeros((tk, LANES), F32)
            new_p = []
            for (qh, doh, hmask), r, p in zip(heads, (_col(rr, 0), _col(rr, HEAD_DIM)), (p0, p1)):
                ls, lm_raw, _, w = _attn_scores(qh, kblk, mask, ustrict, r)
                ew = _dot_nt(doh, vblk) * w
                before = p + _dot3(ew, earlier)
                dz = jnp.where(mask, ew * jnp.exp(lm_raw) - jnp.exp(ls) * before, 0.0)
                dq = dq + _dot(dz, jnp.where(hmask, kblk, 0.0))
                dk_blk = dk_blk + _dot(dz.T, qh)
                dv_blk = dv_blk + _dot(w.T, doh)
                new_p.append(p + jnp.sum(ew, axis=1, keepdims=True))
            dk_acc[rows, :] += dk_blk
            dv_acc[rows, :] += dv_blk
            return kb + 1, new_p[0], new_p[1], dq

        dq = lax.while_loop(lambda c: c[0] <= top, step, (first, zero, zero, jnp.zeros((tq, LANES), F32)))[3]
        dq_ref[...] = (dq * scale).astype(dq_ref.dtype)

        @pl.when(qi == pl.num_programs(1) - 1)
        def _():
            dk_ref[...] = dk_acc[...].astype(dk_ref.dtype)
            dv_ref[...] = dv_acc[...].astype(dv_ref.dtype)

    q, k, v = _attn_specs(tq, s_len)
    blk = pl.BlockSpec((tq, LANES), lambda p, i: (i, p))
    full = pl.BlockSpec((s_len, LANES), lambda p, i: (0, p))
    shape = jax.ShapeDtypeStruct((s_len, n_slabs * LANES), BF16)
    return pl.pallas_call(
        body, name=name, grid=(n_slabs, s_len // tq),
        in_specs=[q, k, v, pl.BlockSpec((tq, LANES), lambda p, i: (i, 1536 // LANES + p))],
        out_specs=[blk, full, full], out_shape=[shape, shape, shape],
        scratch_shapes=[pltpu.VMEM((s_len, LANES), F32), pltpu.VMEM((s_len, LANES), F32),
                        pltpu.VMEM((s_len // tk, tq, LANES), F32)],
        compiler_params=_cparams(("parallel", "arbitrary")),
    )(proj, kv, kv, dmixed)


def _ident(accs, _):
    return accs


def _mixer_fwd(tag, h, p, hooks=_no_hooks):
    u = _rms_fwd(tag + "_norm", h, p["mix_norm"])
    (proj,) = _mm(tag + "_in", [u], [p["w_main"]], nt=False, epilogue=_ident, out_dtypes=[F32], hosted=hooks("in"),
                  tk=2048)
    (dtr,) = _mm(tag + "_indt", [u], [p["w_dt"]], nt=False, epilogue=_ident, out_dtypes=[F32], tk=2048)
    pool_out = _pool_fwd(tag + "_pool", proj, p["pool_w"], p["pool_scale"])
    xbc = _conv_fwd(tag + "_conv", proj, p["conv_w"], p["conv_b"])
    y, states = _ssd_fwd(tag + "_ssd", xbc, dtr, p["dt_bias"], p["a_log"], p["d_skip"])
    ssd_out = _gatenorm_fwd(tag + "_gate", y, proj, p["ssd_norm"])
    kv = _attn_kv(proj)
    attn = _attn_fwd(tag + "_attn", proj, kv)
    mixed = jnp.concatenate([pool_out, ssd_out, attn.astype(BF16)], axis=1)
    (h2,) = _mm(tag + "_out", [mixed], [p["w_out"]], nt=False, extras=[h], epilogue=lambda accs, ex: [ex[0] + accs[0]],
                out_dtypes=[F32], hosted=hooks("out"), tk=2048)
    return h2, (h, u, proj, dtr, xbc, y, states, mixed, kv)


def _mixer_bwd(tag, saved, p, dh2, dh2_bf, hooks=_no_hooks, group=None):
    h, u, proj, dtr, xbc, y, states, mixed, kv = saved
    (dmixed,) = _mm(tag + "_dmix", [dh2_bf], [p["w_out"]], nt=True, epilogue=_ident, out_dtypes=[F32],
                    hosted=hooks("dmix"), tk=2048)
    (dw_out,) = _mm(tag + "_dwout", [mixed], [dh2_bf], nt=False, ta=True, epilogue=_ident, out_dtypes=[F32],
                    hosted=hooks("dwout"))
    dpool_in, dpool_w, dpool_scale = _pool_bwd(tag + "_dpool", proj, dmixed, p["pool_w"], p["pool_scale"])
    dy, dz, dssd_norm = _gatenorm_bwd(tag + "_dgate", y, proj, p["ssd_norm"], dmixed)
    dxs, dbm, dcm, ddtr, ddt_bias, da_log, dd_skip = _ssd_bwd(tag + "_dssd", xbc, dtr, states, dy, p["dt_bias"],
                                                             p["a_log"], p["d_skip"])
    dxbc, dconv_w, dconv_b = _conv_bwd(tag + "_dconv", proj, jnp.concatenate([dxs, dbm, dcm], axis=1), p["conv_w"],
                                       p["conv_b"])
    dq, dk, dv = _attn_bwd(tag + "_dattn", proj, kv, dmixed)
    dproj = jnp.concatenate([dpool_in, dz, dxbc, dq, dk, dv], axis=1)
    ddtr_bf = ddtr.astype(BF16)
    (dw_main,) = _mm(tag + "_dwin", [u], [dproj], nt=False, ta=True, epilogue=_ident, out_dtypes=[F32],
                     hosted=hooks("dwin"))
    (dw_dt,) = _mm(tag + "_dwdt", [u], [ddtr_bf], nt=False, ta=True, epilogue=_ident, out_dtypes=[F32])
    dw_in = jnp.concatenate([dw_main[:, :REF_DT], dw_dt[:, :SSD_HEADS], dw_main[:, REF_DT:]], axis=1)
    dw_in = dw_in.reshape(dw_in.shape[0], N_CHIPS, dw_in.shape[1] // N_CHIPS).transpose(1, 0, 2)
    mine = group(dw_in, dw_out) if group else (dw_in, dw_out)
    (du_dt,) = _mm(tag + "_dudt", [ddtr_bf], [p["w_dt"]], nt=True, epilogue=_ident, out_dtypes=[F32], tn=1024)
    (du,) = _mm(tag + "_du", [dproj], [p["w_main"]], nt=True, extras=[du_dt], epilogue=lambda accs, ex: [accs[0] + ex[0]],
                out_dtypes=[F32], hosted=hooks("du") + ([mine.swap()] if group else []), tm=1024, tn=1024, tk=1536)
    dh, dh_bf, dg = _rms_bwd(tag + "_dnorm", h, du, dh2, p["mix_norm"])
    grads = dict(mix_norm=dg, pool_w=dpool_w, pool_scale=dpool_scale, conv_w=dconv_w, conv_b=dconv_b, dt_bias=ddt_bias,
                 a_log=da_log, d_skip=dd_skip, ssd_norm=dssd_norm)
    return dh, dh_bf, grads, mine


def _axes():
    return lax.axis_index("x"), lax.axis_index("y"), lax.axis_index("c")


def _any_specs(n):
    return [pl.BlockSpec(memory_space=pl.ANY) for _ in range(n)]


def _remote(src, dst, send, recv, k, dev):
    return pltpu.make_async_remote_copy(src_ref=src, dst_ref=dst, send_sem=send.at[k], recv_sem=recv.at[k],
                                        device_id=dev, device_id_type=MESH_ID)


def _chip_peers(x, y):
    return [(1 - x, y), (x, 1 - y), (1 - x, 1 - y)]


def _gather_comm(shards, done):
    n = len(shards)

    def make(in_refs, out_refs, send, recv, loc, r0, l0):
        x, y, c = _axes()
        me = 2 * x + y
        local = [pltpu.make_async_copy(in_refs[a], out_refs[a].at[me], loc.at[l0 + a]) for a in range(n)]
        sent = [_remote(in_refs[a], out_refs[a].at[me], send, recv, r0 + 3 * a + k, (px, py, c))
                for a in range(n) for k, (px, py) in enumerate(_chip_peers(x, y))]
        got = [_remote(in_refs[a], out_refs[a].at[2 * px + py], send, recv, r0 + 3 * a + k, (px, py, c))
               for a in range(n) for k, (px, py) in enumerate(_chip_peers(x, y))]
        return local + sent, [g.wait_recv for g in got] + [s.wait_send for s in sent] + [l.wait for l in local]

    return _Comm(shards, [jax.ShapeDtypeStruct((N_CHIPS,) + s.shape, s.dtype) for s in shards], 3 * n, n, make, done)


def _swap_comm(arrs, kinds, outs, done):
    n = len(arrs)

    def make(in_refs, out_refs, send, recv, loc, r0, l0):
        x, y, c = _axes()
        cps = [_remote(in_refs[a] if kinds[a] is None else _half(kinds[a], in_refs[a], 1 - c), out_refs[a], send, recv,
                       r0 + a, (x, y, 1 - c)) for a in range(n)]
        return cps, [cp.wait for cp in cps]

    return _Comm(arrs, outs, n, 0, make, done)


def _chips_comm(wires, kinds, done):
    n = len(wires)

    def make(in_refs, out_refs, send, recv, loc, r0, l0):
        x, y, c = _axes()
        cps = [_remote(_wire_shard(kinds[a], in_refs[a], 2 * px + py), out_refs[a].at[k], send, recv, r0 + 3 * a + k,
                       (px, py, c)) for a in range(n) for k, (px, py) in enumerate(_chip_peers(x, y))]
        return cps, [cp.wait for cp in cps]

    outs = [jax.ShapeDtypeStruct((3,) + _wire_shard_shape(k, w.shape), w.dtype) for k, w in zip(kinds, wires)]
    return _Comm(wires, outs, 3 * n, 0, make, done)


def _run_comm(name, hosted):
    ins, outs, sems, build, deliver = _comm_plan(hosted)

    def body(*refs):
        starts, waits = build(refs[:len(ins)], refs[len(ins):len(ins) + len(outs)], *refs[len(ins) + len(outs):])
        for cp in starts:
            cp.start()
        for wait in waits:
            wait()

    deliver(pl.pallas_call(body, name=name, in_specs=_any_specs(len(ins)), out_specs=_any_specs(len(outs)),
                           out_shape=outs, scratch_shapes=sems)(*ins))


def _half(kind, ref, hc):
    if kind == "col":
        r = ref.shape[0] // 2
        return ref.at[pl.ds(pl.multiple_of(hc * r, 16), r), :]
    if kind == "row":
        w = ref.shape[1] // 2
        return ref.at[:, pl.ds(pl.multiple_of(hc * w, LANES), w)]
    r = ref.shape[1] // 2
    return ref.at[:, pl.ds(pl.multiple_of(hc * r, 16), r), :]


def _half_shape(kind, s):
    return {"col": (s[0] // 2, s[1]), "row": (s[0], s[1] // 2), "win": (s[0], s[1] // 2) + tuple(s[2:])}[kind]


def _wire_shape(kind, hs):
    return (N_CHIPS, hs[0], hs[1] // N_CHIPS) if kind == "col" else tuple(hs)


def _wire_shard(kind, ref, j):
    if kind == "row":
        r = ref.shape[0] // N_CHIPS
        return ref.at[pl.ds(pl.multiple_of(j * r, 16), r), :]
    return ref.at[j]


def _wire_shard_shape(kind, ws):
    return (ws[0] // N_CHIPS, ws[1]) if kind == "row" else tuple(ws[1:])


class _GradGroup:
    def __init__(self, tag, names, kinds, arrs, where, south):
        self.tag, self.names, self.kinds, self.arrs, self.where, self.south = tag, list(names), list(kinds), list(arrs), where, south
        self.h32, self.wire, self.final = None, None, None
        self.from_chips = [None] * len(arrs)

    def swap(self):
        outs = [jax.ShapeDtypeStruct(_half_shape(k, g.shape), g.dtype) for k, g in zip(self.kinds, self.arrs)]

        def done(from_sibling):
            sums = [_sum_pair(f"{self.tag}_{n}_pair", k, g, r, self.where)
                    for n, k, g, r in zip(self.names, self.kinds, self.arrs, from_sibling)]
            self.h32, self.wire = [s[0] for s in sums], [s[1] for s in sums]

        return _swap_comm(self.arrs, self.kinds, outs, done)

    def chips(self, which):
        def done(got):
            for a, r in zip(which, got):
                self.from_chips[a] = r

        return _chips_comm([self.wire[a] for a in which], [self.kinds[a] for a in which], done)

    def share(self):
        parts = [_sum_chips(f"{self.tag}_{n}_chips", k, h, r, self.where)
                 for n, k, h, r in zip(self.names, self.kinds, self.h32, self.from_chips)]

        def done(theirs):
            axis = lambda k: 1 if k == "row" else 0
            self.final = [jnp.concatenate([jnp.where(self.south, mine, other), jnp.where(self.south, other, mine)], axis=axis(k))
                          for k, mine, other in zip(self.kinds, parts, theirs)]

        return _swap_comm(parts, [None] * len(parts), [jax.ShapeDtypeStruct(p.shape, p.dtype) for p in parts], done)

    def grads(self):
        return dict(zip(self.names, self.final))


def _esum(name, grid, block, ins, outs, where):
    n_in = len(ins)

    def body(s_ref, *refs):
        tot = refs[0][...].astype(F32)
        for r in refs[1:n_in]:
            tot = tot + r[...].astype(F32)
        for o in refs[n_in:]:
            o[...] = tot.astype(o.dtype)

    spec = lambda nd, imap: pl.BlockSpec((None,) * (nd - 2) + tuple(block), imap)
    return pl.pallas_call(
        body, name=name,
        grid_spec=pltpu.PrefetchScalarGridSpec(
            num_scalar_prefetch=1, grid=grid,
            in_specs=[spec(a.ndim, m) for a, m in ins], out_specs=[spec(len(s), m) for s, _, m in outs]),
        out_shape=[jax.ShapeDtypeStruct(s, dt) for s, dt, _ in outs],
        compiler_params=_cparams(("parallel",) * len(grid)),
    )(where, *[a for a, _ in ins])


def _sum_pair(name, kind, g, r1, where):
    hshape = _half_shape(kind, g.shape)
    wshape = _wire_shape(kind, hshape)
    if kind == "col":
        block = (_tile(hshape[0], 512, 16), hshape[1] // N_CHIPS)
        nb = hshape[0] // block[0]
        grid = (nb, N_CHIPS)
        gmap = lambda i, j, s: (s[0] * nb + i, j)
        hmap = lambda i, j, s: (i, j)
        wmap = lambda i, j, s: (j, i, 0)
    elif kind == "row":
        block = (_tile(hshape[0], 512, 16), hshape[1])
        grid = (hshape[0] // block[0],)
        gmap = lambda i, s: (i, s[0])
        hmap = wmap = lambda i, s: (i, 0)
    else:
        block = (_tile(hshape[1], 512, 16), hshape[2])
        nb = hshape[1] // block[0]
        grid = (hshape[0], nb)
        gmap = lambda q, i, s: (q, s[0] * nb + i, 0)
        hmap = wmap = lambda q, i, s: (q, i, 0)
    return _esum(name, grid, block, [(g, gmap), (r1, hmap)], [(hshape, F32, hmap), (wshape, BF16, wmap)], where)


def _sum_chips(name, kind, h32, r2, where):
    tshape = tuple(r2.shape[1:])
    block = (_tile(tshape[0], 512, 16), tshape[1])
    nb = tshape[0] // block[0]
    if kind == "col":
        hmap = lambda i, s: (i, s[1])
    elif kind == "row":
        hmap = lambda i, s: (s[1] * nb + i, 0)
    else:
        hmap = lambda i, s: (s[1], i, 0)
    rmap = lambda k: (lambda i, s: (k, i, 0))
    return _esum(name, (nb,), block, [(h32, hmap)] + [(r2, rmap(k)) for k in range(3)],
                 [(tshape, F32, lambda i, s: (i, 0))], where)[0]


def _allreduce_small(name, vec):
    rows_n = vec.shape[0]

    def body(x_ref, sum_ref, all_ref, send, recv, local_sem):
        x, y, c = _axes()
        me, sibling = (x, y, c), (x, y, 1 - c)
        chips = [(1 - x, y), (x, 1 - y), (1 - x, 1 - y)]

        def rows(px, py, pc):
            return all_ref.at[pl.ds(pl.multiple_of((4 * px + 2 * py + pc) * rows_n, 8), rows_n), :]

        def copy(k, block, to, src=None):
            return _remote(rows(*block) if src is None else src, rows(*block), send, recv, k, to)

        mine = pltpu.make_async_copy(x_ref, rows(*me), local_sem)
        mine.start()
        first = [copy(0, me, sibling, src=x_ref)] + [copy(1 + j, me, (*chip, c), src=x_ref) for j, chip in enumerate(chips)]
        for cp in first:
            cp.start()
        passed = [copy(4 + j, (*chip, c), sibling) for j, chip in enumerate(chips)]
        for j, chip in enumerate(chips):
            copy(1 + j, (*chip, c), me).wait_recv()
            passed[j].start()
        copy(0, sibling, me).wait_recv()
        for j, chip in enumerate(chips):
            copy(4 + j, (*chip, 1 - c), me).wait_recv()
        for cp in first + passed:
            cp.wait_send()
        mine.wait()
        tot = all_ref[0:rows_n, :]
        for d in range(1, 8):
            tot = tot + all_ref[d * rows_n:(d + 1) * rows_n, :]
        sum_ref[...] = tot

    vm = pl.BlockSpec(memory_space=pltpu.VMEM)
    return pl.pallas_call(
        body, name=name, in_specs=[vm], out_specs=[vm, vm],
        out_shape=[jax.ShapeDtypeStruct(vec.shape, F32), jax.ShapeDtypeStruct((8 * rows_n, LANES), F32)],
        scratch_shapes=[pltpu.SemaphoreType.DMA((7,)), pltpu.SemaphoreType.DMA((7,)), pltpu.SemaphoreType.DMA],
        compiler_params=pltpu.CompilerParams(vmem_limit_bytes=VMEM_LIMIT),
    )(vec)[0]


def _adamw(name, w, g, m, v):
    shape = w.shape
    rows_n, cols = shape[-2], shape[-1]
    lead = math.prod(shape[:-2])
    tr = _tile(rows_n, 256, 8)

    def body(w_ref, g_ref, m_ref, v_ref, d_ref, m2_ref, v2_ref):
        gv = g_ref[...]
        m2 = ADAM_B1 * m_ref[...] + (1.0 - ADAM_B1) * gv
        v2 = ADAM_B2 * v_ref[...] + (1.0 - ADAM_B2) * jnp.square(gv)
        m_hat = m2 / (1.0 - ADAM_B1 ** ADAM_STEP)
        v_hat = v2 / (1.0 - ADAM_B2 ** ADAM_STEP)
        d_ref[...] = -ADAM_LR * (m_hat / (jnp.sqrt(v_hat) + ADAM_EPS) + ADAM_WD * w_ref[...])
        m2_ref[...] = m2
        v2_ref[...] = v2

    spec = pl.BlockSpec((None, tr, cols), lambda l, i: (l, i, 0))
    flat = (lead, rows_n, cols)
    outs = pl.pallas_call(
        body, name=name, grid=(lead, rows_n // tr), in_specs=[spec] * 4, out_specs=[spec] * 3,
        out_shape=[jax.ShapeDtypeStruct(flat, F32)] * 3,
        compiler_params=_cparams(("parallel", "parallel")),
    )(*[t.reshape(flat) for t in (w, g, m, v)])
    return [o.reshape(shape) for o in outs]


WEIGHTS = ("ffn1_norm", "ffn1_w_gate", "ffn1_w_up", "ffn1_w_down", "mix_norm", "w_in", "pool_w", "pool_scale", "conv_w",
           "conv_b", "dt_bias", "a_log", "d_skip", "ssd_norm", "w_out", "ffn2_norm", "ffn2_w_gate", "ffn2_w_up",
           "ffn2_w_down", "final_norm")
BIG = {"ffn1_w_gate": "col", "ffn1_w_up": "col", "ffn1_w_down": "row", "w_in": "win", "w_out": "row",
       "ffn2_w_gate": "col", "ffn2_w_up": "col", "ffn2_w_down": "row"}
SMALL = tuple(n for n in WEIGHTS if n not in BIG and n != "conv_w")
REF_DT = 3072


def _pack(parts):
    flat = jnp.concatenate([p.reshape(-1) for p in parts])
    rows_n = -(-flat.shape[0] // (8 * LANES)) * 8
    return jnp.pad(flat, (0, rows_n * LANES - flat.shape[0])).reshape(rows_n, LANES)


def _unpack(block, shapes):
    flat, out, at = block.reshape(-1), [], 0
    for s in shapes:
        n = math.prod(s)
        out.append(flat[at:at + n].reshape(s))
        at += n
    return out


def _train_step(a):
    depth = a["ffn1_norm"].shape[0]
    x_id, y_id, c_id = _axes()
    chip = 2 * x_id + y_id
    where = jnp.stack([c_id, chip]).astype(jnp.int32)

    big = list(BIG)
    south = c_id == 0
    full = [dict() for _ in range(depth)]
    conv_full = []

    def fetch(l, names):
        def done(blocks):
            for n, g in zip(names, blocks):
                full[l][n] = jnp.concatenate([g[s] for s in range(N_CHIPS)], axis=0 if BIG[n] == "row" else 1)

        return [_gather_comm([a[n][l].astype(BF16) for n in names], done)] if l < depth else []

    conv_done = lambda blocks: conv_full.append(jnp.concatenate([blocks[0][s] for s in range(N_CHIPS)], axis=2))
    _run_comm("gather_layer0", fetch(0, big) + [_gather_comm([a["conv_w"]], conv_done)])
    conv_w = conv_full[0]
    heads128 = lambda v: jnp.pad(v, ((0, 0), (0, LANES - SSD_HEADS)))
    dt_bias, a_log, d_skip = heads128(a["dt_bias"]), heads128(a["a_log"]), heads128(a["d_skip"])

    def mixer_params(l):
        w_in = full[l]["w_in"]
        w_main = jnp.concatenate([w_in[:, :REF_DT], w_in[:, REF_DT + SSD_HEADS:]], axis=1)
        w_dt = jnp.pad(w_in[:, REF_DT:REF_DT + SSD_HEADS], ((0, 0), (0, LANES - SSD_HEADS)))
        return dict(mix_norm=a["mix_norm"][l][None], w_main=w_main, w_dt=w_dt, pool_w=a["pool_w"][l],
                    pool_scale=a["pool_scale"][l][None], conv_w=conv_w[l], conv_b=a["conv_b"][l][None],
                    dt_bias=dt_bias[l][None], a_log=a_log[l][None], d_skip=d_skip[l][None],
                    ssd_norm=a["ssd_norm"][l][None], w_out=full[l]["w_out"])

    def ffn_params(l, which):
        return (a[which + "_norm"][l][None], full[l][which + "_w_gate"], full[l][which + "_w_up"], full[l][which + "_w_down"])

    h = a["x"][0]
    saved, mixer_p = [], []
    ffn1, mix, ffn2 = ["ffn1_w_gate", "ffn1_w_up", "ffn1_w_down"], ["w_in", "w_out"], ["ffn2_w_gate", "ffn2_w_up", "ffn2_w_down"]
    for l in range(depth):
        mixer_p.append(mixer_params(l))
        carry = lambda names: (lambda call: fetch(l + 1, names[:-1] if call in ("up", "in") else names[-1:]))
        h, s1 = _ffn_fwd(f"l{l}_ffn1", h, *ffn_params(l, "ffn1"), hooks=carry(ffn1))
        h, sm = _mixer_fwd(f"l{l}_mix", h, mixer_p[l], hooks=carry(mix))
        h, s2 = _ffn_fwd(f"l{l}_ffn2", h, *ffn_params(l, "ffn2"), hooks=carry(ffn2))
        saved.append((s1, sm, s2))
    loss_part, dh, dh_bf, dfinal = _loss_head("loss_head", h, a["final_norm"][None], a["loss_target"][0])

    small = {n: [None] * depth for n in SMALL if n != "final_norm"}
    small["conv_w"] = [None] * depth
    groups = []

    def grouper(tag, names):
        def make(*arrs):
            groups.append(_GradGroup(tag, names, [BIG[n] for n in names], arrs, where, south))
            return groups[-1]
        return make

    above = None
    for l in reversed(range(depth)):
        s1, sm, s2 = saved[l]
        hooks = _no_hooks if above is None else (lambda call, g=above: {"dhm": lambda: [g.chips([0, 1])],
                                                                        "dwd": lambda: [g.chips([2])],
                                                                        "dwgu": lambda: [g.share()]}.get(call, list)())
        dh, dh_bf, small["ffn2_norm"][l], f2 = _ffn_bwd(f"l{l}_ffn2", s2, *ffn_params(l, "ffn2"), dh, dh_bf, hooks=hooks,
                                                        group=grouper(f"l{l}_ffn2", ffn2))
        hooks = lambda call, g=f2: {"dmix": lambda: [g.chips([1])], "dwout": lambda: [g.chips([0])],
                                    "dwin": lambda: [g.chips([2])], "du": lambda: [g.share()]}.get(call, list)()
        dh, dh_bf, g, mx = _mixer_bwd(f"l{l}_mix", sm, mixer_p[l], dh, dh_bf, hooks=hooks, group=grouper(f"l{l}_mix", mix))
        for n in ("mix_norm", "pool_w", "pool_scale", "conv_w", "conv_b", "ssd_norm"):
            small[n][l] = g[n]
        for n in ("dt_bias", "a_log", "d_skip"):
            small[n][l] = g[n][:, :SSD_HEADS]
        hooks = lambda call, g=mx: {"dhm": lambda: [g.chips([0, 1])], "dwd": lambda: [g.share()]}.get(call, list)()
        dh, dh_bf, small["ffn1_norm"][l], above = _ffn_bwd(f"l{l}_ffn1", s1, *ffn_params(l, "ffn1"), dh, dh_bf, hooks=hooks,
                                                           group=grouper(f"l{l}_ffn1", ffn1))
    grad_x = dh[None]
    _run_comm("tail_chips", [above.chips([0, 1, 2])])
    _run_comm("tail_share", [above.share()])
    per_layer = {}
    for grp in groups:
        per_layer.update({(grp.tag, n): g for n, g in grp.grads().items()})
    grad = {n: jnp.stack([per_layer[(f"l{l}_{'mix' if n in mix else n[:4]}", n)] for l in range(depth)]) for n in big}

    small_full = {n: jnp.stack([t.reshape(a[n].shape[1:]) for t in small[n]]) for n in SMALL if n != "final_norm"}
    small_full["final_norm"] = dfinal.reshape(a["final_norm"].shape)
    conv_full = jnp.stack(small["conv_w"])
    shapes = [a[n].shape for n in SMALL] + [conv_full.shape]
    reduced = _unpack(_allreduce_small("allreduce_small", _pack([small_full[n] for n in SMALL] + [conv_full])), shapes)
    grad.update(zip(SMALL, reduced[:-1]))
    shard = a["conv_w"].shape[2]
    grad["conv_w"] = lax.dynamic_slice_in_dim(reduced[-1], chip * shard, shard, axis=2)

    delta, new_m, new_v = {}, {}, {}
    for n in big + ["conv_w"]:
        delta[n], new_m[n], new_v[n] = _adamw(f"adamw_{n}", a[n], grad[n], a["m_" + n], a["v_" + n])
    packed = [_pack([a[pre + n] for n in SMALL]) for pre in ("", "m_", "v_")]
    outs = _adamw("adamw_small", packed[0], _pack([grad[n] for n in SMALL]), packed[1], packed[2])
    for store, block in zip((delta, new_m, new_v), outs):
        store.update(zip(SMALL, _unpack(block, [a[n].shape for n in SMALL])))

    loss = lax.psum(loss_part[0, 0], ("x", "y", "c"))
    return (loss, grad_x, *[grad[n] for n in WEIGHTS], *[delta[n] for n in WEIGHTS], *[new_m[n] for n in WEIGHTS],
            *[new_v[n] for n in WEIGHTS])


def kernel(x, ffn1_norm, ffn1_w_gate, ffn1_w_up, ffn1_w_down, mix_norm, w_in, pool_w, pool_scale, conv_w, conv_b, dt_bias, a_log, d_skip, ssd_norm, w_out, ffn2_norm, ffn2_w_gate, ffn2_w_up, ffn2_w_down, final_norm, loss_target, m_ffn1_norm, m_ffn1_w_gate, m_ffn1_w_up, m_ffn1_w_down, m_mix_norm, m_w_in, m_pool_w, m_pool_scale, m_conv_w, m_conv_b, m_dt_bias, m_a_log, m_d_skip, m_ssd_norm, m_w_out, m_ffn2_norm, m_ffn2_w_gate, m_ffn2_w_up, m_ffn2_w_down, m_final_norm, v_ffn1_norm, v_ffn1_w_gate, v_ffn1_w_up, v_ffn1_w_down, v_mix_norm, v_w_in, v_pool_w, v_pool_scale, v_conv_w, v_conv_b, v_dt_bias, v_a_log, v_d_skip, v_ssd_norm, v_w_out, v_ffn2_norm, v_ffn2_w_gate, v_ffn2_w_up, v_ffn2_w_down, v_final_norm):
    return _train_step(dict(locals()))
```

```python
import functools
import math

import jax
import jax.numpy as jnp
from jax import lax
from jax.experimental import pallas as pl
from jax.experimental.pallas import tpu as pltpu

F32 = jnp.float32
BF16 = jnp.bfloat16
MESH_ID = pl.DeviceIdType.MESH

RMS_EPS = 1e-6
POOL_WINDOWS = (2, 4, 8, 16)
LANES = 128
HEAD_DIM = 64
SSD_HEADS = 16
SSD_CHUNK = 256
ATTN_BLOCK = 128
HALO = 16
EXP_UNDERFLOW = -105.0
VMEM_LIMIT = 56 * 1024 * 1024
MM_SUB = 256
N_CHIPS = 4

ADAM_LR = 0.001
ADAM_B1 = 0.9
ADAM_B2 = 0.999
ADAM_EPS = 1e-08
ADAM_WD = 0.01
ADAM_STEP = 10

C_POOL, C_Z, C_XBC, C_Q, C_K, C_V, C_END = 0, 512, 1536, 3072, 3584, 4096, 4608


def _cparams(sem):
    return pltpu.CompilerParams(dimension_semantics=sem, vmem_limit_bytes=VMEM_LIMIT)


def _tile(dim, pref, unit=LANES):
    if dim <= pref:
        return dim
    t = (pref // unit) * unit
    while t > unit and dim % t:
        t -= unit
    assert dim % t == 0, (dim, pref)
    return t


def _sigmoid(x):
    return 0.5 * jnp.tanh(0.5 * x) + 0.5


def _silu(x):
    return x * _sigmoid(x)


def _dsilu(x):
    s = _sigmoid(x)
    return s * (1.0 + x * (1.0 - s))


def _dot(a, b):
    return jnp.dot(a.astype(BF16), b.astype(BF16), preferred_element_type=F32)


def _dot_nt(a, b):
    return lax.dot_general(a.astype(BF16), b.astype(BF16), (((1,), (1,)), ((), ())), preferred_element_type=F32)


def _dot_tn(a, b):
    return lax.dot_general(a.astype(BF16), b.astype(BF16), (((0,), (0,)), ((), ())), preferred_element_type=F32)


def _split3(x):
    hi = x.astype(BF16)
    r = x - hi.astype(F32)
    mid = r.astype(BF16)
    lo = (r - mid.astype(F32)).astype(BF16)
    return hi, mid, lo


def _dot3(x, m):
    hi, mid, lo = _split3(x)
    dn = (((1,), (0,)), ((), ()))
    f = lambda p: lax.dot_general(p, m, dn, preferred_element_type=F32)
    return f(hi) + f(mid) + f(lo)


def _dot3_left(m, x):
    hi, mid, lo = _split3(x)
    dn = (((1,), (0,)), ((), ()))
    f = lambda p: lax.dot_general(m, p, dn, preferred_element_type=F32)
    return f(hi) + f(mid) + f(lo)


def _iota(shape, axis):
    return lax.broadcasted_iota(jnp.int32, shape, axis)


def _col(x, h):
    return jnp.sum(jnp.where(_iota(x.shape, 1) == h, x, 0.0), axis=1, keepdims=True)


def _roll_down(x, k):
    return x if k == 0 else pltpu.roll(x, k, 0)


def _roll_up(x, k):
    return x if k == 0 else pltpu.roll(x, x.shape[0] - k, 0)


class _Comm:
    def __init__(self, ins, outs, n_remote, n_local, make, done):
        self.ins, self.outs, self.n_remote, self.n_local, self.make, self.done = list(ins), list(outs), n_remote, n_local, make, done


def _comm_plan(hosted):
    ins = [a for cm in hosted for a in cm.ins]
    outs = [o for cm in hosted for o in cm.outs]
    n_remote = sum(cm.n_remote for cm in hosted)
    n_local = sum(cm.n_local for cm in hosted)

    def build(in_refs, out_refs, send, recv, loc):
        starts, waits, i0, o0, r0, l0 = [], [], 0, 0, 0, 0
        for cm in hosted:
            s, w = cm.make(in_refs[i0:i0 + len(cm.ins)], out_refs[o0:o0 + len(cm.outs)], send, recv, loc, r0, l0)
            starts, waits = starts + s, waits + w
            i0, o0, r0, l0 = i0 + len(cm.ins), o0 + len(cm.outs), r0 + cm.n_remote, l0 + cm.n_local
        return starts, waits

    def deliver(results):
        o0 = 0
        for cm in hosted:
            cm.done(results[o0:o0 + len(cm.outs)])
            o0 += len(cm.outs)

    sems = [pltpu.SemaphoreType.DMA((max(n_remote, 1),)), pltpu.SemaphoreType.DMA((max(n_remote, 1),)),
            pltpu.SemaphoreType.DMA((max(n_local, 1),))]
    return ins, outs, sems, build, deliver


def _mm(name, a_list, b_list, *, nt, epilogue, out_dtypes, acc_of=None, extras=(), hosted=(), ta=False,
        tm=1024, tn=512, tk=1024):
    n_pairs = len(a_list)
    acc_of = list(acc_of) if acc_of is not None else [0] * n_pairs
    n_acc = max(acc_of) + 1
    m_dim, k_dim = a_list[0].shape[::-1] if ta else a_list[0].shape
    n_dim = b_list[0].shape[0] if nt else b_list[0].shape[1]
    tm, tn, tk = _tile(m_dim, tm, 8), _tile(n_dim, tn), _tile(k_dim, tk)
    nk = k_dim // tk
    n_ex, n_out = len(extras), len(out_dtypes)
    hosted = list(hosted)
    c_ins, c_outs, c_sems, c_build, c_deliver = _comm_plan(hosted)
    n_ci, n_co = len(c_ins), len(c_outs)
    n_scr = 0 if nk == 1 else n_acc
    grid = (m_dim // tm, n_dim // tn, nk)
    sub = MM_SUB if (nk == 1 and tn > MM_SUB and tn % MM_SUB == 0) else tn

    def body(*refs):
        a_refs = refs[:n_pairs]
        b_refs = refs[n_pairs:2 * n_pairs]
        e_refs = refs[2 * n_pairs:2 * n_pairs + n_ex]
        ci_refs = refs[2 * n_pairs + n_ex:2 * n_pairs + n_ex + n_ci]
        first_out = 2 * n_pairs + n_ex + n_ci
        o_refs = refs[first_out:first_out + n_out]
        co_refs = refs[first_out + n_out:first_out + n_out + n_co]
        acc_refs = refs[first_out + n_out + n_co:first_out + n_out + n_co + n_scr]
        sems = refs[first_out + n_out + n_co + n_scr:]
        if hosted:
            at = [pl.program_id(d) for d in range(3)]
            starts, waits = c_build(ci_refs, co_refs, *sems)

            @pl.when((at[0] == 0) & (at[1] == 0) & (at[2] == 0))
            def _():
                for cp in starts:
                    cp.start()

        if nk == 1:
            for s in range(tn // sub):
                cs = slice(s * sub, (s + 1) * sub)
                accs = [None] * n_acc
                for p in range(n_pairs):
                    a = a_refs[p][...]
                    d = _dot_tn(a, b_refs[p][:, cs]) if ta else _dot_nt(a, b_refs[p][cs, :]) if nt else _dot(a, b_refs[p][:, cs])
                    accs[acc_of[p]] = d if accs[acc_of[p]] is None else accs[acc_of[p]] + d
                outs = epilogue(accs, [e[:, cs] for e in e_refs])
                for o_ref, o in zip(o_refs, outs):
                    o_ref[:, cs] = o.astype(o_ref.dtype)
        else:
            k = pl.program_id(2)

            @pl.when(k == 0)
            def _():
                for acc in acc_refs:
                    acc[...] = jnp.zeros_like(acc)

            for p in range(n_pairs):
                a, b = a_refs[p][...], b_refs[p][...]
                acc_refs[acc_of[p]][...] += _dot_tn(a, b) if ta else _dot_nt(a, b) if nt else _dot(a, b)

            @pl.when(k == nk - 1)
            def _():
                outs = epilogue([acc[...] for acc in acc_refs], [e[...] for e in e_refs])
                for o_ref, o in zip(o_refs, outs):
                    o_ref[...] = o.astype(o_ref.dtype)

        if hosted:
            @pl.when((at[0] == grid[0] - 1) & (at[1] == grid[1] - 1) & (at[2] == grid[2] - 1))
            def _():
                for wait in waits:
                    wait()

    a_spec = pl.BlockSpec((tk, tm), lambda i, j, k: (k, i)) if ta else pl.BlockSpec((tm, tk), lambda i, j, k: (i, k))
    b_spec = pl.BlockSpec((tn, tk), lambda i, j, k: (j, k)) if nt else pl.BlockSpec((tk, tn), lambda i, j, k: (k, j))
    t_spec = pl.BlockSpec((tm, tn), lambda i, j, k: (i, j))
    results = pl.pallas_call(
        body, name=name, grid=grid,
        in_specs=[a_spec] * n_pairs + [b_spec] * n_pairs + [t_spec] * n_ex + _any_specs(n_ci),
        out_specs=[t_spec] * n_out + _any_specs(n_co),
        out_shape=[jax.ShapeDtypeStruct((m_dim, n_dim), dt) for dt in out_dtypes] + c_outs,
        scratch_shapes=[pltpu.VMEM((tm, tn), F32)] * n_scr + (c_sems if hosted else []),
        compiler_params=_cparams(("arbitrary",) * 3 if hosted else ("parallel", "parallel", "arbitrary")),
    )(*a_list, *b_list, *extras, *c_ins)
    c_deliver(results[n_out:])
    return results[:n_out]


def _rms_fwd(name, h, g):
    s_len, d = h.shape
    ts = _tile(s_len, 512, 8)

    def body(h_ref, g_ref, u_ref):
        x = h_ref[...]
        rstd = lax.rsqrt(jnp.mean(x * x, axis=-1, keepdims=True) + RMS_EPS)
        u_ref[...] = (x * rstd * g_ref[...]).astype(BF16)

    return pl.pallas_call(
        body, name=name, grid=(s_len // ts,),
        in_specs=[pl.BlockSpec((ts, d), lambda i: (i, 0)), pl.BlockSpec((1, d), lambda i: (0, 0))],
        out_specs=pl.BlockSpec((ts, d), lambda i: (i, 0)),
        out_shape=jax.ShapeDtypeStruct((s_len, d), BF16),
        compiler_params=_cparams(("parallel",)),
    )(h, g)


def _rms_bwd(name, h, du, dres, g):
    s_len, d = h.shape
    ts = _tile(s_len, 256, 8)

    def body(h_ref, du_ref, dres_ref, g_ref, dh_ref, dhb_ref, dg_ref):
        x = h_ref[...]
        rstd = lax.rsqrt(jnp.mean(x * x, axis=-1, keepdims=True) + RMS_EPS)
        n = x * rstd
        dn = du_ref[...] * g_ref[...]
        dh = dres_ref[...] + rstd * (dn - n * jnp.mean(dn * n, axis=-1, keepdims=True))
        dh_ref[...] = dh
        dhb_ref[...] = dh.astype(BF16)

        @pl.when(pl.program_id(0) == 0)
        def _():
            dg_ref[...] = jnp.zeros_like(dg_ref)

        dg_ref[...] += jnp.sum(du_ref[...] * n, axis=0, keepdims=True)

    row = pl.BlockSpec((ts, d), lambda i: (i, 0))
    vec = pl.BlockSpec((1, d), lambda i: (0, 0))
    return pl.pallas_call(
        body, name=name, grid=(s_len // ts,),
        in_specs=[row, row, row, vec], out_specs=[row, row, vec],
        out_shape=[jax.ShapeDtypeStruct((s_len, d), F32), jax.ShapeDtypeStruct((s_len, d), BF16),
                   jax.ShapeDtypeStruct((1, d), F32)],
        compiler_params=_cparams(("arbitrary",)),
    )(h, du, dres, g)


def _loss_head(name, h, g, target):
    s_len, d = h.shape
    ts = _tile(s_len, 256, 8)

    def body(h_ref, g_ref, t_ref, loss_ref, dh_ref, dhb_ref, dg_ref):
        x = h_ref[...]
        rstd = lax.rsqrt(jnp.mean(x * x, axis=-1, keepdims=True) + RMS_EPS)
        n = x * rstd
        err = n * g_ref[...] - t_ref[...]
        dy = err * (1.0 / d)
        dn = dy * g_ref[...]
        dh = rstd * (dn - n * jnp.mean(dn * n, axis=-1, keepdims=True))
        dh_ref[...] = dh
        dhb_ref[...] = dh.astype(BF16)

        @pl.when(pl.program_id(0) == 0)
        def _():
            dg_ref[...] = jnp.zeros_like(dg_ref)
            loss_ref[...] = jnp.zeros_like(loss_ref)

        dg_ref[...] += jnp.sum(dy * n, axis=0, keepdims=True)
        part = jnp.sum(jnp.sum(err * err, axis=1, keepdims=True), axis=0, keepdims=True) * (0.5 / d)
        loss_ref[...] += jnp.broadcast_to(part, loss_ref.shape)

    row = pl.BlockSpec((ts, d), lambda i: (i, 0))
    vec = pl.BlockSpec((1, d), lambda i: (0, 0))
    lspec = pl.BlockSpec((1, LANES), lambda i: (0, 0))
    return pl.pallas_call(
        body, name=name, grid=(s_len // ts,),
        in_specs=[row, vec, row], out_specs=[lspec, row, row, vec],
        out_shape=[jax.ShapeDtypeStruct((1, LANES), F32), jax.ShapeDtypeStruct((s_len, d), F32),
                   jax.ShapeDtypeStruct((s_len, d), BF16), jax.ShapeDtypeStruct((1, d), F32)],
        compiler_params=_cparams(("arbitrary",)),
    )(h, g, target)


def _no_hooks(_):
    return []


def _ffn_fwd(tag, h, g, wg, wu, wd, hooks=_no_hooks):
    u = _rms_fwd(tag + "_norm", h, g)

    def up(accs, _):
        a, b = accs
        return a, b, _silu(a) * b

    a, b, hm = _mm(tag + "_up", [u, u], [wg, wu], nt=False, acc_of=[0, 1], epilogue=up, out_dtypes=[BF16, BF16, BF16],
                   hosted=hooks("up"))
    (h2,) = _mm(tag + "_down", [hm], [wd], nt=False, extras=[h], epilogue=lambda accs, ex: [ex[0] + 0.5 * accs[0]],
                out_dtypes=[F32], hosted=hooks("down"), tm=1024, tn=1024, tk=1408)
    return h2, (h, u, a, b, hm)


def _ffn_bwd(tag, saved, g, wg, wu, wd, dh2, dh2_bf, hooks=_no_hooks, group=None):
    h, u, a, b, hm = saved

    def dact(accs, ex):
        af, bf = ex[0].astype(F32), ex[1].astype(F32)
        dhm = 0.5 * accs[0]
        return dhm * bf * _dsilu(af), dhm * _silu(af)

    da, db = _mm(tag + "_dhm", [dh2_bf], [wd], nt=True, extras=[a, b], epilogue=dact, out_dtypes=[BF16, BF16],
                 hosted=hooks("dhm"), tk=2048)
    (dwd,) = _mm(tag + "_dwd", [hm], [dh2_bf], nt=False, ta=True, epilogue=lambda accs, _: [0.5 * accs[0]],
                 out_dtypes=[F32], hosted=hooks("dwd"), tm=1408, tn=1024, tk=512)
    dwg, dwu = _mm(tag + "_dwgu", [u, u], [da, db], nt=False, ta=True, acc_of=[0, 1], epilogue=lambda accs, _: accs,
                   out_dtypes=[F32, F32], hosted=hooks("dwgu"), tm=2048, tn=512, tk=1024)
    mine = group(dwg, dwu, dwd) if group else (dwg, dwu, dwd)
    (du,) = _mm(tag + "_du", [da, db], [wg, wu], nt=True, epilogue=lambda accs, _: accs, out_dtypes=[F32],
                hosted=[mine.swap()] if group else [], tm=1024, tn=1024, tk=1408)
    dh, dh_bf, dg = _rms_bwd(tag + "_dnorm", h, du, dh2, g)
    return dh, dh_bf, dg, mine


def _softplus(x):
    e = jnp.exp(-jnp.abs(x))
    u = 1.0 + e
    log1p_e = jnp.where(u == 1.0, e, jnp.log(u) * (e / jnp.where(u == 1.0, 1.0, u - 1.0)))
    return jnp.maximum(x, 0.0) + log1p_e


def _row_spec(ts, width, colblock):
    return pl.BlockSpec((ts, width), lambda i: (i, colblock))


def _halo_before_spec(ts, width, colblock):
    r = ts // HALO
    return pl.BlockSpec((HALO, width), lambda i: (jnp.maximum(i * r - 1, 0), colblock))


def _halo_after_spec(ts, width, colblock, s_len):
    r = ts // HALO
    return pl.BlockSpec((HALO, width), lambda i: (jnp.minimum((i + 1) * r, s_len // HALO - 1), colblock))


def _const_spec(shape):
    nd = len(shape)
    return pl.BlockSpec(shape, lambda *_: (0,) * nd)


def _window_sum(e, win, roll):
    s, sh = e, 1
    while sh < win:
        s = s + roll(s, sh)
        sh *= 2
    return s


def _pool_center(ext, x, t, gi, win):
    sl = slice(gi * LANES, (gi + 1) * LANES)
    s = _window_sum(ext[:, sl], win, _roll_down)
    cnt = jnp.minimum(t + 1, win).astype(F32)
    return s[HALO:] / cnt - x[:, sl]


def _pool_fwd(name, proj, pw, scale):
    s_len = proj.shape[0]
    ts = _tile(s_len, 512, 8)
    width = len(POOL_WINDOWS) * LANES

    def body(x_ref, hb_ref, pw_ref, sc_ref, o_ref):
        i = pl.program_id(0)
        x = x_ref[...]
        ext = jnp.concatenate([jnp.where(i == 0, 0.0, hb_ref[...]), x], axis=0)
        t = i * ts + _iota((ts, 1), 0)
        for gi, win in enumerate(POOL_WINDOWS):
            sl = slice(gi * LANES, (gi + 1) * LANES)
            c = _pool_center(ext, x, t, gi, win)
            o_ref[:, sl] = (_dot(c, pw_ref[gi]) * sc_ref[:, sl]).astype(o_ref.dtype)

    return pl.pallas_call(
        body, name=name, grid=(s_len // ts,),
        in_specs=[_row_spec(ts, width, 0), _halo_before_spec(ts, width, 0), _const_spec(pw.shape), _const_spec(scale.shape)],
        out_specs=_row_spec(ts, width, 0),
        out_shape=jax.ShapeDtypeStruct((s_len, width), BF16),
        compiler_params=_cparams(("parallel",)),
    )(proj, proj, pw, scale)


def _pool_bwd(name, proj, dmixed, pw, scale):
    s_len = proj.shape[0]
    ts = _tile(s_len, 512, 8)
    n_tiles = s_len // ts
    width = len(POOL_WINDOWS) * LANES

    def body(x_ref, hb_ref, d_ref, da_ref, pw_ref, sc_ref, dx_ref, dpw_ref, dsc_ref):
        i = pl.program_id(0)
        x = x_ref[...]
        ext = jnp.concatenate([jnp.where(i == 0, 0.0, hb_ref[...]), x], axis=0)
        dout = d_ref[...]
        dext = jnp.concatenate([dout, jnp.where(i == n_tiles - 1, 0.0, da_ref[...])], axis=0)
        t = i * ts + _iota((ts, 1), 0)
        te = i * ts + _iota((ts + HALO, 1), 0)

        @pl.when(i == 0)
        def _():
            dpw_ref[...] = jnp.zeros_like(dpw_ref)
            dsc_ref[...] = jnp.zeros_like(dsc_ref)

        for gi, win in enumerate(POOL_WINDOWS):
            sl = slice(gi * LANES, (gi + 1) * LANES)
            c = _pool_center(ext, x, t, gi, win)
            o = _dot(c, pw_ref[gi])
            dsc_ref[:, sl] += jnp.sum(dout[:, sl] * o, axis=0, keepdims=True)
            do_ext = dext[:, sl] * sc_ref[:, sl]
            dc = _dot_nt(do_ext, pw_ref[gi])
            e = dc / jnp.minimum(te + 1, win).astype(F32)
            back = _window_sum(e, win, _roll_up)
            dx_ref[:, sl] = (back[:ts] - dc[:ts]).astype(dx_ref.dtype)
            dpw_ref[gi] += _dot(c.T, do_ext[:ts])

    return pl.pallas_call(
        body, name=name, grid=(n_tiles,),
        in_specs=[_row_spec(ts, width, 0), _halo_before_spec(ts, width, 0), _row_spec(ts, width, 0),
                  _halo_after_spec(ts, width, 0, s_len), _const_spec(pw.shape), _const_spec(scale.shape)],
        out_specs=[_row_spec(ts, width, 0), _const_spec(pw.shape), _const_spec(scale.shape)],
        out_shape=[jax.ShapeDtypeStruct((s_len, width), BF16), jax.ShapeDtypeStruct(pw.shape, F32),
                   jax.ShapeDtypeStruct(scale.shape, F32)],
        compiler_params=_cparams(("arbitrary",)),
    )(proj, proj, dmixed, dmixed, pw, scale)


def _conv_pre(ext, w_ref, b_ref):
    k_len = w_ref.shape[0]
    y = _roll_down(ext, k_len - 1) * w_ref[0:1, :]
    for k in range(1, k_len):
        y = y + _roll_down(ext, k_len - 1 - k) * w_ref[k:k + 1, :]
    return y + b_ref[...]


def _conv_fwd(name, proj, w, b):
    s_len = proj.shape[0]
    width = w.shape[1]
    ts = _tile(s_len, 512, 8)
    cb = C_XBC // width

    def body(x_ref, hb_ref, w_ref, b_ref, o_ref):
        i = pl.program_id(0)
        ext = jnp.concatenate([jnp.where(i == 0, 0.0, hb_ref[...]), x_ref[...]], axis=0)
        o_ref[...] = _silu(_conv_pre(ext, w_ref, b_ref)[HALO:])

    return pl.pallas_call(
        body, name=name, grid=(s_len // ts,),
        in_specs=[_row_spec(ts, width, cb), _halo_before_spec(ts, width, cb), _const_spec(w.shape), _const_spec(b.shape)],
        out_specs=_row_spec(ts, width, 0),
        out_shape=jax.ShapeDtypeStruct((s_len, width), F32),
        compiler_params=_cparams(("parallel",)),
    )(proj, proj, w, b)


def _conv_bwd(name, proj, dact, w, b):
    s_len = proj.shape[0]
    width = w.shape[1]
    k_len = w.shape[0]
    ts = _tile(s_len, 512, 8)
    n_tiles = s_len // ts
    cb = C_XBC // width

    def body(x_ref, hb_ref, ha_ref, d_ref, da_ref, w_ref, b_ref, dx_ref, dw_ref, db_ref):
        i = pl.program_id(0)
        last = i == n_tiles - 1
        ext = jnp.concatenate([jnp.where(i == 0, 0.0, hb_ref[...]), x_ref[...], jnp.where(last, 0.0, ha_ref[...])], axis=0)
        pre = _conv_pre(ext, w_ref, b_ref)[HALO:]
        dpre = jnp.concatenate([d_ref[...], jnp.where(last, 0.0, da_ref[...])], axis=0) * _dsilu(pre)

        @pl.when(i == 0)
        def _():
            dw_ref[...] = jnp.zeros_like(dw_ref)
            db_ref[...] = jnp.zeros_like(db_ref)

        dx = _roll_up(dpre, k_len - 1) * w_ref[0:1, :]
        for k in range(1, k_len):
            dx = dx + _roll_up(dpre, k_len - 1 - k) * w_ref[k:k + 1, :]
        dx_ref[...] = dx[:ts].astype(dx_ref.dtype)
        dtile = dpre[:ts]
        for k in range(k_len):
            xk = _roll_down(ext, k_len - 1 - k)[HALO:HALO + ts]
            dw_ref[k:k + 1, :] += jnp.sum(dtile * xk, axis=0, keepdims=True)
        db_ref[...] += jnp.sum(dtile, axis=0, keepdims=True)

    return pl.pallas_call(
        body, name=name, grid=(n_tiles,),
        in_specs=[_row_spec(ts, width, cb), _halo_before_spec(ts, width, cb), _halo_after_spec(ts, width, cb, s_len),
                  _row_spec(ts, width, 0), _halo_after_spec(ts, width, 0, s_len), _const_spec(w.shape), _const_spec(b.shape)],
        out_specs=[_row_spec(ts, width, 0), _const_spec(w.shape), _const_spec(b.shape)],
        out_shape=[jax.ShapeDtypeStruct((s_len, width), BF16), jax.ShapeDtypeStruct(w.shape, F32),
                   jax.ShapeDtypeStruct(b.shape, F32)],
        compiler_params=_cparams(("arbitrary",)),
    )(proj, proj, proj, dact, dact, w, b)


def _pair_cols(c0, c1, lo_half):
    return jnp.where(lo_half, c0, c1)


def _ssd_common(dtr_ref, bias_ref, alog_ref, acs_t_ref):
    chunk = dtr_ref.shape[0]
    xpre = dtr_ref[...] + bias_ref[...]
    dt = _softplus(xpre)
    a_neg = -jnp.exp(alog_ref[...])
    tri = _iota((chunk, chunk), 1) <= _iota((chunk, chunk), 0)
    a_cs = _dot3_left(tri.astype(BF16), dt * a_neg)
    acs_t_ref[...] = a_cs.T
    a_last = jnp.sum(jnp.where(_iota(a_cs.shape, 0) == chunk - 1, a_cs, 0.0), axis=0, keepdims=True)
    return xpre, dt, a_neg, tri, a_cs, a_last


def _ssd_specs(chunk, order):
    xs = pl.BlockSpec((chunk, 1024), lambda c: (order(c), 0))
    bm = pl.BlockSpec((chunk, 256), lambda c: (order(c), 4))
    cm = pl.BlockSpec((chunk, 256), lambda c: (order(c), 5))
    lanes = pl.BlockSpec((chunk, LANES), lambda c: (order(c), 0))
    return xs, bm, cm, lanes


def _ssd_fwd(name, xbc, dtr, dt_bias, a_log, d_skip):
    s_len = xbc.shape[0]
    chunk = SSD_CHUNK
    nc = s_len // chunk
    n_pairs = SSD_HEADS // 2

    def body(xs_ref, b_ref, c_ref, dtr_ref, bias_ref, alog_ref, dsk_ref, y_ref, st_ref, state_ref, acs_t_ref):
        @pl.when(pl.program_id(0) == 0)
        def _():
            state_ref[...] = jnp.zeros_like(state_ref)

        _, dt, _, tri, a_cs, a_last = _ssd_common(dtr_ref, bias_ref, alog_ref, acs_t_ref)
        lo_half = _iota((chunk, LANES), 1) < HEAD_DIM
        lo_lane = _iota((1, LANES), 1) < HEAD_DIM
        lo_row = _iota((LANES, 1), 0) < HEAD_DIM
        dsk = dsk_ref[...]
        for g in range(2):
            gsl = slice(g * LANES, (g + 1) * LANES)
            bg, cg = b_ref[:, gsl], c_ref[:, gsl]
            gmat = _dot_nt(cg, bg)
            for pr in range(n_pairs // 2):
                pair = g * (n_pairs // 2) + pr
                h0, h1 = 2 * pair, 2 * pair + 1
                psl = slice(pair * LANES, (pair + 1) * LANES)
                x2 = xs_ref[:, psl]
                acs0, acs1 = _col(a_cs, h0), _col(a_cs, h1)
                xdt = x2 * _pair_cols(_col(dt, h0), _col(dt, h1), lo_half)
                y2 = jnp.zeros((chunk, LANES), F32)
                for h, acs_c, hmask in ((h0, acs0, lo_half), (h1, acs1, ~lo_half)):
                    lam = jnp.where(tri, jnp.exp(jnp.minimum(acs_c - acs_t_ref[h:h + 1, :], 0.0)), 0.0)
                    y2 = y2 + _dot(gmat * lam, jnp.where(hmask, xdt, 0.0))
                s2 = state_ref[pair]
                st_ref[0, pair] = s2
                y2 = y2 + _pair_cols(jnp.exp(acs0), jnp.exp(acs1), lo_half) * _dot_nt(cg, s2)
                y_ref[:, psl] = y2 + _pair_cols(_col(dsk, h0), _col(dsk, h1), lo_lane) * x2
                al0, al1 = _col(a_last, h0), _col(a_last, h1)
                wl2 = _pair_cols(jnp.exp(al0 - acs0), jnp.exp(al1 - acs1), lo_half)
                state_ref[pair] = _pair_cols(jnp.exp(al0), jnp.exp(al1), lo_row) * s2 + _dot((xdt * wl2).T, bg)

    xs, bm, cm, lanes = _ssd_specs(chunk, lambda c: c)
    vec = _const_spec((1, LANES))
    return pl.pallas_call(
        body, name=name, grid=(nc,),
        in_specs=[xs, bm, cm, lanes, vec, vec, vec],
        out_specs=[xs, pl.BlockSpec((1, n_pairs, LANES, LANES), lambda c: (c, 0, 0, 0))],
        out_shape=[jax.ShapeDtypeStruct((s_len, 1024), F32), jax.ShapeDtypeStruct((nc, n_pairs, LANES, LANES), F32)],
        scratch_shapes=[pltpu.VMEM((n_pairs, LANES, LANES), F32), pltpu.VMEM((LANES, chunk), F32)],
        compiler_params=_cparams(("arbitrary",)),
    )(xbc, xbc, xbc, dtr, dt_bias, a_log, d_skip)


def _ssd_bwd(name, xbc, dtr, states, dy, dt_bias, a_log, d_skip):
    s_len = xbc.shape[0]
    chunk = SSD_CHUNK
    nc = s_len // chunk
    n_pairs = SSD_HEADS // 2
    rev = lambda c: nc - 1 - c

    def body(xs_ref, b_ref, c_ref, dtr_ref, dy_ref, sin_ref, bias_ref, alog_ref, dsk_ref,
             dxs_ref, db_ref, dc_ref, ddtr_ref, dbias_ref, dalog_ref, ddsk_ref, dstate_ref, acs_t_ref):
        @pl.when(pl.program_id(0) == 0)
        def _():
            dstate_ref[...] = jnp.zeros_like(dstate_ref)
            dbias_ref[...] = jnp.zeros_like(dbias_ref)
            dalog_ref[...] = jnp.zeros_like(dalog_ref)
            ddsk_ref[...] = jnp.zeros_like(ddsk_ref)

        xpre, dt, a_neg, tri, a_cs, a_last = _ssd_common(dtr_ref, bias_ref, alog_ref, acs_t_ref)
        lane = _iota((chunk, LANES), 1)
        lo_half = lane < HEAD_DIM
        lane1 = _iota((1, LANES), 1)
        lo_lane = lane1 < HEAD_DIM
        lo_row = _iota((LANES, 1), 0) < HEAD_DIM
        head_row = _iota((LANES, chunk), 0)
        sq_row, sq_col = _iota((chunk, chunk), 0), _iota((chunk, chunk), 1)
        before = (sq_row < sq_col).astype(BF16)
        dsk = dsk_ref[...]
        da_rows = jnp.zeros((LANES, chunk), F32)
        yo = jnp.zeros((chunk, LANES), F32)
        to = jnp.zeros((chunk, LANES), F32)
        vs = jnp.zeros((1, LANES), F32)
        ddt = jnp.zeros((chunk, LANES), F32)
        ddsk = jnp.zeros((1, LANES), F32)

        def half_sums(v):
            lo = jnp.sum(jnp.where(lo_half, v, 0.0), axis=1, keepdims=True)
            return lo, jnp.sum(v, axis=1, keepdims=True) - lo

        for g in range(2):
            gsl = slice(g * LANES, (g + 1) * LANES)
            bg, cg = b_ref[:, gsl], c_ref[:, gsl]
            gmat = _dot_nt(cg, bg)
            dgm = jnp.zeros((chunk, chunk), F32)
            dbg = jnp.zeros((chunk, LANES), F32)
            dcg = jnp.zeros((chunk, LANES), F32)
            for pr in range(n_pairs // 2):
                pair = g * (n_pairs // 2) + pr
                h0, h1 = 2 * pair, 2 * pair + 1
                psl = slice(pair * LANES, (pair + 1) * LANES)
                x2, dy2 = xs_ref[:, psl], dy_ref[:, psl]
                acs0, acs1 = _col(a_cs, h0), _col(a_cs, h1)
                dt2 = _pair_cols(_col(dt, h0), _col(dt, h1), lo_half)
                xdt = x2 * dt2
                al0, al1 = _col(a_last, h0), _col(a_last, h1)
                v2 = _pair_cols(jnp.exp(acs0), jnp.exp(acs1), lo_half)
                wl2 = _pair_cols(jnp.exp(al0 - acs0), jnp.exp(al1 - acs1), lo_half)
                s_in, ds2 = sin_ref[0, pair], dstate_ref[pair]
                y_off = v2 * _dot_nt(cg, s_in)
                dx_state = wl2 * _dot_nt(bg, ds2)
                dx2 = dx_state
                for h, acs_c, hmask in ((h0, acs0, lo_half), (h1, acs1, ~lo_half)):
                    lam = jnp.where(tri, jnp.exp(jnp.minimum(acs_c - acs_t_ref[h:h + 1, :], 0.0)), 0.0)
                    m = gmat * lam
                    dyh = jnp.where(hmask, dy2, 0.0)
                    dx2 = dx2 + _dot(m.T, dyh)
                    dml = _dot_nt(dyh, xdt) * lam
                    dgm = dgm + dml
                    crossed = jnp.where(sq_row >= sq_col, _dot3(dml * gmat, before), 0.0)
                    da_rows = jnp.where(head_row == h, jnp.sum(crossed, axis=0, keepdims=True), da_rows)
                vdy = v2 * dy2
                dcg = dcg + _dot(vdy, s_in)
                dbg = dbg + _dot(wl2 * xdt, ds2)
                yo0, yo1 = half_sums(dy2 * y_off)
                yo = jnp.where(lane == h0, yo0, jnp.where(lane == h1, yo1, yo))
                to0, to1 = half_sums(dx_state * xdt)
                to = jnp.where(lane == h0, to0, jnp.where(lane == h1, to1, to))
                prod = jnp.sum(ds2 * s_in, axis=1, keepdims=True)
                e0 = jnp.sum(jnp.where(lo_row, prod, 0.0), axis=0, keepdims=True)
                e1 = jnp.sum(prod, axis=0, keepdims=True) - e0
                vs = jnp.where(lane1 == h0, jnp.exp(al0) * e0, jnp.where(lane1 == h1, jnp.exp(al1) * e1, vs))
                q0, q1 = half_sums(dx2 * x2)
                ddt = jnp.where(lane == h0, q0, jnp.where(lane == h1, q1, ddt))
                dxs_ref[:, psl] = dx2 * dt2 + _pair_cols(_col(dsk, h0), _col(dsk, h1), lo_lane) * dy2
                s0, s1 = half_sums(dy2 * x2)
                ddsk = jnp.where(lane1 == h0, jnp.sum(s0, axis=0, keepdims=True),
                                 jnp.where(lane1 == h1, jnp.sum(s1, axis=0, keepdims=True), ddsk))
                dstate_ref[pair] = _pair_cols(jnp.exp(al0), jnp.exp(al1), lo_row) * ds2 + _dot(vdy.T, cg)
            dc_ref[:, gsl] = dcg + _dot(dgm, bg)
            db_ref[:, gsl] = dbg + _dot(dgm.T, cg)

        da = (da_rows.T + _dot3_left((sq_col >= sq_row).astype(BF16), yo)
              + _dot3_left((sq_col < sq_row).astype(BF16), to) + vs)
        ddt = ddt + da * a_neg
        dalog_ref[...] += jnp.sum(da * dt, axis=0, keepdims=True) * a_neg
        ddtr = jnp.where(lane < SSD_HEADS, ddt * jax.nn.sigmoid(xpre), 0.0)
        ddtr_ref[...] = ddtr
        dbias_ref[...] += jnp.sum(ddtr, axis=0, keepdims=True)
        ddsk_ref[...] += ddsk

    xs, bm, cm, lanes = _ssd_specs(chunk, rev)
    vec = _const_spec((1, LANES))
    st_in = pl.BlockSpec((1, n_pairs, LANES, LANES), lambda c: (rev(c), 0, 0, 0))
    bc_out = pl.BlockSpec((chunk, 256), lambda c: (rev(c), 0))
    return pl.pallas_call(
        body, name=name, grid=(nc,),
        in_specs=[xs, bm, cm, lanes, xs, st_in, vec, vec, vec],
        out_specs=[xs, bc_out, bc_out, lanes, vec, vec, vec],
        out_shape=[jax.ShapeDtypeStruct((s_len, 1024), F32), jax.ShapeDtypeStruct((s_len, 256), F32),
                   jax.ShapeDtypeStruct((s_len, 256), F32), jax.ShapeDtypeStruct((s_len, LANES), F32),
                   jax.ShapeDtypeStruct((1, LANES), F32), jax.ShapeDtypeStruct((1, LANES), F32),
                   jax.ShapeDtypeStruct((1, LANES), F32)],
        scratch_shapes=[pltpu.VMEM((n_pairs, LANES, LANES), F32), pltpu.VMEM((LANES, chunk), F32)],
        compiler_params=_cparams(("arbitrary",)),
    )(xbc, xbc, xbc, dtr, dy, states, dt_bias, a_log, d_skip)


def _gatenorm_fwd(name, y, proj, g):
    s_len = y.shape[0]
    ts = _tile(s_len, 512, 8)
    gw = 512

    def body(y_ref, z_ref, g_ref, o_ref):
        yg = y_ref[...] * _silu(z_ref[...])
        rstd = lax.rsqrt(jnp.mean(yg * yg, axis=-1, keepdims=True) + RMS_EPS)
        o_ref[...] = (yg * rstd * g_ref[...]).astype(o_ref.dtype)

    return pl.pallas_call(
        body, name=name, grid=(2, s_len // ts),
        in_specs=[pl.BlockSpec((ts, gw), lambda gi, i: (i, gi)), pl.BlockSpec((ts, gw), lambda gi, i: (i, C_Z // gw + gi)),
                  pl.BlockSpec((1, gw), lambda gi, i: (0, gi))],
        out_specs=pl.BlockSpec((ts, gw), lambda gi, i: (i, gi)),
        out_shape=jax.ShapeDtypeStruct((s_len, 2 * gw), BF16),
        compiler_params=_cparams(("parallel", "parallel")),
    )(y, proj, g)


def _gatenorm_bwd(name, y, proj, g, dmixed):
    s_len = y.shape[0]
    ts = _tile(s_len, 512, 8)
    gw = 512

    def body(y_ref, z_ref, g_ref, d_ref, dy_ref, dz_ref, dg_ref):
        yv, z = y_ref[...], z_ref[...]
        sz = _silu(z)
        yg = yv * sz
        rstd = lax.rsqrt(jnp.mean(yg * yg, axis=-1, keepdims=True) + RMS_EPS)
        n = yg * rstd
        dn = d_ref[...] * g_ref[...]
        dyg = rstd * (dn - n * jnp.mean(dn * n, axis=-1, keepdims=True))
        dy_ref[...] = dyg * sz
        dz_ref[...] = (dyg * yv * _dsilu(z)).astype(dz_ref.dtype)

        @pl.when(pl.program_id(1) == 0)
        def _():
            dg_ref[...] = jnp.zeros_like(dg_ref)

        dg_ref[...] += jnp.sum(d_ref[...] * n, axis=0, keepdims=True)

    grp = pl.BlockSpec((ts, gw), lambda gi, i: (i, gi))
    vec = pl.BlockSpec((1, gw), lambda gi, i: (0, gi))
    return pl.pallas_call(
        body, name=name, grid=(2, s_len // ts),
        in_specs=[grp, pl.BlockSpec((ts, gw), lambda gi, i: (i, C_Z // gw + gi)), vec,
                  pl.BlockSpec((ts, gw), lambda gi, i: (i, 1 + gi))],
        out_specs=[grp, grp, vec],
        out_shape=[jax.ShapeDtypeStruct((s_len, 2 * gw), F32), jax.ShapeDtypeStruct((s_len, 2 * gw), BF16),
                   jax.ShapeDtypeStruct((1, 2 * gw), F32)],
        compiler_params=_cparams(("parallel", "arbitrary")),
    )(y, proj, g, dmixed)


def _attn_scores(qh, kblk, mask, ustrict, r):
    z = _dot_nt(qh, kblk)
    sp = jnp.log1p(jnp.exp(-jnp.abs(z)))
    ls = jnp.minimum(z, 0.0) - sp
    lm_raw = jnp.minimum(-z, 0.0) - sp
    lm = jnp.where(mask, lm_raw, 0.0)
    suffix = _dot3(lm, ustrict)
    w = jnp.where(mask, jnp.exp(ls + suffix + r), 0.0)
    return ls, lm_raw, lm, w


def _attn_tiles(s_len):
    tk = ATTN_BLOCK
    return next(m * tk for m in (2, 1) if s_len % (m * tk) == 0), tk


def _attn_kv(proj):
    return proj[:, C_K:C_END].astype(BF16)


def _attn_specs(tq, s_len):
    qcol, vcol = C_Q // LANES, (C_V - C_K) // LANES
    q = pl.BlockSpec((tq, LANES), lambda p, i: (i, qcol + p))
    k = pl.BlockSpec((s_len, LANES), lambda p, i: (0, p))
    v = pl.BlockSpec((s_len, LANES), lambda p, i: (0, vcol + p))
    return q, k, v


def _attn_fwd(name, proj, kv):
    s_len = proj.shape[0]
    tq, tk = _attn_tiles(s_len)
    n_slabs = 4

    def body(q_ref, k_ref, v_ref, o_ref):
        qi = pl.program_id(1)
        q2 = q_ref[...] * (HEAD_DIM ** -0.5)
        lo = _iota((tq, LANES), 1) < HEAD_DIM
        lo_k = _iota((tk, LANES), 1) < HEAD_DIM
        heads = ((jnp.where(lo, q2, 0.0).astype(BF16), lo_k), (jnp.where(lo, 0.0, q2).astype(BF16), ~lo_k))
        ustrict = (_iota((tk, tk), 0) > _iota((tk, tk), 1)).astype(BF16)
        q_pos = qi * tq + _iota((tq, tk), 0)
        k_off = _iota((tq, tk), 1)

        def step(carry):
            kb, _, r0, r1, acc = carry
            rows = pl.ds(pl.multiple_of(kb * tk, tk), tk)
            kblk, vblk = k_ref[rows, :], v_ref[rows, :]
            mask = kb * tk + k_off < q_pos
            new_r = []
            for (qh, hmask), r in zip(heads, (r0, r1)):
                _, _, lm, w = _attn_scores(qh, kblk, mask, ustrict, r)
                acc = acc + _dot(w, jnp.where(hmask, vblk, 0.0))
                new_r.append(r + jnp.sum(lm, axis=1, keepdims=True))
            go = (jnp.maximum(jnp.max(new_r[0]), jnp.max(new_r[1])) > EXP_UNDERFLOW).astype(jnp.int32)
            return kb - 1, go, new_r[0], new_r[1], acc

        zero = jnp.zeros((tq, 1), F32)
        init = ((qi + 1) * (tq // tk) - 1, jnp.int32(1), zero, zero, jnp.zeros((tq, LANES), F32))
        o_ref[...] = lax.while_loop(lambda c: (c[0] >= 0) & (c[1] > 0), step, init)[4]

    q, k, v = _attn_specs(tq, s_len)
    return pl.pallas_call(
        body, name=name, grid=(n_slabs, s_len // tq),
        in_specs=[q, k, v], out_specs=pl.BlockSpec((tq, LANES), lambda p, i: (i, p)),
        out_shape=jax.ShapeDtypeStruct((s_len, n_slabs * LANES), F32),
        compiler_params=_cparams(("parallel", "arbitrary")),
    )(proj, kv, kv)


def _attn_bwd(name, proj, kv, dmixed):
    s_len = proj.shape[0]
    tq, tk = _attn_tiles(s_len)
    n_slabs = 4
    scale = HEAD_DIM ** -0.5

    def body(q_ref, k_ref, v_ref, do_ref, dq_ref, dk_ref, dv_ref, dk_acc, dv_acc, r_hist):
        qi = pl.program_id(1)

        @pl.when(qi == 0)
        def _():
            dk_acc[...] = jnp.zeros_like(dk_acc)
            dv_acc[...] = jnp.zeros_like(dv_acc)

        q2 = q_ref[...] * scale
        do2 = do_ref[...]
        lo = _iota((tq, LANES), 1) < HEAD_DIM
        lo_k = _iota((tk, LANES), 1) < HEAD_DIM
        heads = ((jnp.where(lo, q2, 0.0).astype(BF16), jnp.where(lo, do2, 0.0).astype(BF16), lo_k),
                 (jnp.where(lo, 0.0, q2).astype(BF16), jnp.where(lo, 0.0, do2).astype(BF16), ~lo_k))
        row, col = _iota((tk, tk), 0), _iota((tk, tk), 1)
        ustrict = (row > col).astype(BF16)
        earlier = (row < col).astype(BF16)
        q_pos = qi * tq + _iota((tq, tk), 0)
        k_off = _iota((tq, tk), 1)
        top = (qi + 1) * (tq // tk) - 1
        zero = jnp.zeros((tq, 1), F32)

        def scan(carry):
            kb, _, r0, r1 = carry
            kblk = k_ref[pl.ds(pl.multiple_of(kb * tk, tk), tk), :]
            mask = kb * tk + k_off < q_pos
            r_hist[kb] = jnp.where(lo, r0, r1)
            new_r = []
            for (qh, _, _), r in zip(heads, (r0, r1)):
                z = _dot_nt(qh, kblk)
                lm = jnp.where(mask, jnp.minimum(-z, 0.0) - jnp.log1p(jnp.exp(-jnp.abs(z))), 0.0)
                new_r.append(r + jnp.sum(lm, axis=1, keepdims=True))
            go = (jnp.maximum(jnp.max(new_r[0]), jnp.max(new_r[1])) > EXP_UNDERFLOW).astype(jnp.int32)
            return kb - 1, go, new_r[0], new_r[1]

        first = lax.while_loop(lambda c: (c[0] >= 0) & (c[1] > 0), scan, (top, jnp.int32(1), zero, zero))[0] + 1

        def step(carry):
            kb, p0, p1, dq = carry
            rows = pl.ds(pl.multiple_of(kb * tk, tk), tk)
            kblk, vblk = k_ref[rows, :], v_ref[rows, :]
            mask = kb * tk + k_off < q_pos
            rr = r_hist[kb]
            dk_blk = jnp.zeros((tk, LANES), F32)
            dv_blk = jnp.zeros((tk, LANES), F32)
            new_p = []
            for (qh, doh, hmask), r, p in zip(heads, (_col(rr, 0), _col(rr, HEAD_DIM)), (p0, p1)):
                ls, lm_raw, _, w = _attn_scores(qh, kblk, mask, ustrict, r)
                ew = _dot_nt(doh, vblk) * w
                before = p + _dot3(ew, earlier)
                dz = jnp.where(mask, ew * jnp.exp(lm_raw) - jnp.exp(ls) * before, 0.0)
                dq = dq + _dot(dz, jnp.where(hmask, kblk, 0.0))
                dk_blk = dk_blk + _dot(dz.T, qh)
                dv_blk = dv_blk + _dot(w.T, doh)
                new_p.append(p + jnp.sum(ew, axis=1, keepdims=True))
            dk_acc[rows, :] += dk_blk
            dv_acc[rows, :] += dv_blk
            return kb + 1, new_p[0], new_p[1], dq

        dq = lax.while_loop(lambda c: c[0] <= top, step, (first, zero, zero, jnp.zeros((tq, LANES), F32)))[3]
        dq_ref[...] = (dq * scale).astype(dq_ref.dtype)

        @pl.when(qi == pl.num_programs(1) - 1)
        def _():
            dk_ref[...] = dk_acc[...].astype(dk_ref.dtype)
            dv_ref[...] = dv_acc[...].astype(dv_ref.dtype)

    q, k, v = _attn_specs(tq, s_len)
    blk = pl.BlockSpec((tq, LANES), lambda p, i: (i, p))
    full = pl.BlockSpec((s_len, LANES), lambda p, i: (0, p))
    shape = jax.ShapeDtypeStruct((s_len, n_slabs * LANES), BF16)
    return pl.pallas_call(
        body, name=name, grid=(n_slabs, s_len // tq),
        in_specs=[q, k, v, pl.BlockSpec((tq, LANES), lambda p, i: (i, 1536 // LANES + p))],
        out_specs=[blk, full, full], out_shape=[shape, shape, shape],
        scratch_shapes=[pltpu.VMEM((s_len, LANES), F32), pltpu.VMEM((s_len, LANES), F32),
                        pltpu.VMEM((s_len // tk, tq, LANES), F32)],
        compiler_params=_cparams(("parallel", "arbitrary")),
    )(proj, kv, kv, dmixed)


def _ident(accs, _):
    return accs


def _mixer_fwd(tag, h, p, hooks=_no_hooks):
    u = _rms_fwd(tag + "_norm", h, p["mix_norm"])
    (proj,) = _mm(tag + "_in", [u], [p["w_main"]], nt=False, epilogue=_ident, out_dtypes=[F32], hosted=hooks("in"),
                  tk=2048)
    (dtr,) = _mm(tag + "_indt", [u], [p["w_dt"]], nt=False, epilogue=_ident, out_dtypes=[F32], tk=2048)
    pool_out = _pool_fwd(tag + "_pool", proj, p["pool_w"], p["pool_scale"])
    xbc = _conv_fwd(tag + "_conv", proj, p["conv_w"], p["conv_b"])
    y, states = _ssd_fwd(tag + "_ssd", xbc, dtr, p["dt_bias"], p["a_log"], p["d_skip"])
    ssd_out = _gatenorm_fwd(tag + "_gate", y, proj, p["ssd_norm"])
    kv = _attn_kv(proj)
    attn = _attn_fwd(tag + "_attn", proj, kv)
    mixed = jnp.concatenate([pool_out, ssd_out, attn.astype(BF16)], axis=1)
    (h2,) = _mm(tag + "_out", [mixed], [p["w_out"]], nt=False, extras=[h], epilogue=lambda accs, ex: [ex[0] + accs[0]],
                out_dtypes=[F32], hosted=hooks("out"), tk=2048)
    return h2, (h, u, proj, dtr, xbc, y, states, mixed, kv)


def _mixer_bwd(tag, saved, p, dh2, dh2_bf, hooks=_no_hooks, group=None):
    h, u, proj, dtr, xbc, y, states, mixed, kv = saved
    (dmixed,) = _mm(tag + "_dmix", [dh2_bf], [p["w_out"]], nt=True, epilogue=_ident, out_dtypes=[F32],
                    hosted=hooks("dmix"), tk=2048)
    (dw_out,) = _mm(tag + "_dwout", [mixed], [dh2_bf], nt=False, ta=True, epilogue=_ident, out_dtypes=[F32],
                    hosted=hooks("dwout"), tm=2048, tn=1024, tk=1024)
    dpool_in, dpool_w, dpool_scale = _pool_bwd(tag + "_dpool", proj, dmixed, p["pool_w"], p["pool_scale"])
    dy, dz, dssd_norm = _gatenorm_bwd(tag + "_dgate", y, proj, p["ssd_norm"], dmixed)
    dxs, dbm, dcm, ddtr, ddt_bias, da_log, dd_skip = _ssd_bwd(tag + "_dssd", xbc, dtr, states, dy, p["dt_bias"],
                                                             p["a_log"], p["d_skip"])
    dxbc, dconv_w, dconv_b = _conv_bwd(tag + "_dconv", proj, jnp.concatenate([dxs, dbm, dcm], axis=1), p["conv_w"],
                                       p["conv_b"])
    dq, dk, dv = _attn_bwd(tag + "_dattn", proj, kv, dmixed)
    dproj = jnp.concatenate([dpool_in, dz, dxbc, dq, dk, dv], axis=1)
    ddtr_bf = ddtr.astype(BF16)
    (dw_main,) = _mm(tag + "_dwin", [u], [dproj], nt=False, ta=True, epilogue=_ident, out_dtypes=[F32],
                     hosted=hooks("dwin"), tm=2048, tn=512, tk=1024)
    (dw_dt,) = _mm(tag + "_dwdt", [u], [ddtr_bf], nt=False, ta=True, epilogue=_ident, out_dtypes=[F32])
    dw_in = jnp.concatenate([dw_main[:, :REF_DT], dw_dt[:, :SSD_HEADS], dw_main[:, REF_DT:]], axis=1)
    dw_in = dw_in.reshape(dw_in.shape[0], N_CHIPS, dw_in.shape[1] // N_CHIPS).transpose(1, 0, 2)
    mine = group(dw_in, dw_out) if group else (dw_in, dw_out)
    (du_dt,) = _mm(tag + "_dudt", [ddtr_bf], [p["w_dt"]], nt=True, epilogue=_ident, out_dtypes=[F32], tn=1024)
    (du,) = _mm(tag + "_du", [dproj], [p["w_main"]], nt=True, extras=[du_dt], epilogue=lambda accs, ex: [accs[0] + ex[0]],
                out_dtypes=[F32], hosted=hooks("du") + ([mine.swap()] if group else []), tm=1024, tn=1024, tk=1536)
    dh, dh_bf, dg = _rms_bwd(tag + "_dnorm", h, du, dh2, p["mix_norm"])
    grads = dict(mix_norm=dg, pool_w=dpool_w, pool_scale=dpool_scale, conv_w=dconv_w, conv_b=dconv_b, dt_bias=ddt_bias,
                 a_log=da_log, d_skip=dd_skip, ssd_norm=dssd_norm)
    return dh, dh_bf, grads, mine


def _axes():
    return lax.axis_index("x"), lax.axis_index("y"), lax.axis_index("c")


def _any_specs(n):
    return [pl.BlockSpec(memory_space=pl.ANY) for _ in range(n)]


def _remote(src, dst, send, recv, k, dev):
    return pltpu.make_async_remote_copy(src_ref=src, dst_ref=dst, send_sem=send.at[k], recv_sem=recv.at[k],
                                        device_id=dev, device_id_type=MESH_ID)


def _chip_peers(x, y):
    return [(1 - x, y), (x, 1 - y), (1 - x, 1 - y)]


def _gather_comm(shards, done):
    n = len(shards)

    def make(in_refs, out_refs, send, recv, loc, r0, l0):
        x, y, c = _axes()
        me = 2 * x + y
        local = [pltpu.make_async_copy(in_refs[a], out_refs[a].at[me], loc.at[l0 + a]) for a in range(n)]
        sent = [_remote(in_refs[a], out_refs[a].at[me], send, recv, r0 + 3 * a + k, (px, py, c))
                for a in range(n) for k, (px, py) in enumerate(_chip_peers(x, y))]
        got = [_remote(in_refs[a], out_refs[a].at[2 * px + py], send, recv, r0 + 3 * a + k, (px, py, c))
               for a in range(n) for k, (px, py) in enumerate(_chip_peers(x, y))]
        return local + sent, [g.wait_recv for g in got] + [s.wait_send for s in sent] + [l.wait for l in local]

    return _Comm(shards, [jax.ShapeDtypeStruct((N_CHIPS,) + s.shape, s.dtype) for s in shards], 3 * n, n, make, done)


def _swap_comm(arrs, kinds, outs, done):
    n = len(arrs)

    def make(in_refs, out_refs, send, recv, loc, r0, l0):
        x, y, c = _axes()
        cps = [_remote(in_refs[a] if kinds[a] is None else _half(kinds[a], in_refs[a], 1 - c), out_refs[a], send, recv,
                       r0 + a, (x, y, 1 - c)) for a in range(n)]
        return cps, [cp.wait for cp in cps]

    return _Comm(arrs, outs, n, 0, make, done)


def _chips_comm(wires, kinds, done):
    n = len(wires)

    def make(in_refs, out_refs, send, recv, loc, r0, l0):
        x, y, c = _axes()
        cps = [_remote(_wire_shard(kinds[a], in_refs[a], 2 * px + py), out_refs[a].at[k], send, recv, r0 + 3 * a + k,
                       (px, py, c)) for a in range(n) for k, (px, py) in enumerate(_chip_peers(x, y))]
        return cps, [cp.wait for cp in cps]

    outs = [jax.ShapeDtypeStruct((3,) + _wire_shard_shape(k, w.shape), w.dtype) for k, w in zip(kinds, wires)]
    return _Comm(wires, outs, 3 * n, 0, make, done)


def _run_comm(name, hosted):
    ins, outs, sems, build, deliver = _comm_plan(hosted)

    def body(*refs):
        starts, waits = build(refs[:len(ins)], refs[len(ins):len(ins) + len(outs)], *refs[len(ins) + len(outs):])
        for cp in starts:
            cp.start()
        for wait in waits:
            wait()

    deliver(pl.pallas_call(body, name=name, in_specs=_any_specs(len(ins)), out_specs=_any_specs(len(outs)),
                           out_shape=outs, scratch_shapes=sems)(*ins))


def _half(kind, ref, hc):
    if kind == "col":
        r = ref.shape[0] // 2
        return ref.at[pl.ds(pl.multiple_of(hc * r, 16), r), :]
    if kind == "row":
        w = ref.shape[1] // 2
        return ref.at[:, pl.ds(pl.multiple_of(hc * w, LANES), w)]
    r = ref.shape[1] // 2
    return ref.at[:, pl.ds(pl.multiple_of(hc * r, 16), r), :]


def _half_shape(kind, s):
    return {"col": (s[0] // 2, s[1]), "row": (s[0], s[1] // 2), "win": (s[0], s[1] // 2) + tuple(s[2:])}[kind]


def _wire_shape(kind, hs):
    return (N_CHIPS, hs[0], hs[1] // N_CHIPS) if kind == "col" else tuple(hs)


def _wire_shard(kind, ref, j):
    if kind == "row":
        r = ref.shape[0] // N_CHIPS
        return ref.at[pl.ds(pl.multiple_of(j * r, 16), r), :]
    return ref.at[j]


def _wire_shard_shape(kind, ws):
    return (ws[0] // N_CHIPS, ws[1]) if kind == "row" else tuple(ws[1:])


class _GradGroup:
    def __init__(self, tag, names, kinds, arrs, where, south):
        self.tag, self.names, self.kinds, self.arrs, self.where, self.south = tag, list(names), list(kinds), list(arrs), where, south
        self.h32, self.wire, self.final = None, None, None
        self.from_chips = [None] * len(arrs)

    def swap(self):
        outs = [jax.ShapeDtypeStruct(_half_shape(k, g.shape), g.dtype) for k, g in zip(self.kinds, self.arrs)]

        def done(from_sibling):
            sums = [_sum_pair(f"{self.tag}_{n}_pair", k, g, r, self.where)
                    for n, k, g, r in zip(self.names, self.kinds, self.arrs, from_sibling)]
            self.h32, self.wire = [s[0] for s in sums], [s[1] for s in sums]

        return _swap_comm(self.arrs, self.kinds, outs, done)

    def chips(self, which):
        def done(got):
            for a, r in zip(which, got):
                self.from_chips[a] = r

        return _chips_comm([self.wire[a] for a in which], [self.kinds[a] for a in which], done)

    def share(self):
        parts = [_sum_chips(f"{self.tag}_{n}_chips", k, h, r, self.where)
                 for n, k, h, r in zip(self.names, self.kinds, self.h32, self.from_chips)]

        def done(theirs):
            axis = lambda k: 1 if k == "row" else 0
            self.final = [jnp.concatenate([jnp.where(self.south, mine, other), jnp.where(self.south, other, mine)], axis=axis(k))
                          for k, mine, other in zip(self.kinds, parts, theirs)]

        return _swap_comm(parts, [None] * len(parts), [jax.ShapeDtypeStruct(p.shape, p.dtype) for p in parts], done)

    def grads(self):
        return dict(zip(self.names, self.final))


def _esum(name, grid, block, ins, outs, where):
    n_in = len(ins)

    def body(s_ref, *refs):
        tot = refs[0][...].astype(F32)
        for r in refs[1:n_in]:
            tot = tot + r[...].astype(F32)
        for o in refs[n_in:]:
            o[...] = tot.astype(o.dtype)

    spec = lambda nd, imap: pl.BlockSpec((None,) * (nd - 2) + tuple(block), imap)
    return pl.pallas_call(
        body, name=name,
        grid_spec=pltpu.PrefetchScalarGridSpec(
            num_scalar_prefetch=1, grid=grid,
            in_specs=[spec(a.ndim, m) for a, m in ins], out_specs=[spec(len(s), m) for s, _, m in outs]),
        out_shape=[jax.ShapeDtypeStruct(s, dt) for s, dt, _ in outs],
        compiler_params=_cparams(("parallel",) * len(grid)),
    )(where, *[a for a, _ in ins])


def _sum_pair(name, kind, g, r1, where):
    hshape = _half_shape(kind, g.shape)
    wshape = _wire_shape(kind, hshape)
    if kind == "col":
        block = (_tile(hshape[0], 512, 16), hshape[1] // N_CHIPS)
        nb = hshape[0] // block[0]
        grid = (nb, N_CHIPS)
        gmap = lambda i, j, s: (s[0] * nb + i, j)
        hmap = lambda i, j, s: (i, j)
        wmap = lambda i, j, s: (j, i, 0)
    elif kind == "row":
        block = (_tile(hshape[0], 512, 16), hshape[1])
        grid = (hshape[0] // block[0],)
        gmap = lambda i, s: (i, s[0])
        hmap = wmap = lambda i, s: (i, 0)
    else:
        block = (_tile(hshape[1], 512, 16), hshape[2])
        nb = hshape[1] // block[0]
        grid = (hshape[0], nb)
        gmap = lambda q, i, s: (q, s[0] * nb + i, 0)
        hmap = wmap = lambda q, i, s: (q, i, 0)
    return _esum(name, grid, block, [(g, gmap), (r1, hmap)], [(hshape, F32, hmap), (wshape, BF16, wmap)], where)


def _sum_chips(name, kind, h32, r2, where):
    tshape = tuple(r2.shape[1:])
    block = (_tile(tshape[0], 512, 16), tshape[1])
    nb = tshape[0] // block[0]
    if kind == "col":
        hmap = lambda i, s: (i, s[1])
    elif kind == "row":
        hmap = lambda i, s: (s[1] * nb + i, 0)
    else:
        hmap = lambda i, s: (s[1], i, 0)
    rmap = lambda k: (lambda i, s: (k, i, 0))
    return _esum(name, (nb,), block, [(h32, hmap)] + [(r2, rmap(k)) for k in range(3)],
                 [(tshape, F32, lambda i, s: (i, 0))], where)[0]


def _allreduce_small(name, vec):
    rows_n = vec.shape[0]

    def body(x_ref, sum_ref, all_ref, send, recv, local_sem):
        x, y, c = _axes()
        me, sibling = (x, y, c), (x, y, 1 - c)
        chips = [(1 - x, y), (x, 1 - y), (1 - x, 1 - y)]

        def rows(px, py, pc):
            return all_ref.at[pl.ds(pl.multiple_of((4 * px + 2 * py + pc) * rows_n, 8), rows_n), :]

        def copy(k, block, to, src=None):
            return _remote(rows(*block) if src is None else src, rows(*block), send, recv, k, to)

        mine = pltpu.make_async_copy(x_ref, rows(*me), local_sem)
        mine.start()
        first = [copy(0, me, sibling, src=x_ref)] + [copy(1 + j, me, (*chip, c), src=x_ref) for j, chip in enumerate(chips)]
        for cp in first:
            cp.start()
        passed = [copy(4 + j, (*chip, c), sibling) for j, chip in enumerate(chips)]
        for j, chip in enumerate(chips):
            copy(1 + j, (*chip, c), me).wait_recv()
            passed[j].start()
        copy(0, sibling, me).wait_recv()
        for j, chip in enumerate(chips):
            copy(4 + j, (*chip, 1 - c), me).wait_recv()
        for cp in first + passed:
            cp.wait_send()
        mine.wait()
        tot = all_ref[0:rows_n, :]
        for d in range(1, 8):
            tot = tot + all_ref[d * rows_n:(d + 1) * rows_n, :]
        sum_ref[...] = tot

    vm = pl.BlockSpec(memory_space=pltpu.VMEM)
    return pl.pallas_call(
        body, name=name, in_specs=[vm], out_specs=[vm, vm],
        out_shape=[jax.ShapeDtypeStruct(vec.shape, F32), jax.ShapeDtypeStruct((8 * rows_n, LANES), F32)],
        scratch_shapes=[pltpu.SemaphoreType.DMA((7,)), pltpu.SemaphoreType.DMA((7,)), pltpu.SemaphoreType.DMA],
        compiler_params=pltpu.CompilerParams(vmem_limit_bytes=VMEM_LIMIT),
    )(vec)[0]


def _adamw(name, w, g, m, v):
    shape = w.shape
    rows_n, cols = shape[-2], shape[-1]
    lead = math.prod(shape[:-2])
    tr = _tile(rows_n, 256, 8)

    def body(w_ref, g_ref, m_ref, v_ref, d_ref, m2_ref, v2_ref):
        gv = g_ref[...]
        m2 = ADAM_B1 * m_ref[...] + (1.0 - ADAM_B1) * gv
        v2 = ADAM_B2 * v_ref[...] + (1.0 - ADAM_B2) * jnp.square(gv)
        m_hat = m2 / (1.0 - ADAM_B1 ** ADAM_STEP)
        v_hat = v2 / (1.0 - ADAM_B2 ** ADAM_STEP)
        d_ref[...] = -ADAM_LR * (m_hat / (jnp.sqrt(v_hat) + ADAM_EPS) + ADAM_WD * w_ref[...])
        m2_ref[...] = m2
        v2_ref[...] = v2

    spec = pl.BlockSpec((None, tr, cols), lambda l, i: (l, i, 0))
    flat = (lead, rows_n, cols)
    outs = pl.pallas_call(
        body, name=name, grid=(lead, rows_n // tr), in_specs=[spec] * 4, out_specs=[spec] * 3,
        out_shape=[jax.ShapeDtypeStruct(flat, F32)] * 3,
        compiler_params=_cparams(("parallel", "parallel")),
    )(*[t.reshape(flat) for t in (w, g, m, v)])
    return [o.reshape(shape) for o in outs]


WEIGHTS = ("ffn1_norm", "ffn1_w_gate", "ffn1_w_up", "ffn1_w_down", "mix_norm", "w_in", "pool_w", "pool_scale", "conv_w",
           "conv_b", "dt_bias", "a_log", "d_skip", "ssd_norm", "w_out", "ffn2_norm", "ffn2_w_gate", "ffn2_w_up",
           "ffn2_w_down", "final_norm")
BIG = {"ffn1_w_gate": "col", "ffn1_w_up": "col", "ffn1_w_down": "row", "w_in": "win", "w_out": "row",
       "ffn2_w_gate": "col", "ffn2_w_up": "col", "ffn2_w_down": "row"}
SMALL = tuple(n for n in WEIGHTS if n not in BIG and n != "conv_w")
REF_DT = 3072


def _pack(parts):
    flat = jnp.concatenate([p.reshape(-1) for p in parts])
    rows_n = -(-flat.shape[0] // (8 * LANES)) * 8
    return jnp.pad(flat, (0, rows_n * LANES - flat.shape[0])).reshape(rows_n, LANES)


def _unpack(block, shapes):
    flat, out, at = block.reshape(-1), [], 0
    for s in shapes:
        n = math.prod(s)
        out.append(flat[at:at + n].reshape(s))
        at += n
    return out


def _train_step(a):
    depth = a["ffn1_norm"].shape[0]
    x_id, y_id, c_id = _axes()
    chip = 2 * x_id + y_id
    where = jnp.stack([c_id, chip]).astype(jnp.int32)

    big = list(BIG)
    south = c_id == 0
    full = [dict() for _ in range(depth)]
    conv_full = []

    def fetch(l, names):
        def done(blocks):
            for n, g in zip(names, blocks):
                full[l][n] = jnp.concatenate([g[s] for s in range(N_CHIPS)], axis=0 if BIG[n] == "row" else 1)

        return [_gather_comm([a[n][l].astype(BF16) for n in names], done)] if l < depth else []

    ffn1, mix, ffn2 = ["ffn1_w_gate", "ffn1_w_up", "ffn1_w_down"], ["w_in", "w_out"], ["ffn2_w_gate", "ffn2_w_up", "ffn2_w_down"]
    conv_done = lambda blocks: conv_full.append(jnp.concatenate([blocks[0][s] for s in range(N_CHIPS)], axis=2))
    _run_comm("gather_first", fetch(0, ffn1) + [_gather_comm([a["conv_w"]], conv_done)])
    conv_w = conv_full[0]
    heads128 = lambda v: jnp.pad(v, ((0, 0), (0, LANES - SSD_HEADS)))
    dt_bias, a_log, d_skip = heads128(a["dt_bias"]), heads128(a["a_log"]), heads128(a["d_skip"])

    def mixer_params(l):
        w_in = full[l]["w_in"]
        w_main = jnp.concatenate([w_in[:, :REF_DT], w_in[:, REF_DT + SSD_HEADS:]], axis=1)
        w_dt = jnp.pad(w_in[:, REF_DT:REF_DT + SSD_HEADS], ((0, 0), (0, LANES - SSD_HEADS)))
        return dict(mix_norm=a["mix_norm"][l][None], w_main=w_main, w_dt=w_dt, pool_w=a["pool_w"][l],
                    pool_scale=a["pool_scale"][l][None], conv_w=conv_w[l], conv_b=a["conv_b"][l][None],
                    dt_bias=dt_bias[l][None], a_log=a_log[l][None], d_skip=d_skip[l][None],
                    ssd_norm=a["ssd_norm"][l][None], w_out=full[l]["w_out"])

    def ffn_params(l, which):
        return (a[which + "_norm"][l][None], full[l][which + "_w_gate"], full[l][which + "_w_up"], full[l][which + "_w_down"])

    h = a["x"][0]
    saved, mixer_p = [], []
    for l in range(depth):
        carry = lambda plan: (lambda call: fetch(*plan[call]) if call in plan else [])
        h, s1 = _ffn_fwd(f"l{l}_ffn1", h, *ffn_params(l, "ffn1"),
                         hooks=carry({"up": (l, mix + ffn2[:1]), "down": (l, ffn2[1:2])}))
        mixer_p.append(mixer_params(l))
        h, sm = _mixer_fwd(f"l{l}_mix", h, mixer_p[l], hooks=carry({"in": (l, ffn2[2:])}))
        h, s2 = _ffn_fwd(f"l{l}_ffn2", h, *ffn_params(l, "ffn2"),
                         hooks=carry({"up": (l + 1, ffn1[:2]), "down": (l + 1, ffn1[2:])}))
        saved.append((s1, sm, s2))
    loss_part, dh, dh_bf, dfinal = _loss_head("loss_head", h, a["final_norm"][None], a["loss_target"][0])

    small = {n: [None] * depth for n in SMALL if n != "final_norm"}
    small["conv_w"] = [None] * depth
    groups = []

    def grouper(tag, names):
        def make(*arrs):
            groups.append(_GradGroup(tag, names, [BIG[n] for n in names], arrs, where, south))
            return groups[-1]
        return make

    above = None
    for l in reversed(range(depth)):
        s1, sm, s2 = saved[l]
        hooks = _no_hooks if above is None else (lambda call, g=above: {"dhm": lambda: [g.chips([0, 1])],
                                                                        "dwd": lambda: [g.chips([2])],
                                                                        "dwgu": lambda: [g.share()]}.get(call, list)())
        dh, dh_bf, small["ffn2_norm"][l], f2 = _ffn_bwd(f"l{l}_ffn2", s2, *ffn_params(l, "ffn2"), dh, dh_bf, hooks=hooks,
                                                        group=grouper(f"l{l}_ffn2", ffn2))
        hooks = lambda call, g=f2: {"dmix": lambda: [g.chips([1])], "dwout": lambda: [g.chips([0])],
                                    "dwin": lambda: [g.chips([2])], "du": lambda: [g.share()]}.get(call, list)()
        dh, dh_bf, g, mx = _mixer_bwd(f"l{l}_mix", sm, mixer_p[l], dh, dh_bf, hooks=hooks, group=grouper(f"l{l}_mix", mix))
        for n in ("mix_norm", "pool_w", "pool_scale", "conv_w", "conv_b", "ssd_norm"):
            small[n][l] = g[n]
        for n in ("dt_bias", "a_log", "d_skip"):
            small[n][l] = g[n][:, :SSD_HEADS]
        hooks = lambda call, g=mx: {"dhm": lambda: [g.chips([0, 1])], "dwd": lambda: [g.share()]}.get(call, list)()
        dh, dh_bf, small["ffn1_norm"][l], above = _ffn_bwd(f"l{l}_ffn1", s1, *ffn_params(l, "ffn1"), dh, dh_bf, hooks=hooks,
                                                           group=grouper(f"l{l}_ffn1", ffn1))
    grad_x = dh[None]
    _run_comm("tail_chips", [above.chips([0, 1, 2])])
    _run_comm("tail_share", [above.share()])
    per_layer = {}
    for grp in groups:
        per_layer.update({(grp.tag, n): g for n, g in grp.grads().items()})
    grad = {n: jnp.stack([per_layer[(f"l{l}_{'mix' if n in mix else n[:4]}", n)] for l in range(depth)]) for n in big}

    small_full = {n: jnp.stack([t.reshape(a[n].shape[1:]) for t in small[n]]) for n in SMALL if n != "final_norm"}
    small_full["final_norm"] = dfinal.reshape(a["final_norm"].shape)
    conv_full = jnp.stack(small["conv_w"])
    shapes = [a[n].shape for n in SMALL] + [conv_full.shape]
    reduced = _unpack(_allreduce_small("allreduce_small", _pack([small_full[n] for n in SMALL] + [conv_full])), shapes)
    grad.update(zip(SMALL, reduced[:-1]))
    shard = a["conv_w"].shape[2]
    grad["conv_w"] = lax.dynamic_slice_in_dim(reduced[-1], chip * shard, shard, axis=2)

    delta, new_m, new_v = {}, {}, {}
    for n in big + ["conv_w"]:
        delta[n], new_m[n], new_v[n] = _adamw(f"adamw_{n}", a[n], grad[n], a["m_" + n], a["v_" + n])
    packed = [_pack([a[pre + n] for n in SMALL]) for pre in ("", "m_", "v_")]
    outs = _adamw("adamw_small", packed[0], _pack([grad[n] for n in SMALL]), packed[1], packed[2])
    for store, block in zip((delta, new_m, new_v), outs):
        store.update(zip(SMALL, _unpack(block, [a[n].shape for n in SMALL])))

    loss = lax.psum(loss_part[0, 0], ("x", "y", "c"))
    return (loss, grad_x, *[grad[n] for n in WEIGHTS], *[delta[n] for n in WEIGHTS], *[new_m[n] for n in WEIGHTS],
            *[new_v[n] for n in WEIGHTS])


def kernel(x, ffn1_norm, ffn1_w_gate, ffn1_w_up, ffn1_w_down, mix_norm, w_in, pool_w, pool_scale, conv_w, conv_b, dt_bias, a_log, d_skip, ssd_norm, w_out, ffn2_norm, ffn2_w_gate, ffn2_w_up, ffn2_w_down, final_norm, loss_target, m_ffn1_norm, m_ffn1_w_gate, m_ffn1_w_up, m_ffn1_w_down, m_mix_norm, m_w_in, m_pool_w, m_pool_scale, m_conv_w, m_conv_b, m_dt_bias, m_a_log, m_d_skip, m_ssd_norm, m_w_out, m_ffn2_norm, m_ffn2_w_gate, m_ffn2_w_up, m_ffn2_w_down, m_final_norm, v_ffn1_norm, v_ffn1_w_gate, v_ffn1_w_up, v_ffn1_w_down, v_mix_norm, v_w_in, v_pool_w, v_pool_scale, v_conv_w, v_conv_b, v_dt_bias, v_a_log, v_d_skip, v_ssd_norm, v_w_out, v_ffn2_norm, v_ffn2_w_gate, v_ffn2_w_up, v_ffn2_w_down, v_final_norm):
    return _train_step(dict(locals()))
```

```python
import functools
import math

import jax
import jax.numpy as jnp
from jax import lax
from jax.experimental import pallas as pl
from jax.experimental.pallas import tpu as pltpu

F32 = jnp.float32
BF16 = jnp.bfloat16
MESH_ID = pl.DeviceIdType.MESH

RMS_EPS = 1e-6
POOL_WINDOWS = (2, 4, 8, 16)
LANES = 128
HEAD_DIM = 64
SSD_HEADS = 16
SSD_CHUNK = 256
ATTN_BLOCK = 128
HALO = 16
EXP_UNDERFLOW = -105.0
VMEM_LIMIT = 56 * 1024 * 1024
MM_SUB = 256
N_CHIPS = 4

ADAM_LR = 0.001
ADAM_B1 = 0.9
ADAM_B2 = 0.999
ADAM_EPS = 1e-08
ADAM_WD = 0.01
ADAM_STEP = 10

C_POOL, C_Z, C_XBC, C_Q, C_K, C_V, C_END = 0, 512, 1536, 3072, 3584, 4096, 4608


def _cparams(sem):
    return pltpu.CompilerParams(dimension_semantics=sem, vmem_limit_bytes=VMEM_LIMIT)


def _tile(dim, pref, unit=LANES):
    if dim <= pref:
        return dim
    t = (pref // unit) * unit
    while t > unit and dim % t:
        t -= unit
    assert dim % t == 0, (dim, pref)
    return t


def _sigmoid(x):
    return 0.5 * jnp.tanh(0.5 * x) + 0.5


def _silu(x):
    return x * _sigmoid(x)


def _dsilu(x):
    s = _sigmoid(x)
    return s * (1.0 + x * (1.0 - s))


def _dot(a, b):
    return jnp.dot(a.astype(BF16), b.astype(BF16), preferred_element_type=F32)


def _dot_nt(a, b):
    return lax.dot_general(a.astype(BF16), b.astype(BF16), (((1,), (1,)), ((), ())), preferred_element_type=F32)


def _dot_tn(a, b):
    return lax.dot_general(a.astype(BF16), b.astype(BF16), (((0,), (0,)), ((), ())), preferred_element_type=F32)


def _split3(x):
    hi = x.astype(BF16)
    r = x - hi.astype(F32)
    mid = r.astype(BF16)
    lo = (r - mid.astype(F32)).astype(BF16)
    return hi, mid, lo


def _dot3(x, m):
    hi, mid, lo = _split3(x)
    dn = (((1,), (0,)), ((), ()))
    f = lambda p: lax.dot_general(p, m, dn, preferred_element_type=F32)
    return f(hi) + f(mid) + f(lo)


def _dot3_left(m, x):
    hi, mid, lo = _split3(x)
    dn = (((1,), (0,)), ((), ()))
    f = lambda p: lax.dot_general(m, p, dn, preferred_element_type=F32)
    return f(hi) + f(mid) + f(lo)


def _iota(shape, axis):
    return lax.broadcasted_iota(jnp.int32, shape, axis)


def _col(x, h):
    return jnp.sum(jnp.where(_iota(x.shape, 1) == h, x, 0.0), axis=1, keepdims=True)


def _roll_down(x, k):
    return x if k == 0 else pltpu.roll(x, k, 0)


def _roll_up(x, k):
    return x if k == 0 else pltpu.roll(x, x.shape[0] - k, 0)


class _Comm:
    def __init__(self, ins, outs, n_remote, n_local, make, done):
        self.ins, self.outs, self.n_remote, self.n_local, self.make, self.done = list(ins), list(outs), n_remote, n_local, make, done


def _comm_plan(hosted):
    ins = [a for cm in hosted for a in cm.ins]
    outs = [o for cm in hosted for o in cm.outs]
    n_remote = sum(cm.n_remote for cm in hosted)
    n_local = sum(cm.n_local for cm in hosted)

    def build(in_refs, out_refs, send, recv, loc):
        starts, waits, i0, o0, r0, l0 = [], [], 0, 0, 0, 0
        for cm in hosted:
            s, w = cm.make(in_refs[i0:i0 + len(cm.ins)], out_refs[o0:o0 + len(cm.outs)], send, recv, loc, r0, l0)
            starts, waits = starts + s, waits + w
            i0, o0, r0, l0 = i0 + len(cm.ins), o0 + len(cm.outs), r0 + cm.n_remote, l0 + cm.n_local
        return starts, waits

    def deliver(results):
        o0 = 0
        for cm in hosted:
            cm.done(results[o0:o0 + len(cm.outs)])
            o0 += len(cm.outs)

    sems = [pltpu.SemaphoreType.DMA((max(n_remote, 1),)), pltpu.SemaphoreType.DMA((max(n_remote, 1),)),
            pltpu.SemaphoreType.DMA((max(n_local, 1),))]
    return ins, outs, sems, build, deliver


def _mm(name, a_list, b_list, *, nt, epilogue, out_dtypes, acc_of=None, extras=(), hosted=(), ta=False,
        tm=1024, tn=512, tk=1024):
    n_pairs = len(a_list)
    acc_of = list(acc_of) if acc_of is not None else [0] * n_pairs
    n_acc = max(acc_of) + 1
    m_dim, k_dim = a_list[0].shape[::-1] if ta else a_list[0].shape
    n_dim = b_list[0].shape[0] if nt else b_list[0].shape[1]
    tm, tn, tk = _tile(m_dim, tm, 8), _tile(n_dim, tn), _tile(k_dim, tk)
    nk = k_dim // tk
    n_ex, n_out = len(extras), len(out_dtypes)
    hosted = list(hosted)
    c_ins, c_outs, c_sems, c_build, c_deliver = _comm_plan(hosted)
    n_ci, n_co = len(c_ins), len(c_outs)
    n_scr = 0 if nk == 1 else n_acc
    grid = (m_dim // tm, n_dim // tn, nk)
    sub = MM_SUB if (nk == 1 and tn > MM_SUB and tn % MM_SUB == 0) else tn

    def body(*refs):
        a_refs = refs[:n_pairs]
        b_refs = refs[n_pairs:2 * n_pairs]
        e_refs = refs[2 * n_pairs:2 * n_pairs + n_ex]
        ci_refs = refs[2 * n_pairs + n_ex:2 * n_pairs + n_ex + n_ci]
        first_out = 2 * n_pairs + n_ex + n_ci
        o_refs = refs[first_out:first_out + n_out]
        co_refs = refs[first_out + n_out:first_out + n_out + n_co]
        acc_refs = refs[first_out + n_out + n_co:first_out + n_out + n_co + n_scr]
        sems = refs[first_out + n_out + n_co + n_scr:]
        if hosted:
            at = [pl.program_id(d) for d in range(3)]
            starts, waits = c_build(ci_refs, co_refs, *sems)

            @pl.when((at[0] == 0) & (at[1] == 0) & (at[2] == 0))
            def _():
                for cp in starts:
                    cp.start()

        if nk == 1:
            for s in range(tn // sub):
                cs = slice(s * sub, (s + 1) * sub)
                accs = [None] * n_acc
                for p in range(n_pairs):
                    a = a_refs[p][...]
                    d = _dot_tn(a, b_refs[p][:, cs]) if ta else _dot_nt(a, b_refs[p][cs, :]) if nt else _dot(a, b_refs[p][:, cs])
                    accs[acc_of[p]] = d if accs[acc_of[p]] is None else accs[acc_of[p]] + d
                outs = epilogue(accs, [e[:, cs] for e in e_refs])
                for o_ref, o in zip(o_refs, outs):
                    o_ref[:, cs] = o.astype(o_ref.dtype)
        else:
            k = pl.program_id(2)

            @pl.when(k == 0)
            def _():
                for acc in acc_refs:
                    acc[...] = jnp.zeros_like(acc)

            for p in range(n_pairs):
                a, b = a_refs[p][...], b_refs[p][...]
                acc_refs[acc_of[p]][...] += _dot_tn(a, b) if ta else _dot_nt(a, b) if nt else _dot(a, b)

            @pl.when(k == nk - 1)
            def _():
                outs = epilogue([acc[...] for acc in acc_refs], [e[...] for e in e_refs])
                for o_ref, o in zip(o_refs, outs):
                    o_ref[...] = o.astype(o_ref.dtype)

        if hosted:
            @pl.when((at[0] == grid[0] - 1) & (at[1] == grid[1] - 1) & (at[2] == grid[2] - 1))
            def _():
                for wait in waits:
                    wait()

    a_spec = pl.BlockSpec((tk, tm), lambda i, j, k: (k, i)) if ta else pl.BlockSpec((tm, tk), lambda i, j, k: (i, k))
    b_spec = pl.BlockSpec((tn, tk), lambda i, j, k: (j, k)) if nt else pl.BlockSpec((tk, tn), lambda i, j, k: (k, j))
    t_spec = pl.BlockSpec((tm, tn), lambda i, j, k: (i, j))
    results = pl.pallas_call(
        body, name=name, grid=grid,
        in_specs=[a_spec] * n_pairs + [b_spec] * n_pairs + [t_spec] * n_ex + _any_specs(n_ci),
        out_specs=[t_spec] * n_out + _any_specs(n_co),
        out_shape=[jax.ShapeDtypeStruct((m_dim, n_dim), dt) for dt in out_dtypes] + c_outs,
        scratch_shapes=[pltpu.VMEM((tm, tn), F32)] * n_scr + (c_sems if hosted else []),
        compiler_params=_cparams(("arbitrary",) * 3 if hosted else ("parallel", "parallel", "arbitrary")),
    )(*a_list, *b_list, *extras, *c_ins)
    c_deliver(results[n_out:])
    return results[:n_out]


def _rms_fwd(name, h, g):
    s_len, d = h.shape
    ts = _tile(s_len, 512, 8)

    def body(h_ref, g_ref, u_ref):
        x = h_ref[...]
        rstd = lax.rsqrt(jnp.mean(x * x, axis=-1, keepdims=True) + RMS_EPS)
        u_ref[...] = (x * rstd * g_ref[...]).astype(BF16)

    return pl.pallas_call(
        body, name=name, grid=(s_len // ts,),
        in_specs=[pl.BlockSpec((ts, d), lambda i: (i, 0)), pl.BlockSpec((1, d), lambda i: (0, 0))],
        out_specs=pl.BlockSpec((ts, d), lambda i: (i, 0)),
        out_shape=jax.ShapeDtypeStruct((s_len, d), BF16),
        compiler_params=_cparams(("parallel",)),
    )(h, g)


def _rms_bwd(name, h, du, dres, g):
    s_len, d = h.shape
    ts = _tile(s_len, 256, 8)

    def body(h_ref, du_ref, dres_ref, g_ref, dh_ref, dhb_ref, dg_ref):
        x = h_ref[...]
        rstd = lax.rsqrt(jnp.mean(x * x, axis=-1, keepdims=True) + RMS_EPS)
        n = x * rstd
        dn = du_ref[...] * g_ref[...]
        dh = dres_ref[...] + rstd * (dn - n * jnp.mean(dn * n, axis=-1, keepdims=True))
        dh_ref[...] = dh
        dhb_ref[...] = dh.astype(BF16)

        @pl.when(pl.program_id(0) == 0)
        def _():
            dg_ref[...] = jnp.zeros_like(dg_ref)

        dg_ref[...] += jnp.sum(du_ref[...] * n, axis=0, keepdims=True)

    row = pl.BlockSpec((ts, d), lambda i: (i, 0))
    vec = pl.BlockSpec((1, d), lambda i: (0, 0))
    return pl.pallas_call(
        body, name=name, grid=(s_len // ts,),
        in_specs=[row, row, row, vec], out_specs=[row, row, vec],
        out_shape=[jax.ShapeDtypeStruct((s_len, d), F32), jax.ShapeDtypeStruct((s_len, d), BF16),
                   jax.ShapeDtypeStruct((1, d), F32)],
        compiler_params=_cparams(("arbitrary",)),
    )(h, du, dres, g)


def _loss_head(name, h, g, target):
    s_len, d = h.shape
    ts = _tile(s_len, 256, 8)

    def body(h_ref, g_ref, t_ref, loss_ref, dh_ref, dhb_ref, dg_ref):
        x = h_ref[...]
        rstd = lax.rsqrt(jnp.mean(x * x, axis=-1, keepdims=True) + RMS_EPS)
        n = x * rstd
        err = n * g_ref[...] - t_ref[...]
        dy = err * (1.0 / d)
        dn = dy * g_ref[...]
        dh = rstd * (dn - n * jnp.mean(dn * n, axis=-1, keepdims=True))
        dh_ref[...] = dh
        dhb_ref[...] = dh.astype(BF16)

        @pl.when(pl.program_id(0) == 0)
        def _():
            dg_ref[...] = jnp.zeros_like(dg_ref)
            loss_ref[...] = jnp.zeros_like(loss_ref)

        dg_ref[...] += jnp.sum(dy * n, axis=0, keepdims=True)
        part = jnp.sum(jnp.sum(err * err, axis=1, keepdims=True), axis=0, keepdims=True) * (0.5 / d)
        loss_ref[...] += jnp.broadcast_to(part, loss_ref.shape)

    row = pl.BlockSpec((ts, d), lambda i: (i, 0))
    vec = pl.BlockSpec((1, d), lambda i: (0, 0))
    lspec = pl.BlockSpec((1, LANES), lambda i: (0, 0))
    return pl.pallas_call(
        body, name=name, grid=(s_len // ts,),
        in_specs=[row, vec, row], out_specs=[lspec, row, row, vec],
        out_shape=[jax.ShapeDtypeStruct((1, LANES), F32), jax.ShapeDtypeStruct((s_len, d), F32),
                   jax.ShapeDtypeStruct((s_len, d), BF16), jax.ShapeDtypeStruct((1, d), F32)],
        compiler_params=_cparams(("arbitrary",)),
    )(h, g, target)


def _no_hooks(_):
    return []


def _ffn_fwd(tag, h, g, wg, wu, wd, hooks=_no_hooks):
    u = _rms_fwd(tag + "_norm", h, g)

    def up(accs, _):
        a, b = accs
        return a, b, _silu(a) * b

    a, b, hm = _mm(tag + "_up", [u, u], [wg, wu], nt=False, acc_of=[0, 1], epilogue=up, out_dtypes=[BF16, BF16, BF16],
                   hosted=hooks("up"))
    (h2,) = _mm(tag + "_down", [hm], [wd], nt=False, extras=[h], epilogue=lambda accs, ex: [ex[0] + 0.5 * accs[0]],
                out_dtypes=[F32], hosted=hooks("down"), tm=1024, tn=1024, tk=1408)
    return h2, (h, u, a, b, hm)


def _ffn_bwd(tag, saved, g, wg, wu, wd, dh2, dh2_bf, hooks=_no_hooks, group=None):
    h, u, a, b, hm = saved

    def dact(accs, ex):
        af, bf = ex[0].astype(F32), ex[1].astype(F32)
        dhm = 0.5 * accs[0]
        return dhm * bf * _dsilu(af), dhm * _silu(af)

    da, db = _mm(tag + "_dhm", [dh2_bf], [wd], nt=True, extras=[a, b], epilogue=dact, out_dtypes=[BF16, BF16],
                 hosted=hooks("dhm"), tk=2048)
    (dwd,) = _mm(tag + "_dwd", [hm], [dh2_bf], nt=False, ta=True, epilogue=lambda accs, _: [0.5 * accs[0]],
                 out_dtypes=[F32], hosted=hooks("dwd"), tm=1408, tn=1024, tk=512)
    dwg, dwu = _mm(tag + "_dwgu", [u, u], [da, db], nt=False, ta=True, acc_of=[0, 1], epilogue=lambda accs, _: accs,
                   out_dtypes=[F32, F32], hosted=hooks("dwgu"), tm=2048, tn=512, tk=1024)
    mine = group(dwg, dwu, dwd) if group else (dwg, dwu, dwd)
    (du,) = _mm(tag + "_du", [da, db], [wg, wu], nt=True, epilogue=lambda accs, _: accs, out_dtypes=[F32],
                hosted=[mine.swap()] if group else [], tm=1024, tn=1024, tk=1408)
    dh, dh_bf, dg = _rms_bwd(tag + "_dnorm", h, du, dh2, g)
    return dh, dh_bf, dg, mine


def _softplus(x):
    e = jnp.exp(-jnp.abs(x))
    u = 1.0 + e
    log1p_e = jnp.where(u == 1.0, e, jnp.log(u) * (e / jnp.where(u == 1.0, 1.0, u - 1.0)))
    return jnp.maximum(x, 0.0) + log1p_e


def _row_spec(ts, width, colblock):
    return pl.BlockSpec((ts, width), lambda i: (i, colblock))


def _halo_before_spec(ts, width, colblock):
    r = ts // HALO
    return pl.BlockSpec((HALO, width), lambda i: (jnp.maximum(i * r - 1, 0), colblock))


def _halo_after_spec(ts, width, colblock, s_len):
    r = ts // HALO
    return pl.BlockSpec((HALO, width), lambda i: (jnp.minimum((i + 1) * r, s_len // HALO - 1), colblock))


def _const_spec(shape):
    nd = len(shape)
    return pl.BlockSpec(shape, lambda *_: (0,) * nd)


def _window_sum(e, win, roll):
    s, sh = e, 1
    while sh < win:
        s = s + roll(s, sh)
        sh *= 2
    return s


def _pool_center(ext, x, t, gi, win):
    sl = slice(gi * LANES, (gi + 1) * LANES)
    s = _window_sum(ext[:, sl], win, _roll_down)
    cnt = jnp.minimum(t + 1, win).astype(F32)
    return s[HALO:] / cnt - x[:, sl]


def _pool_fwd(name, proj, pw, scale):
    s_len = proj.shape[0]
    ts = _tile(s_len, 512, 8)
    width = len(POOL_WINDOWS) * LANES

    def body(x_ref, hb_ref, pw_ref, sc_ref, o_ref):
        i = pl.program_id(0)
        x = x_ref[...]
        ext = jnp.concatenate([jnp.where(i == 0, 0.0, hb_ref[...]), x], axis=0)
        t = i * ts + _iota((ts, 1), 0)
        for gi, win in enumerate(POOL_WINDOWS):
            sl = slice(gi * LANES, (gi + 1) * LANES)
            c = _pool_center(ext, x, t, gi, win)
            o_ref[:, sl] = (_dot(c, pw_ref[gi]) * sc_ref[:, sl]).astype(o_ref.dtype)

    return pl.pallas_call(
        body, name=name, grid=(s_len // ts,),
        in_specs=[_row_spec(ts, width, 0), _halo_before_spec(ts, width, 0), _const_spec(pw.shape), _const_spec(scale.shape)],
        out_specs=_row_spec(ts, width, 0),
        out_shape=jax.ShapeDtypeStruct((s_len, width), BF16),
        compiler_params=_cparams(("parallel",)),
    )(proj, proj, pw, scale)


def _pool_bwd(name, proj, dmixed, pw, scale):
    s_len = proj.shape[0]
    ts = _tile(s_len, 512, 8)
    n_tiles = s_len // ts
    width = len(POOL_WINDOWS) * LANES

    def body(x_ref, hb_ref, d_ref, da_ref, pw_ref, sc_ref, dx_ref, dpw_ref, dsc_ref):
        i = pl.program_id(0)
        x = x_ref[...]
        ext = jnp.concatenate([jnp.where(i == 0, 0.0, hb_ref[...]), x], axis=0)
        dout = d_ref[...]
        dext = jnp.concatenate([dout, jnp.where(i == n_tiles - 1, 0.0, da_ref[...])], axis=0)
        t = i * ts + _iota((ts, 1), 0)
        te = i * ts + _iota((ts + HALO, 1), 0)

        @pl.when(i == 0)
        def _():
            dpw_ref[...] = jnp.zeros_like(dpw_ref)
            dsc_ref[...] = jnp.zeros_like(dsc_ref)

        for gi, win in enumerate(POOL_WINDOWS):
            sl = slice(gi * LANES, (gi + 1) * LANES)
            c = _pool_center(ext, x, t, gi, win)
            o = _dot(c, pw_ref[gi])
            dsc_ref[:, sl] += jnp.sum(dout[:, sl] * o, axis=0, keepdims=True)
            do_ext = dext[:, sl] * sc_ref[:, sl]
            dc = _dot_nt(do_ext, pw_ref[gi])
            e = dc / jnp.minimum(te + 1, win).astype(F32)
            back = _window_sum(e, win, _roll_up)
            dx_ref[:, sl] = (back[:ts] - dc[:ts]).astype(dx_ref.dtype)
            dpw_ref[gi] += _dot(c.T, do_ext[:ts])

    return pl.pallas_call(
        body, name=name, grid=(n_tiles,),
        in_specs=[_row_spec(ts, width, 0), _halo_before_spec(ts, width, 0), _row_spec(ts, width, 0),
                  _halo_after_spec(ts, width, 0, s_len), _const_spec(pw.shape), _const_spec(scale.shape)],
        out_specs=[_row_spec(ts, width, 0), _const_spec(pw.shape), _const_spec(scale.shape)],
        out_shape=[jax.ShapeDtypeStruct((s_len, width), BF16), jax.ShapeDtypeStruct(pw.shape, F32),
                   jax.ShapeDtypeStruct(scale.shape, F32)],
        compiler_params=_cparams(("arbitrary",)),
    )(proj, proj, dmixed, dmixed, pw, scale)


def _conv_pre(ext, w_ref, b_ref):
    k_len = w_ref.shape[0]
    y = _roll_down(ext, k_len - 1) * w_ref[0:1, :]
    for k in range(1, k_len):
        y = y + _roll_down(ext, k_len - 1 - k) * w_ref[k:k + 1, :]
    return y + b_ref[...]


def _conv_fwd(name, proj, w, b):
    s_len = proj.shape[0]
    width = w.shape[1]
    ts = _tile(s_len, 512, 8)
    cb = C_XBC // width

    def body(x_ref, hb_ref, w_ref, b_ref, o_ref):
        i = pl.program_id(0)
        ext = jnp.concatenate([jnp.where(i == 0, 0.0, hb_ref[...]), x_ref[...]], axis=0)
        o_ref[...] = _silu(_conv_pre(ext, w_ref, b_ref)[HALO:])

    return pl.pallas_call(
        body, name=name, grid=(s_len // ts,),
        in_specs=[_row_spec(ts, width, cb), _halo_before_spec(ts, width, cb), _const_spec(w.shape), _const_spec(b.shape)],
        out_specs=_row_spec(ts, width, 0),
        out_shape=jax.ShapeDtypeStruct((s_len, width), F32),
        compiler_params=_cparams(("parallel",)),
    )(proj, proj, w, b)


def _conv_bwd(name, proj, dact, w, b):
    s_len = proj.shape[0]
    width = w.shape[1]
    k_len = w.shape[0]
    ts = _tile(s_len, 512, 8)
    n_tiles = s_len // ts
    cb = C_XBC // width

    def body(x_ref, hb_ref, ha_ref, d_ref, da_ref, w_ref, b_ref, dx_ref, dw_ref, db_ref):
        i = pl.program_id(0)
        last = i == n_tiles - 1
        ext = jnp.concatenate([jnp.where(i == 0, 0.0, hb_ref[...]), x_ref[...], jnp.where(last, 0.0, ha_ref[...])], axis=0)
        pre = _conv_pre(ext, w_ref, b_ref)[HALO:]
        dpre = jnp.concatenate([d_ref[...], jnp.where(last, 0.0, da_ref[...])], axis=0) * _dsilu(pre)

        @pl.when(i == 0)
        def _():
            dw_ref[...] = jnp.zeros_like(dw_ref)
            db_ref[...] = jnp.zeros_like(db_ref)

        dx = _roll_up(dpre, k_len - 1) * w_ref[0:1, :]
        for k in range(1, k_len):
            dx = dx + _roll_up(dpre, k_len - 1 - k) * w_ref[k:k + 1, :]
        dx_ref[...] = dx[:ts].astype(dx_ref.dtype)
        dtile = dpre[:ts]
        for k in range(k_len):
            xk = _roll_down(ext, k_len - 1 - k)[HALO:HALO + ts]
            dw_ref[k:k + 1, :] += jnp.sum(dtile * xk, axis=0, keepdims=True)
        db_ref[...] += jnp.sum(dtile, axis=0, keepdims=True)

    return pl.pallas_call(
        body, name=name, grid=(n_tiles,),
        in_specs=[_row_spec(ts, width, cb), _halo_before_spec(ts, width, cb), _halo_after_spec(ts, width, cb, s_len),
                  _row_spec(ts, width, 0), _halo_after_spec(ts, width, 0, s_len), _const_spec(w.shape), _const_spec(b.shape)],
        out_specs=[_row_spec(ts, width, 0), _const_spec(w.shape), _const_spec(b.shape)],
        out_shape=[jax.ShapeDtypeStruct((s_len, width), BF16), jax.ShapeDtypeStruct(w.shape, F32),
                   jax.ShapeDtypeStruct(b.shape, F32)],
        compiler_params=_cparams(("arbitrary",)),
    )(proj, proj, proj, dact, dact, w, b)


def _pair_cols(c0, c1, lo_half):
    return jnp.where(lo_half, c0, c1)


def _ssd_common(dtr_ref, bias_ref, alog_ref, acs_t_ref):
    chunk = dtr_ref.shape[0]
    xpre = dtr_ref[...] + bias_ref[...]
    dt = _softplus(xpre)
    a_neg = -jnp.exp(alog_ref[...])
    tri = _iota((chunk, chunk), 1) <= _iota((chunk, chunk), 0)
    a_cs = _dot3_left(tri.astype(BF16), dt * a_neg)
    acs_t_ref[...] = a_cs.T
    a_last = jnp.sum(jnp.where(_iota(a_cs.shape, 0) == chunk - 1, a_cs, 0.0), axis=0, keepdims=True)
    return xpre, dt, a_neg, tri, a_cs, a_last


def _ssd_specs(chunk, order):
    xs = pl.BlockSpec((chunk, 1024), lambda c: (order(c), 0))
    bm = pl.BlockSpec((chunk, 256), lambda c: (order(c), 4))
    cm = pl.BlockSpec((chunk, 256), lambda c: (order(c), 5))
    lanes = pl.BlockSpec((chunk, LANES), lambda c: (order(c), 0))
    return xs, bm, cm, lanes


def _ssd_fwd(name, xbc, dtr, dt_bias, a_log, d_skip):
    s_len = xbc.shape[0]
    chunk = SSD_CHUNK
    nc = s_len // chunk
    n_pairs = SSD_HEADS // 2

    def body(xs_ref, b_ref, c_ref, dtr_ref, bias_ref, alog_ref, dsk_ref, y_ref, st_ref, state_ref, acs_t_ref):
        @pl.when(pl.program_id(0) == 0)
        def _():
            state_ref[...] = jnp.zeros_like(state_ref)

        _, dt, _, tri, a_cs, a_last = _ssd_common(dtr_ref, bias_ref, alog_ref, acs_t_ref)
        lo_half = _iota((chunk, LANES), 1) < HEAD_DIM
        lo_lane = _iota((1, LANES), 1) < HEAD_DIM
        lo_row = _iota((LANES, 1), 0) < HEAD_DIM
        dsk = dsk_ref[...]
        for g in range(2):
            gsl = slice(g * LANES, (g + 1) * LANES)
            bg, cg = b_ref[:, gsl], c_ref[:, gsl]
            gmat = _dot_nt(cg, bg)
            for pr in range(n_pairs // 2):
                pair = g * (n_pairs // 2) + pr
                h0, h1 = 2 * pair, 2 * pair + 1
                psl = slice(pair * LANES, (pair + 1) * LANES)
                x2 = xs_ref[:, psl]
                acs0, acs1 = _col(a_cs, h0), _col(a_cs, h1)
                xdt = x2 * _pair_cols(_col(dt, h0), _col(dt, h1), lo_half)
                y2 = jnp.zeros((chunk, LANES), F32)
                for h, acs_c, hmask in ((h0, acs0, lo_half), (h1, acs1, ~lo_half)):
                    lam = jnp.where(tri, jnp.exp(jnp.minimum(acs_c - acs_t_ref[h:h + 1, :], 0.0)), 0.0)
                    y2 = y2 + _dot(gmat * lam, jnp.where(hmask, xdt, 0.0))
                s2 = state_ref[pair]
                st_ref[0, pair] = s2
                y2 = y2 + _pair_cols(jnp.exp(acs0), jnp.exp(acs1), lo_half) * _dot_nt(cg, s2)
                y_ref[:, psl] = y2 + _pair_cols(_col(dsk, h0), _col(dsk, h1), lo_lane) * x2
                al0, al1 = _col(a_last, h0), _col(a_last, h1)
                wl2 = _pair_cols(jnp.exp(al0 - acs0), jnp.exp(al1 - acs1), lo_half)
                state_ref[pair] = _pair_cols(jnp.exp(al0), jnp.exp(al1), lo_row) * s2 + _dot((xdt * wl2).T, bg)

    xs, bm, cm, lanes = _ssd_specs(chunk, lambda c: c)
    vec = _const_spec((1, LANES))
    return pl.pallas_call(
        body, name=name, grid=(nc,),
        in_specs=[xs, bm, cm, lanes, vec, vec, vec],
        out_specs=[xs, pl.BlockSpec((1, n_pairs, LANES, LANES), lambda c: (c, 0, 0, 0))],
        out_shape=[jax.ShapeDtypeStruct((s_len, 1024), F32), jax.ShapeDtypeStruct((nc, n_pairs, LANES, LANES), F32)],
        scratch_shapes=[pltpu.VMEM((n_pairs, LANES, LANES), F32), pltpu.VMEM((LANES, chunk), F32)],
        compiler_params=_cparams(("arbitrary",)),
    )(xbc, xbc, xbc, dtr, dt_bias, a_log, d_skip)


def _ssd_bwd(name, xbc, dtr, states, dy, dt_bias, a_log, d_skip):
    s_len = xbc.shape[0]
    chunk = SSD_CHUNK
    nc = s_len // chunk
    n_pairs = SSD_HEADS // 2
    rev = lambda c: nc - 1 - c

    def body(xs_ref, b_ref, c_ref, dtr_ref, dy_ref, sin_ref, bias_ref, alog_ref, dsk_ref,
             dxs_ref, db_ref, dc_ref, ddtr_ref, dbias_ref, dalog_ref, ddsk_ref, dstate_ref, acs_t_ref):
        @pl.when(pl.program_id(0) == 0)
        def _():
            dstate_ref[...] = jnp.zeros_like(dstate_ref)
            dbias_ref[...] = jnp.zeros_like(dbias_ref)
            dalog_ref[...] = jnp.zeros_like(dalog_ref)
            ddsk_ref[...] = jnp.zeros_like(ddsk_ref)

        xpre, dt, a_neg, tri, a_cs, a_last = _ssd_common(dtr_ref, bias_ref, alog_ref, acs_t_ref)
        lane = _iota((chunk, LANES), 1)
        lo_half = lane < HEAD_DIM
        lane1 = _iota((1, LANES), 1)
        lo_lane = lane1 < HEAD_DIM
        lo_row = _iota((LANES, 1), 0) < HEAD_DIM
        head_row = _iota((LANES, chunk), 0)
        sq_row, sq_col = _iota((chunk, chunk), 0), _iota((chunk, chunk), 1)
        before = (sq_row < sq_col).astype(BF16)
        dsk = dsk_ref[...]
        da_rows = jnp.zeros((LANES, chunk), F32)
        yo = jnp.zeros((chunk, LANES), F32)
        to = jnp.zeros((chunk, LANES), F32)
        vs = jnp.zeros((1, LANES), F32)
        ddt = jnp.zeros((chunk, LANES), F32)
        ddsk = jnp.zeros((1, LANES), F32)

        def half_sums(v):
            lo = jnp.sum(jnp.where(lo_half, v, 0.0), axis=1, keepdims=True)
            return lo, jnp.sum(v, axis=1, keepdims=True) - lo

        for g in range(2):
            gsl = slice(g * LANES, (g + 1) * LANES)
            bg, cg = b_ref[:, gsl], c_ref[:, gsl]
            gmat = _dot_nt(cg, bg)
            dgm = jnp.zeros((chunk, chunk), F32)
            dbg = jnp.zeros((chunk, LANES), F32)
            dcg = jnp.zeros((chunk, LANES), F32)
            for pr in range(n_pairs // 2):
                pair = g * (n_pairs // 2) + pr
                h0, h1 = 2 * pair, 2 * pair + 1
                psl = slice(pair * LANES, (pair + 1) * LANES)
                x2, dy2 = xs_ref[:, psl], dy_ref[:, psl]
                acs0, acs1 = _col(a_cs, h0), _col(a_cs, h1)
                dt2 = _pair_cols(_col(dt, h0), _col(dt, h1), lo_half)
                xdt = x2 * dt2
                al0, al1 = _col(a_last, h0), _col(a_last, h1)
                v2 = _pair_cols(jnp.exp(acs0), jnp.exp(acs1), lo_half)
                wl2 = _pair_cols(jnp.exp(al0 - acs0), jnp.exp(al1 - acs1), lo_half)
                s_in, ds2 = sin_ref[0, pair], dstate_ref[pair]
                y_off = v2 * _dot_nt(cg, s_in)
                dx_state = wl2 * _dot_nt(bg, ds2)
                dx2 = dx_state
                for h, acs_c, hmask in ((h0, acs0, lo_half), (h1, acs1, ~lo_half)):
                    lam = jnp.where(tri, jnp.exp(jnp.minimum(acs_c - acs_t_ref[h:h + 1, :], 0.0)), 0.0)
                    m = gmat * lam
                    dyh = jnp.where(hmask, dy2, 0.0)
                    dx2 = dx2 + _dot(m.T, dyh)
                    dml = _dot_nt(dyh, xdt) * lam
                    dgm = dgm + dml
                    crossed = jnp.where(sq_row >= sq_col, _dot3(dml * gmat, before), 0.0)
                    da_rows = jnp.where(head_row == h, jnp.sum(crossed, axis=0, keepdims=True), da_rows)
                vdy = v2 * dy2
                dcg = dcg + _dot(vdy, s_in)
                dbg = dbg + _dot(wl2 * xdt, ds2)
                yo0, yo1 = half_sums(dy2 * y_off)
                yo = jnp.where(lane == h0, yo0, jnp.where(lane == h1, yo1, yo))
                to0, to1 = half_sums(dx_state * xdt)
                to = jnp.where(lane == h0, to0, jnp.where(lane == h1, to1, to))
                prod = jnp.sum(ds2 * s_in, axis=1, keepdims=True)
                e0 = jnp.sum(jnp.where(lo_row, prod, 0.0), axis=0, keepdims=True)
                e1 = jnp.sum(prod, axis=0, keepdims=True) - e0
                vs = jnp.where(lane1 == h0, jnp.exp(al0) * e0, jnp.where(lane1 == h1, jnp.exp(al1) * e1, vs))
                q0, q1 = half_sums(dx2 * x2)
                ddt = jnp.where(lane == h0, q0, jnp.where(lane == h1, q1, ddt))
                dxs_ref[:, psl] = dx2 * dt2 + _pair_cols(_col(dsk, h0), _col(dsk, h1), lo_lane) * dy2
                s0, s1 = half_sums(dy2 * x2)
                ddsk = jnp.where(lane1 == h0, jnp.sum(s0, axis=0, keepdims=True),
                                 jnp.where(lane1 == h1, jnp.sum(s1, axis=0, keepdims=True), ddsk))
                dstate_ref[pair] = _pair_cols(jnp.exp(al0), jnp.exp(al1), lo_row) * ds2 + _dot(vdy.T, cg)
            dc_ref[:, gsl] = dcg + _dot(dgm, bg)
            db_ref[:, gsl] = dbg + _dot(dgm.T, cg)

        da = (da_rows.T + _dot3_left((sq_col >= sq_row).astype(BF16), yo)
              + _dot3_left((sq_col < sq_row).astype(BF16), to) + vs)
        ddt = ddt + da * a_neg
        dalog_ref[...] += jnp.sum(da * dt, axis=0, keepdims=True) * a_neg
        ddtr = jnp.where(lane < SSD_HEADS, ddt * jax.nn.sigmoid(xpre), 0.0)
        ddtr_ref[...] = ddtr
        dbias_ref[...] += jnp.sum(ddtr, axis=0, keepdims=True)
        ddsk_ref[...] += ddsk

    xs, bm, cm, lanes = _ssd_specs(chunk, rev)
    vec = _const_spec((1, LANES))
    st_in = pl.BlockSpec((1, n_pairs, LANES, LANES), lambda c: (rev(c), 0, 0, 0))
    bc_out = pl.BlockSpec((chunk, 256), lambda c: (rev(c), 0))
    return pl.pallas_call(
        body, name=name, grid=(nc,),
        in_specs=[xs, bm, cm, lanes, xs, st_in, vec, vec, vec],
        out_specs=[xs, bc_out, bc_out, lanes, vec, vec, vec],
        out_shape=[jax.ShapeDtypeStruct((s_len, 1024), F32), jax.ShapeDtypeStruct((s_len, 256), F32),
                   jax.ShapeDtypeStruct((s_len, 256), F32), jax.ShapeDtypeStruct((s_len, LANES), F32),
                   jax.ShapeDtypeStruct((1, LANES), F32), jax.ShapeDtypeStruct((1, LANES), F32),
                   jax.ShapeDtypeStruct((1, LANES), F32)],
        scratch_shapes=[pltpu.VMEM((n_pairs, LANES, LANES), F32), pltpu.VMEM((LANES, chunk), F32)],
        compiler_params=_cparams(("arbitrary",)),
    )(xbc, xbc, xbc, dtr, dy, states, dt_bias, a_log, d_skip)


def _gatenorm_fwd(name, y, proj, g):
    s_len = y.shape[0]
    ts = _tile(s_len, 512, 8)
    gw = 512

    def body(y_ref, z_ref, g_ref, o_ref):
        yg = y_ref[...] * _silu(z_ref[...])
        rstd = lax.rsqrt(jnp.mean(yg * yg, axis=-1, keepdims=True) + RMS_EPS)
        o_ref[...] = (yg * rstd * g_ref[...]).astype(o_ref.dtype)

    return pl.pallas_call(
        body, name=name, grid=(2, s_len // ts),
        in_specs=[pl.BlockSpec((ts, gw), lambda gi, i: (i, gi)), pl.BlockSpec((ts, gw), lambda gi, i: (i, C_Z // gw + gi)),
                  pl.BlockSpec((1, gw), lambda gi, i: (0, gi))],
        out_specs=pl.BlockSpec((ts, gw), lambda gi, i: (i, gi)),
        out_shape=jax.ShapeDtypeStruct((s_len, 2 * gw), BF16),
        compiler_params=_cparams(("parallel", "parallel")),
    )(y, proj, g)


def _gatenorm_bwd(name, y, proj, g, dmixed):
    s_len = y.shape[0]
    ts = _tile(s_len, 512, 8)
    gw = 512

    def body(y_ref, z_ref, g_ref, d_ref, dy_ref, dz_ref, dg_ref):
        yv, z = y_ref[...], z_ref[...]
        sz = _silu(z)
        yg = yv * sz
        rstd = lax.rsqrt(jnp.mean(yg * yg, axis=-1, keepdims=True) + RMS_EPS)
        n = yg * rstd
        dn = d_ref[...] * g_ref[...]
        dyg = rstd * (dn - n * jnp.mean(dn * n, axis=-1, keepdims=True))
        dy_ref[...] = dyg * sz
        dz_ref[...] = (dyg * yv * _dsilu(z)).astype(dz_ref.dtype)

        @pl.when(pl.program_id(1) == 0)
        def _():
            dg_ref[...] = jnp.zeros_like(dg_ref)

        dg_ref[...] += jnp.sum(d_ref[...] * n, axis=0, keepdims=True)

    grp = pl.BlockSpec((ts, gw), lambda gi, i: (i, gi))
    vec = pl.BlockSpec((1, gw), lambda gi, i: (0, gi))
    return pl.pallas_call(
        body, name=name, grid=(2, s_len // ts),
        in_specs=[grp, pl.BlockSpec((ts, gw), lambda gi, i: (i, C_Z // gw + gi)), vec,
                  pl.BlockSpec((ts, gw), lambda gi, i: (i, 1 + gi))],
        out_specs=[grp, grp, vec],
        out_shape=[jax.ShapeDtypeStruct((s_len, 2 * gw), F32), jax.ShapeDtypeStruct((s_len, 2 * gw), BF16),
                   jax.ShapeDtypeStruct((1, 2 * gw), F32)],
        compiler_params=_cparams(("parallel", "arbitrary")),
    )(y, proj, g, dmixed)


def _stack_heads(x2, lo):
    return jnp.concatenate([jnp.where(lo, x2, 0.0), jnp.where(lo, 0.0, x2)], axis=0)


def _dot3_stacked(x, m):
    n = x.shape[0]
    y = jnp.dot(jnp.concatenate(_split3(x), axis=0), m, preferred_element_type=F32)
    return y[:n] + y[n:2 * n] + y[2 * n:]


def _attn_scores(q_st, kblk, mask, ustrict, r):
    z = _dot_nt(q_st, kblk)
    sp = jnp.log1p(jnp.exp(-jnp.abs(z)))
    ls = jnp.minimum(z, 0.0) - sp
    lm_raw = jnp.minimum(-z, 0.0) - sp
    lm = jnp.where(mask, lm_raw, 0.0)
    w = jnp.where(mask, jnp.exp(ls + _dot3_stacked(lm, ustrict) + r), 0.0)
    return ls, lm_raw, lm, w


def _attn_masks(qi, tq, tk):
    row = _iota((2 * tq, tk), 0)
    return qi * tq + jnp.where(row < tq, row, row - tq), _iota((2 * tq, tk), 1)


def _attn_tiles(s_len):
    tk = ATTN_BLOCK
    return next(m * tk for m in (2, 1) if s_len % (m * tk) == 0), tk


def _attn_kv(proj):
    return proj[:, C_K:C_END].astype(BF16)


def _attn_specs(tq, s_len):
    qcol, vcol = C_Q // LANES, (C_V - C_K) // LANES
    q = pl.BlockSpec((tq, LANES), lambda p, i: (i, qcol + p))
    k = pl.BlockSpec((s_len, LANES), lambda p, i: (0, p))
    v = pl.BlockSpec((s_len, LANES), lambda p, i: (0, vcol + p))
    return q, k, v


def _attn_fwd(name, proj, kv):
    s_len = proj.shape[0]
    tq, tk = _attn_tiles(s_len)
    n_slabs = 4

    def body(q_ref, k_ref, v_ref, o_ref):
        qi = pl.program_id(1)
        q2 = q_ref[...] * (HEAD_DIM ** -0.5)
        lo = _iota((tq, LANES), 1) < HEAD_DIM
        lo_k = _iota((tk, LANES), 1) < HEAD_DIM
        q_st = _stack_heads(q2, lo).astype(BF16)
        ustrict = (_iota((tk, tk), 0) > _iota((tk, tk), 1)).astype(BF16)
        q_pos, k_off = _attn_masks(qi, tq, tk)

        def step(carry):
            kb, _, r, acc = carry
            rows = pl.ds(pl.multiple_of(kb * tk, tk), tk)
            kblk, vblk = k_ref[rows, :], v_ref[rows, :]
            _, _, lm, w = _attn_scores(q_st, kblk, kb * tk + k_off < q_pos, ustrict, r)
            wb = w.astype(BF16)
            acc = acc + _dot(jnp.concatenate([wb[:tq], wb[tq:]], axis=1), _stack_heads(vblk, lo_k))
            r = r + jnp.sum(lm, axis=1, keepdims=True)
            return kb - 1, (jnp.max(r) > EXP_UNDERFLOW).astype(jnp.int32), r, acc

        init = ((qi + 1) * (tq // tk) - 1, jnp.int32(1), jnp.zeros((2 * tq, 1), F32), jnp.zeros((tq, LANES), F32))
        o_ref[...] = lax.while_loop(lambda c: (c[0] >= 0) & (c[1] > 0), step, init)[3]

    q, k, v = _attn_specs(tq, s_len)
    return pl.pallas_call(
        body, name=name, grid=(n_slabs, s_len // tq),
        in_specs=[q, k, v], out_specs=pl.BlockSpec((tq, LANES), lambda p, i: (i, p)),
        out_shape=jax.ShapeDtypeStruct((s_len, n_slabs * LANES), F32),
        compiler_params=_cparams(("parallel", "arbitrary")),
    )(proj, kv, kv)


def _attn_bwd(name, proj, kv, dmixed):
    s_len = proj.shape[0]
    tq, tk = _attn_tiles(s_len)
    n_slabs = 4
    scale = HEAD_DIM ** -0.5

    def body(q_ref, k_ref, v_ref, do_ref, dq_ref, dk_ref, dv_ref, dk_acc, dv_acc, r_hist):
        qi = pl.program_id(1)

        @pl.when(qi == 0)
        def _():
            dk_acc[...] = jnp.zeros_like(dk_acc)
            dv_acc[...] = jnp.zeros_like(dv_acc)

        q2 = q_ref[...] * scale
        do2 = do_ref[...]
        lo = _iota((tq, LANES), 1) < HEAD_DIM
        q_st = _stack_heads(q2, lo).astype(BF16)
        do_st = _stack_heads(do2, lo).astype(BF16)
        row, col = _iota((tk, tk), 0), _iota((tk, tk), 1)
        ustrict = (row > col).astype(BF16)
        earlier = (row < col).astype(BF16)
        q_pos, k_off = _attn_masks(qi, tq, tk)
        top = (qi + 1) * (tq // tk) - 1
        zero = jnp.zeros((2 * tq, 1), F32)

        def scan(carry):
            kb, _, r = carry
            kblk = k_ref[pl.ds(pl.multiple_of(kb * tk, tk), tk), :]
            r_hist[kb] = jnp.where(lo, r[:tq], r[tq:])
            z = _dot_nt(q_st, kblk)
            lm = jnp.where(kb * tk + k_off < q_pos, jnp.minimum(-z, 0.0) - jnp.log1p(jnp.exp(-jnp.abs(z))), 0.0)
            r = r + jnp.sum(lm, axis=1, keepdims=True)
            return kb - 1, (jnp.max(r) > EXP_UNDERFLOW).astype(jnp.int32), r

        first = lax.while_loop(lambda c: (c[0] >= 0) & (c[1] > 0), scan, (top, jnp.int32(1), zero))[0] + 1

        def step(carry):
            kb, p, dq = carry
            rows = pl.ds(pl.multiple_of(kb * tk, tk), tk)
            kblk, vblk = k_ref[rows, :], v_ref[rows, :]
            mask = kb * tk + k_off < q_pos
            rr = r_hist[kb]
            r = jnp.concatenate([_col(rr, 0), _col(rr, HEAD_DIM)], axis=0)
            ls, lm_raw, _, w = _attn_scores(q_st, kblk, mask, ustrict, r)
            ew = _dot_nt(do_st, vblk) * w
            dz = jnp.where(mask, ew * jnp.exp(lm_raw) - jnp.exp(ls) * (p + _dot3_stacked(ew, earlier)), 0.0).astype(BF16)
            both = _dot(dz, kblk)
            dq = dq + jnp.where(lo, both[:tq], both[tq:])
            dk_acc[rows, :] += _dot_tn(dz, q_st)
            dv_acc[rows, :] += _dot_tn(w, do_st)
            return kb + 1, p + jnp.sum(ew, axis=1, keepdims=True), dq

        dq = lax.while_loop(lambda c: c[0] <= top, step, (first, zero, jnp.zeros((tq, LANES), F32)))[2]
        dq_ref[...] = (dq * scale).astype(dq_ref.dtype)

        @pl.when(qi == pl.num_programs(1) - 1)
        def _():
            dk_ref[...] = dk_acc[...].astype(dk_ref.dtype)
            dv_ref[...] = dv_acc[...].astype(dv_ref.dtype)

    q, k, v = _attn_specs(tq, s_len)
    blk = pl.BlockSpec((tq, LANES), lambda p, i: (i, p))
    full = pl.BlockSpec((s_len, LANES), lambda p, i: (0, p))
    shape = jax.ShapeDtypeStruct((s_len, n_slabs * LANES), BF16)
    return pl.pallas_call(
        body, name=name, grid=(n_slabs, s_len // tq),
        in_specs=[q, k, v, pl.BlockSpec((tq, LANES), lambda p, i: (i, 1536 // LANES + p))],
        out_specs=[blk, full, full], out_shape=[shape, shape, shape],
        scratch_shapes=[pltpu.VMEM((s_len, LANES), F32), pltpu.VMEM((s_len, LANES), F32),
                        pltpu.VMEM((s_len // tk, tq, LANES), F32)],
        compiler_params=_cparams(("parallel", "arbitrary")),
    )(proj, kv, kv, dmixed)


def _ident(accs, _):
    return accs


def _mixer_fwd(tag, h, p, hooks=_no_hooks):
    u = _rms_fwd(tag + "_norm", h, p["mix_norm"])
    (proj,) = _mm(tag + "_in", [u], [p["w_main"]], nt=False, epilogue=_ident, out_dtypes=[F32], hosted=hooks("in"),
                  tk=2048)
    (dtr,) = _mm(tag + "_indt", [u], [p["w_dt"]], nt=False, epilogue=_ident, out_dtypes=[F32], tk=2048)
    pool_out = _pool_fwd(tag + "_pool", proj, p["pool_w"], p["pool_scale"])
    xbc = _conv_fwd(tag + "_conv", proj, p["conv_w"], p["conv_b"])
    y, states = _ssd_fwd(tag + "_ssd", xbc, dtr, p["dt_bias"], p["a_log"], p["d_skip"])
    ssd_out = _gatenorm_fwd(tag + "_gate", y, proj, p["ssd_norm"])
    kv = _attn_kv(proj)
    attn = _attn_fwd(tag + "_attn", proj, kv)
    mixed = jnp.concatenate([pool_out, ssd_out, attn.astype(BF16)], axis=1)
    (h2,) = _mm(tag + "_out", [mixed], [p["w_out"]], nt=False, extras=[h], epilogue=lambda accs, ex: [ex[0] + accs[0]],
                out_dtypes=[F32], hosted=hooks("out"), tk=2048)
    return h2, (h, u, proj, dtr, xbc, y, states, mixed, kv)


def _mixer_bwd(tag, saved, p, dh2, dh2_bf, hooks=_no_hooks, group=None):
    h, u, proj, dtr, xbc, y, states, mixed, kv = saved
    (dmixed,) = _mm(tag + "_dmix", [dh2_bf], [p["w_out"]], nt=True, epilogue=_ident, out_dtypes=[F32],
                    hosted=hooks("dmix"), tk=2048)
    (dw_out,) = _mm(tag + "_dwout", [mixed], [dh2_bf], nt=False, ta=True, epilogue=_ident, out_dtypes=[F32],
                    hosted=hooks("dwout"), tm=2048, tn=1024, tk=1024)
    dpool_in, dpool_w, dpool_scale = _pool_bwd(tag + "_dpool", proj, dmixed, p["pool_w"], p["pool_scale"])
    dy, dz, dssd_norm = _gatenorm_bwd(tag + "_dgate", y, proj, p["ssd_norm"], dmixed)
    dxs, dbm, dcm, ddtr, ddt_bias, da_log, dd_skip = _ssd_bwd(tag + "_dssd", xbc, dtr, states, dy, p["dt_bias"],
                                                             p["a_log"], p["d_skip"])
    dxbc, dconv_w, dconv_b = _conv_bwd(tag + "_dconv", proj, jnp.concatenate([dxs, dbm, dcm], axis=1), p["conv_w"],
                                       p["conv_b"])
    dq, dk, dv = _attn_bwd(tag + "_dattn", proj, kv, dmixed)
    dproj = jnp.concatenate([dpool_in, dz, dxbc, dq, dk, dv], axis=1)
    ddtr_bf = ddtr.astype(BF16)
    (dw_main,) = _mm(tag + "_dwin", [u], [dproj], nt=False, ta=True, epilogue=_ident, out_dtypes=[F32],
                     hosted=hooks("dwin"), tm=2048, tn=512, tk=1024)
    (dw_dt,) = _mm(tag + "_dwdt", [u], [ddtr_bf], nt=False, ta=True, epilogue=_ident, out_dtypes=[F32])
    dw_in = jnp.concatenate([dw_main[:, :REF_DT], dw_dt[:, :SSD_HEADS], dw_main[:, REF_DT:]], axis=1)
    dw_in = dw_in.reshape(dw_in.shape[0], N_CHIPS, dw_in.shape[1] // N_CHIPS).transpose(1, 0, 2)
    mine = group(dw_in, dw_out) if group else (dw_in, dw_out)
    (du_dt,) = _mm(tag + "_dudt", [ddtr_bf], [p["w_dt"]], nt=True, epilogue=_ident, out_dtypes=[F32], tn=1024)
    (du,) = _mm(tag + "_du", [dproj], [p["w_main"]], nt=True, extras=[du_dt], epilogue=lambda accs, ex: [accs[0] + ex[0]],
                out_dtypes=[F32], hosted=hooks("du") + ([mine.swap()] if group else []), tm=1024, tn=1024, tk=1536)
    dh, dh_bf, dg = _rms_bwd(tag + "_dnorm", h, du, dh2, p["mix_norm"])
    grads = dict(mix_norm=dg, pool_w=dpool_w, pool_scale=dpool_scale, conv_w=dconv_w, conv_b=dconv_b, dt_bias=ddt_bias,
                 a_log=da_log, d_skip=dd_skip, ssd_norm=dssd_norm)
    return dh, dh_bf, grads, mine


def _axes():
    return lax.axis_index("x"), lax.axis_index("y"), lax.axis_index("c")


def _any_specs(n):
    return [pl.BlockSpec(memory_space=pl.ANY) for _ in range(n)]


def _remote(src, dst, send, recv, k, dev):
    return pltpu.make_async_remote_copy(src_ref=src, dst_ref=dst, send_sem=send.at[k], recv_sem=recv.at[k],
                                        device_id=dev, device_id_type=MESH_ID)


def _chip_peers(x, y):
    return [(1 - x, y), (x, 1 - y), (1 - x, 1 - y)]


def _gather_comm(shards, done):
    n = len(shards)

    def make(in_refs, out_refs, send, recv, loc, r0, l0):
        x, y, c = _axes()
        me = 2 * x + y
        local = [pltpu.make_async_copy(in_refs[a], out_refs[a].at[me], loc.at[l0 + a]) for a in range(n)]
        sent = [_remote(in_refs[a], out_refs[a].at[me], send, recv, r0 + 3 * a + k, (px, py, c))
                for a in range(n) for k, (px, py) in enumerate(_chip_peers(x, y))]
        got = [_remote(in_refs[a], out_refs[a].at[2 * px + py], send, recv, r0 + 3 * a + k, (px, py, c))
               for a in range(n) for k, (px, py) in enumerate(_chip_peers(x, y))]
        return local + sent, [g.wait_recv for g in got] + [s.wait_send for s in sent] + [l.wait for l in local]

    return _Comm(shards, [jax.ShapeDtypeStruct((N_CHIPS,) + s.shape, s.dtype) for s in shards], 3 * n, n, make, done)


def _swap_comm(arrs, kinds, outs, done):
    n = len(arrs)

    def make(in_refs, out_refs, send, recv, loc, r0, l0):
        x, y, c = _axes()
        cps = [_remote(in_refs[a] if kinds[a] is None else _half(kinds[a], in_refs[a], 1 - c), out_refs[a], send, recv,
                       r0 + a, (x, y, 1 - c)) for a in range(n)]
        return cps, [cp.wait for cp in cps]

    return _Comm(arrs, outs, n, 0, make, done)


def _chips_comm(wires, kinds, done):
    n = len(wires)

    def make(in_refs, out_refs, send, recv, loc, r0, l0):
        x, y, c = _axes()
        cps = [_remote(_wire_shard(kinds[a], in_refs[a], 2 * px + py), out_refs[a].at[k], send, recv, r0 + 3 * a + k,
                       (px, py, c)) for a in range(n) for k, (px, py) in enumerate(_chip_peers(x, y))]
        return cps, [cp.wait for cp in cps]

    outs = [jax.ShapeDtypeStruct((3,) + _wire_shard_shape(k, w.shape), w.dtype) for k, w in zip(kinds, wires)]
    return _Comm(wires, outs, 3 * n, 0, make, done)


def _run_comm(name, hosted):
    ins, outs, sems, build, deliver = _comm_plan(hosted)

    def body(*refs):
        starts, waits = build(refs[:len(ins)], refs[len(ins):len(ins) + len(outs)], *refs[len(ins) + len(outs):])
        for cp in starts:
            cp.start()
        for wait in waits:
            wait()

    deliver(pl.pallas_call(body, name=name, in_specs=_any_specs(len(ins)), out_specs=_any_specs(len(outs)),
                           out_shape=outs, scratch_shapes=sems)(*ins))


def _half(kind, ref, hc):
    if kind == "col":
        r = ref.shape[0] // 2
        return ref.at[pl.ds(pl.multiple_of(hc * r, 16), r), :]
    if kind == "row":
        w = ref.shape[1] // 2
        return ref.at[:, pl.ds(pl.multiple_of(hc * w, LANES), w)]
    r = ref.shape[1] // 2
    return ref.at[:, pl.ds(pl.multiple_of(hc * r, 16), r), :]


def _half_shape(kind, s):
    return {"col": (s[0] // 2, s[1]), "row": (s[0], s[1] // 2), "win": (s[0], s[1] // 2) + tuple(s[2:])}[kind]


def _wire_shape(kind, hs):
    return (N_CHIPS, hs[0], hs[1] // N_CHIPS) if kind == "col" else tuple(hs)


def _wire_shard(kind, ref, j):
    if kind == "row":
        r = ref.shape[0] // N_CHIPS
        return ref.at[pl.ds(pl.multiple_of(j * r, 16), r), :]
    return ref.at[j]


def _wire_shard_shape(kind, ws):
    return (ws[0] // N_CHIPS, ws[1]) if kind == "row" else tuple(ws[1:])


class _GradGroup:
    def __init__(self, tag, names, kinds, arrs, where, south):
        self.tag, self.names, self.kinds, self.arrs, self.where, self.south = tag, list(names), list(kinds), list(arrs), where, south
        self.h32, self.wire, self.final = None, None, None
        self.from_chips = [None] * len(arrs)

    def swap(self):
        outs = [jax.ShapeDtypeStruct(_half_shape(k, g.shape), g.dtype) for k, g in zip(self.kinds, self.arrs)]

        def done(from_sibling):
            sums = [_sum_pair(f"{self.tag}_{n}_pair", k, g, r, self.where)
                    for n, k, g, r in zip(self.names, self.kinds, self.arrs, from_sibling)]
            self.h32, self.wire = [s[0] for s in sums], [s[1] for s in sums]

        return _swap_comm(self.arrs, self.kinds, outs, done)

    def chips(self, which):
        def done(got):
            for a, r in zip(which, got):
                self.from_chips[a] = r

        return _chips_comm([self.wire[a] for a in which], [self.kinds[a] for a in which], done)

    def share(self):
        parts = [_sum_chips(f"{self.tag}_{n}_chips", k, h, r, self.where)
                 for n, k, h, r in zip(self.names, self.kinds, self.h32, self.from_chips)]

        def done(theirs):
            axis = lambda k: 1 if k == "row" else 0
            self.final = [jnp.concatenate([jnp.where(self.south, mine, other), jnp.where(self.south, other, mine)], axis=axis(k))
                          for k, mine, other in zip(self.kinds, parts, theirs)]

        return _swap_comm(parts, [None] * len(parts), [jax.ShapeDtypeStruct(p.shape, p.dtype) for p in parts], done)

    def grads(self):
        return dict(zip(self.names, self.final))


def _esum(name, grid, block, ins, outs, where):
    n_in = len(ins)

    def body(s_ref, *refs):
        tot = refs[0][...].astype(F32)
        for r in refs[1:n_in]:
            tot = tot + r[...].astype(F32)
        for o in refs[n_in:]:
            o[...] = tot.astype(o.dtype)

    spec = lambda nd, imap: pl.BlockSpec((None,) * (nd - 2) + tuple(block), imap)
    return pl.pallas_call(
        body, name=name,
        grid_spec=pltpu.PrefetchScalarGridSpec(
            num_scalar_prefetch=1, grid=grid,
            in_specs=[spec(a.ndim, m) for a, m in ins], out_specs=[spec(len(s), m) for s, _, m in outs]),
        out_shape=[jax.ShapeDtypeStruct(s, dt) for s, dt, _ in outs],
        compiler_params=_cparams(("parallel",) * len(grid)),
    )(where, *[a for a, _ in ins])


def _sum_pair(name, kind, g, r1, where):
    hshape = _half_shape(kind, g.shape)
    wshape = _wire_shape(kind, hshape)
    if kind == "col":
        block = (_tile(hshape[0], 512, 16), hshape[1] // N_CHIPS)
        nb = hshape[0] // block[0]
        grid = (nb, N_CHIPS)
        gmap = lambda i, j, s: (s[0] * nb + i, j)
        hmap = lambda i, j, s: (i, j)
        wmap = lambda i, j, s: (j, i, 0)
    elif kind == "row":
        block = (_tile(hshape[0], 512, 16), hshape[1])
        grid = (hshape[0] // block[0],)
        gmap = lambda i, s: (i, s[0])
        hmap = wmap = lambda i, s: (i, 0)
    else:
        block = (_tile(hshape[1], 512, 16), hshape[2])
        nb = hshape[1] // block[0]
        grid = (hshape[0], nb)
        gmap = lambda q, i, s: (q, s[0] * nb + i, 0)
        hmap = wmap = lambda q, i, s: (q, i, 0)
    return _esum(name, grid, block, [(g, gmap), (r1, hmap)], [(hshape, F32, hmap), (wshape, BF16, wmap)], where)


def _sum_chips(name, kind, h32, r2, where):
    tshape = tuple(r2.shape[1:])
    block = (_tile(tshape[0], 512, 16), tshape[1])
    nb = tshape[0] // block[0]
    if kind == "col":
        hmap = lambda i, s: (i, s[1])
    elif kind == "row":
        hmap = lambda i, s: (s[1] * nb + i, 0)
    else:
        hmap = lambda i, s: (s[1], i, 0)
    rmap = lambda k: (lambda i, s: (k, i, 0))
    return _esum(name, (nb,), block, [(h32, hmap)] + [(r2, rmap(k)) for k in range(3)],
                 [(tshape, F32, lambda i, s: (i, 0))], where)[0]


def _allreduce_small(name, vec):
    rows_n = vec.shape[0]

    def body(x_ref, sum_ref, all_ref, send, recv, local_sem):
        x, y, c = _axes()
        me, sibling = (x, y, c), (x, y, 1 - c)
        chips = [(1 - x, y), (x, 1 - y), (1 - x, 1 - y)]

        def rows(px, py, pc):
            return all_ref.at[pl.ds(pl.multiple_of((4 * px + 2 * py + pc) * rows_n, 8), rows_n), :]

        def copy(k, block, to, src=None):
            return _remote(rows(*block) if src is None else src, rows(*block), send, recv, k, to)

        mine = pltpu.make_async_copy(x_ref, rows(*me), local_sem)
        mine.start()
        first = [copy(0, me, sibling, src=x_ref)] + [copy(1 + j, me, (*chip, c), src=x_ref) for j, chip in enumerate(chips)]
        for cp in first:
            cp.start()
        passed = [copy(4 + j, (*chip, c), sibling) for j, chip in enumerate(chips)]
        for j, chip in enumerate(chips):
            copy(1 + j, (*chip, c), me).wait_recv()
            passed[j].start()
        copy(0, sibling, me).wait_recv()
        for j, chip in enumerate(chips):
            copy(4 + j, (*chip, 1 - c), me).wait_recv()
        for cp in first + passed:
            cp.wait_send()
        mine.wait()
        tot = all_ref[0:rows_n, :]
        for d in range(1, 8):
            tot = tot + all_ref[d * rows_n:(d + 1) * rows_n, :]
        sum_ref[...] = tot

    vm = pl.BlockSpec(memory_space=pltpu.VMEM)
    return pl.pallas_call(
        body, name=name, in_specs=[vm], out_specs=[vm, vm],
        out_shape=[jax.ShapeDtypeStruct(vec.shape, F32), jax.ShapeDtypeStruct((8 * rows_n, LANES), F32)],
        scratch_shapes=[pltpu.SemaphoreType.DMA((7,)), pltpu.SemaphoreType.DMA((7,)), pltpu.SemaphoreType.DMA],
        compiler_params=pltpu.CompilerParams(vmem_limit_bytes=VMEM_LIMIT),
    )(vec)[0]


def _adamw(name, w, g, m, v):
    shape = w.shape
    rows_n, cols = shape[-2], shape[-1]
    lead = math.prod(shape[:-2])
    tr = _tile(rows_n, 256, 8)

    def body(w_ref, g_ref, m_ref, v_ref, d_ref, m2_ref, v2_ref):
        gv = g_ref[...]
        m2 = ADAM_B1 * m_ref[...] + (1.0 - ADAM_B1) * gv
        v2 = ADAM_B2 * v_ref[...] + (1.0 - ADAM_B2) * jnp.square(gv)
        m_hat = m2 / (1.0 - ADAM_B1 ** ADAM_STEP)
        v_hat = v2 / (1.0 - ADAM_B2 ** ADAM_STEP)
        d_ref[...] = -ADAM_LR * (m_hat / (jnp.sqrt(v_hat) + ADAM_EPS) + ADAM_WD * w_ref[...])
        m2_ref[...] = m2
        v2_ref[...] = v2

    spec = pl.BlockSpec((None, tr, cols), lambda l, i: (l, i, 0))
    flat = (lead, rows_n, cols)
    outs = pl.pallas_call(
        body, name=name, grid=(lead, rows_n // tr), in_specs=[spec] * 4, out_specs=[spec] * 3,
        out_shape=[jax.ShapeDtypeStruct(flat, F32)] * 3,
        compiler_params=_cparams(("parallel", "parallel")),
    )(*[t.reshape(flat) for t in (w, g, m, v)])
    return [o.reshape(shape) for o in outs]


WEIGHTS = ("ffn1_norm", "ffn1_w_gate", "ffn1_w_up", "ffn1_w_down", "mix_norm", "w_in", "pool_w", "pool_scale", "conv_w",
           "conv_b", "dt_bias", "a_log", "d_skip", "ssd_norm", "w_out", "ffn2_norm", "ffn2_w_gate", "ffn2_w_up",
           "ffn2_w_down", "final_norm")
BIG = {"ffn1_w_gate": "col", "ffn1_w_up": "col", "ffn1_w_down": "row", "w_in": "win", "w_out": "row",
       "ffn2_w_gate": "col", "ffn2_w_up": "col", "ffn2_w_down": "row"}
SMALL = tuple(n for n in WEIGHTS if n not in BIG and n != "conv_w")
REF_DT = 3072


def _pack(parts):
    flat = jnp.concatenate([p.reshape(-1) for p in parts])
    rows_n = -(-flat.shape[0] // (8 * LANES)) * 8
    return jnp.pad(flat, (0, rows_n * LANES - flat.shape[0])).reshape(rows_n, LANES)


def _unpack(block, shapes):
    flat, out, at = block.reshape(-1), [], 0
    for s in shapes:
        n = math.prod(s)
        out.append(flat[at:at + n].reshape(s))
        at += n
    return out


def _train_step(a):
    depth = a["ffn1_norm"].shape[0]
    x_id, y_id, c_id = _axes()
    chip = 2 * x_id + y_id
    where = jnp.stack([c_id, chip]).astype(jnp.int32)

    big = list(BIG)
    south = c_id == 0
    full = [dict() for _ in range(depth)]
    conv_full = []

    def fetch(l, names):
        def done(blocks):
            for n, g in zip(names, blocks):
                full[l][n] = jnp.concatenate([g[s] for s in range(N_CHIPS)], axis=0 if BIG[n] == "row" else 1)

        return [_gather_comm([a[n][l].astype(BF16) for n in names], done)] if l < depth else []

    ffn1, mix, ffn2 = ["ffn1_w_gate", "ffn1_w_up", "ffn1_w_down"], ["w_in", "w_out"], ["ffn2_w_gate", "ffn2_w_up", "ffn2_w_down"]
    conv_done = lambda blocks: conv_full.append(jnp.concatenate([blocks[0][s] for s in range(N_CHIPS)], axis=2))
    _run_comm("gather_first", fetch(0, ffn1) + [_gather_comm([a["conv_w"]], conv_done)])
    conv_w = conv_full[0]
    heads128 = lambda v: jnp.pad(v, ((0, 0), (0, LANES - SSD_HEADS)))
    dt_bias, a_log, d_skip = heads128(a["dt_bias"]), heads128(a["a_log"]), heads128(a["d_skip"])

    def mixer_params(l):
        w_in = full[l]["w_in"]
        w_main = jnp.concatenate([w_in[:, :REF_DT], w_in[:, REF_DT + SSD_HEADS:]], axis=1)
        w_dt = jnp.pad(w_in[:, REF_DT:REF_DT + SSD_HEADS], ((0, 0), (0, LANES - SSD_HEADS)))
        return dict(mix_norm=a["mix_norm"][l][None], w_main=w_main, w_dt=w_dt, pool_w=a["pool_w"][l],
                    pool_scale=a["pool_scale"][l][None], conv_w=conv_w[l], conv_b=a["conv_b"][l][None],
                    dt_bias=dt_bias[l][None], a_log=a_log[l][None], d_skip=d_skip[l][None],
                    ssd_norm=a["ssd_norm"][l][None], w_out=full[l]["w_out"])

    def ffn_params(l, which):
        return (a[which + "_norm"][l][None], full[l][which + "_w_gate"], full[l][which + "_w_up"], full[l][which + "_w_down"])

    h = a["x"][0]
    saved, mixer_p = [], []
    for l in range(depth):
        carry = lambda plan: (lambda call: fetch(*plan[call]) if call in plan else [])
        h, s1 = _ffn_fwd(f"l{l}_ffn1", h, *ffn_params(l, "ffn1"),
                         hooks=carry({"up": (l, mix + ffn2[:1]), "down": (l, ffn2[1:2])}))
        mixer_p.append(mixer_params(l))
        h, sm = _mixer_fwd(f"l{l}_mix", h, mixer_p[l], hooks=carry({"in": (l, ffn2[2:])}))
        h, s2 = _ffn_fwd(f"l{l}_ffn2", h, *ffn_params(l, "ffn2"),
                         hooks=carry({"up": (l + 1, ffn1[:2]), "down": (l + 1, ffn1[2:])}))
        saved.append((s1, sm, s2))
    loss_part, dh, dh_bf, dfinal = _loss_head("loss_head", h, a["final_norm"][None], a["loss_target"][0])

    small = {n: [None] * depth for n in SMALL if n != "final_norm"}
    small["conv_w"] = [None] * depth
    groups = []

    def grouper(tag, names):
        def make(*arrs):
            groups.append(_GradGroup(tag, names, [BIG[n] for n in names], arrs, where, south))
            return groups[-1]
        return make

    above = None
    for l in reversed(range(depth)):
        s1, sm, s2 = saved[l]
        hooks = _no_hooks if above is None else (lambda call, g=above: {"dhm": lambda: [g.chips([0, 1])],
                                                                        "dwd": lambda: [g.chips([2])],
                                                                        "dwgu": lambda: [g.share()]}.get(call, list)())
        dh, dh_bf, small["ffn2_norm"][l], f2 = _ffn_bwd(f"l{l}_ffn2", s2, *ffn_params(l, "ffn2"), dh, dh_bf, hooks=hooks,
                                                        group=grouper(f"l{l}_ffn2", ffn2))
        hooks = lambda call, g=f2: {"dmix": lambda: [g.chips([1])], "dwout": lambda: [g.chips([0])],
                                    "dwin": lambda: [g.chips([2])], "du": lambda: [g.share()]}.get(call, list)()
        dh, dh_bf, g, mx = _mixer_bwd(f"l{l}_mix", sm, mixer_p[l], dh, dh_bf, hooks=hooks, group=grouper(f"l{l}_mix", mix))
        for n in ("mix_norm", "pool_w", "pool_scale", "conv_w", "conv_b", "ssd_norm"):
            small[n][l] = g[n]
        for n in ("dt_bias", "a_log", "d_skip"):
            small[n][l] = g[n][:, :SSD_HEADS]
        hooks = lambda call, g=mx: {"dhm": lambda: [g.chips([0, 1])], "dwd": lambda: [g.share()]}.get(call, list)()
        dh, dh_bf, small["ffn1_norm"][l], above = _ffn_bwd(f"l{l}_ffn1", s1, *ffn_params(l, "ffn1"), dh, dh_bf, hooks=hooks,
                                                           group=grouper(f"l{l}_ffn1", ffn1))
    grad_x = dh[None]
    _run_comm("tail_chips", [above.chips([0, 1, 2])])
    _run_comm("tail_share", [above.share()])
    per_layer = {}
    for grp in groups:
        per_layer.update({(grp.tag, n): g for n, g in grp.grads().items()})
    grad = {n: jnp.stack([per_layer[(f"l{l}_{'mix' if n in mix else n[:4]}", n)] for l in range(depth)]) for n in big}

    small_full = {n: jnp.stack([t.reshape(a[n].shape[1:]) for t in small[n]]) for n in SMALL if n != "final_norm"}
    small_full["final_norm"] = dfinal.reshape(a["final_norm"].shape)
    conv_full = jnp.stack(small["conv_w"])
    shapes = [a[n].shape for n in SMALL] + [conv_full.shape]
    reduced = _unpack(_allreduce_small("allreduce_small", _pack([small_full[n] for n in SMALL] + [conv_full])), shapes)
    grad.update(zip(SMALL, reduced[:-1]))
    shard = a["conv_w"].shape[2]
    grad["conv_w"] = lax.dynamic_slice_in_dim(reduced[-1], chip * shard, shard, axis=2)

    delta, new_m, new_v = {}, {}, {}
    for n in big + ["conv_w"]:
        delta[n], new_m[n], new_v[n] = _adamw(f"adamw_{n}", a[n], grad[n], a["m_" + n], a["v_" + n])
    packed = [_pack([a[pre + n] for n in SMALL]) for pre in ("", "m_", "v_")]
    outs = _adamw("adamw_small", packed[0], _pack([grad[n] for n in SMALL]), packed[1], packed[2])
    for store, block in zip((delta, new_m, new_v), outs):
        store.update(zip(SMALL, _unpack(block, [a[n].shape for n in SMALL])))

    loss = lax.psum(loss_part[0, 0], ("x", "y", "c"))
    return (loss, grad_x, *[grad[n] for n in WEIGHTS], *[delta[n] for n in WEIGHTS], *[new_m[n] for n in WEIGHTS],
            *[new_v[n] for n in WEIGHTS])


def kernel(x, ffn1_norm, ffn1_w_gate, ffn1_w_up, ffn1_w_down, mix_norm, w_in, pool_w, pool_scale, conv_w, conv_b, dt_bias, a_log, d_skip, ssd_norm, w_out, ffn2_norm, ffn2_w_gate, ffn2_w_up, ffn2_w_down, final_norm, loss_target, m_ffn1_norm, m_ffn1_w_gate, m_ffn1_w_up, m_ffn1_w_down, m_mix_norm, m_w_in, m_pool_w, m_pool_scale, m_conv_w, m_conv_b, m_dt_bias, m_a_log, m_d_skip, m_ssd_norm, m_w_out, m_ffn2_norm, m_ffn2_w_gate, m_ffn2_w_up, m_ffn2_w_down, m_final_norm, v_ffn1_norm, v_ffn1_w_gate, v_ffn1_w_up, v_ffn1_w_down, v_mix_norm, v_w_in, v_pool_w, v_pool_scale, v_conv_w, v_conv_b, v_dt_bias, v_a_log, v_d_skip, v_ssd_norm, v_w_out, v_ffn2_norm, v_ffn2_w_gate, v_ffn2_w_up, v_ffn2_w_down, v_final_norm):
    return _train_step(dict(locals()))
```

```python
import math

import jax
import jax.numpy as jnp
from jax import lax
from jax.experimental import pallas as pl
from jax.experimental.pallas import tpu as pltpu

F32 = jnp.float32
BF16 = jnp.bfloat16
MESH_ID = pl.DeviceIdType.MESH

RMS_EPS = 1e-6
POOL_WINDOWS = (2, 4, 8, 16)
LANES = 128
HEAD_DIM = 64
SSD_HEADS = 16
SSD_CHUNK = 256
ATTN_BLOCK = 128
HALO = 16
EXP_UNDERFLOW = -105.0
VMEM_LIMIT = 56 * 1024 * 1024
MM_SUB = 256
N_CHIPS = 4

ADAM_LR = 0.001
ADAM_B1 = 0.9
ADAM_B2 = 0.999
ADAM_EPS = 1e-08
ADAM_WD = 0.01
ADAM_STEP = 10

C_POOL, C_Z, C_XBC, C_Q, C_K, C_V, C_END = 0, 512, 1536, 3072, 3584, 4096, 4608


def _cparams(sem):
    return pltpu.CompilerParams(dimension_semantics=sem, vmem_limit_bytes=VMEM_LIMIT)


def _tile(dim, pref, unit=LANES):
    if dim <= pref:
        return dim
    t = (pref // unit) * unit
    while t > unit and dim % t:
        t -= unit
    assert dim % t == 0, (dim, pref)
    return t


def _sigmoid(x):
    return 0.5 * jnp.tanh(0.5 * x) + 0.5


def _silu(x):
    return x * _sigmoid(x)


def _dsilu(x):
    s = _sigmoid(x)
    return s * (1.0 + x * (1.0 - s))


def _dot(a, b):
    return jnp.dot(a.astype(BF16), b.astype(BF16), preferred_element_type=F32)


def _dot_nt(a, b):
    return lax.dot_general(a.astype(BF16), b.astype(BF16), (((1,), (1,)), ((), ())), preferred_element_type=F32)


def _dot_tn(a, b):
    return lax.dot_general(a.astype(BF16), b.astype(BF16), (((0,), (0,)), ((), ())), preferred_element_type=F32)


def _split3(x):
    hi = x.astype(BF16)
    r = x - hi.astype(F32)
    mid = r.astype(BF16)
    lo = (r - mid.astype(F32)).astype(BF16)
    return hi, mid, lo


def _dot3_stacked(x, m):
    n = x.shape[0]
    y = jnp.dot(jnp.concatenate(_split3(x), axis=0), m, preferred_element_type=F32)
    return y[:n] + y[n:2 * n] + y[2 * n:]


def _dot3_left(m, x):
    hi, mid, lo = _split3(x)
    dn = (((1,), (0,)), ((), ()))
    f = lambda p: lax.dot_general(m, p, dn, preferred_element_type=F32)
    return f(hi) + f(mid) + f(lo)


def _iota(shape, axis):
    return lax.broadcasted_iota(jnp.int32, shape, axis)


def _col(x, h):
    return jnp.sum(jnp.where(_iota(x.shape, 1) == h, x, 0.0), axis=1, keepdims=True)


def _roll_down(x, k):
    return x if k == 0 else pltpu.roll(x, k, 0)


def _roll_up(x, k):
    return x if k == 0 else pltpu.roll(x, x.shape[0] - k, 0)


class _Comm:
    def __init__(self, ins, outs, n_remote, n_local, make, done):
        self.ins, self.outs, self.n_remote, self.n_local, self.make, self.done = list(ins), list(outs), n_remote, n_local, make, done


def _comm_plan(hosted):
    ins = [a for cm in hosted for a in cm.ins]
    outs = [o for cm in hosted for o in cm.outs]
    n_remote = sum(cm.n_remote for cm in hosted)
    n_local = sum(cm.n_local for cm in hosted)

    def build(in_refs, out_refs, send, recv, loc):
        starts, waits, i0, o0, r0, l0 = [], [], 0, 0, 0, 0
        for cm in hosted:
            s, w = cm.make(in_refs[i0:i0 + len(cm.ins)], out_refs[o0:o0 + len(cm.outs)], send, recv, loc, r0, l0)
            starts, waits = starts + s, waits + w
            i0, o0, r0, l0 = i0 + len(cm.ins), o0 + len(cm.outs), r0 + cm.n_remote, l0 + cm.n_local
        return starts, waits

    def deliver(results):
        o0 = 0
        for cm in hosted:
            cm.done(results[o0:o0 + len(cm.outs)])
            o0 += len(cm.outs)

    sems = [pltpu.SemaphoreType.DMA((max(n_remote, 1),)), pltpu.SemaphoreType.DMA((max(n_remote, 1),)),
            pltpu.SemaphoreType.DMA((max(n_local, 1),))]
    return ins, outs, sems, build, deliver


def _mm(name, a_list, b_list, *, nt, epilogue, out_dtypes, acc_of=None, extras=(), hosted=(), ta=False,
        tm=1024, tn=512, tk=1024):
    n_pairs = len(a_list)
    acc_of = list(acc_of) if acc_of is not None else [0] * n_pairs
    n_acc = max(acc_of) + 1
    m_dim, k_dim = a_list[0].shape[::-1] if ta else a_list[0].shape
    n_dim = b_list[0].shape[0] if nt else b_list[0].shape[1]
    tm, tn, tk = _tile(m_dim, tm, 8), _tile(n_dim, tn), _tile(k_dim, tk)
    nk = k_dim // tk
    n_ex, n_out = len(extras), len(out_dtypes)
    hosted = list(hosted)
    c_ins, c_outs, c_sems, c_build, c_deliver = _comm_plan(hosted)
    n_ci, n_co = len(c_ins), len(c_outs)
    n_scr = 0 if nk == 1 else n_acc
    grid = (m_dim // tm, n_dim // tn, nk)
    sub = MM_SUB if (nk == 1 and tn > MM_SUB and tn % MM_SUB == 0) else tn

    def body(*refs):
        a_refs = refs[:n_pairs]
        b_refs = refs[n_pairs:2 * n_pairs]
        e_refs = refs[2 * n_pairs:2 * n_pairs + n_ex]
        ci_refs = refs[2 * n_pairs + n_ex:2 * n_pairs + n_ex + n_ci]
        first_out = 2 * n_pairs + n_ex + n_ci
        o_refs = refs[first_out:first_out + n_out]
        co_refs = refs[first_out + n_out:first_out + n_out + n_co]
        acc_refs = refs[first_out + n_out + n_co:first_out + n_out + n_co + n_scr]
        sems = refs[first_out + n_out + n_co + n_scr:]
        if hosted:
            at = [pl.program_id(d) for d in range(3)]
            starts, waits = c_build(ci_refs, co_refs, *sems)

            @pl.when((at[0] == 0) & (at[1] == 0) & (at[2] == 0))
            def _():
                for cp in starts:
                    cp.start()

        if nk == 1:
            for s in range(tn // sub):
                cs = slice(s * sub, (s + 1) * sub)
                accs = [None] * n_acc
                for p in range(n_pairs):
                    a = a_refs[p][...]
                    d = _dot_tn(a, b_refs[p][:, cs]) if ta else _dot_nt(a, b_refs[p][cs, :]) if nt else _dot(a, b_refs[p][:, cs])
                    accs[acc_of[p]] = d if accs[acc_of[p]] is None else accs[acc_of[p]] + d
                outs = epilogue(accs, [e[:, cs] for e in e_refs])
                for o_ref, o in zip(o_refs, outs):
                    o_ref[:, cs] = o.astype(o_ref.dtype)
        else:
            k = pl.program_id(2)

            @pl.when(k == 0)
            def _():
                for acc in acc_refs:
                    acc[...] = jnp.zeros_like(acc)

            for p in range(n_pairs):
                a, b = a_refs[p][...], b_refs[p][...]
                acc_refs[acc_of[p]][...] += _dot_tn(a, b) if ta else _dot_nt(a, b) if nt else _dot(a, b)

            @pl.when(k == nk - 1)
            def _():
                outs = epilogue([acc[...] for acc in acc_refs], [e[...] for e in e_refs])
                for o_ref, o in zip(o_refs, outs):
                    o_ref[...] = o.astype(o_ref.dtype)

        if hosted:
            @pl.when((at[0] == grid[0] - 1) & (at[1] == grid[1] - 1) & (at[2] == grid[2] - 1))
            def _():
                for wait in waits:
                    wait()

    a_spec = pl.BlockSpec((tk, tm), lambda i, j, k: (k, i)) if ta else pl.BlockSpec((tm, tk), lambda i, j, k: (i, k))
    b_spec = pl.BlockSpec((tn, tk), lambda i, j, k: (j, k)) if nt else pl.BlockSpec((tk, tn), lambda i, j, k: (k, j))
    t_spec = pl.BlockSpec((tm, tn), lambda i, j, k: (i, j))
    results = pl.pallas_call(
        body, name=name, grid=grid,
        in_specs=[a_spec] * n_pairs + [b_spec] * n_pairs + [t_spec] * n_ex + _any_specs(n_ci),
        out_specs=[t_spec] * n_out + _any_specs(n_co),
        out_shape=[jax.ShapeDtypeStruct((m_dim, n_dim), dt) for dt in out_dtypes] + c_outs,
        scratch_shapes=[pltpu.VMEM((tm, tn), F32)] * n_scr + (c_sems if hosted else []),
        compiler_params=_cparams(("arbitrary",) * 3 if hosted else ("parallel", "parallel", "arbitrary")),
    )(*a_list, *b_list, *extras, *c_ins)
    c_deliver(results[n_out:])
    return results[:n_out]


def _rms_fwd(name, h, g):
    s_len, d = h.shape
    ts = _tile(s_len, 512, 8)

    def body(h_ref, g_ref, u_ref):
        x = h_ref[...]
        rstd = lax.rsqrt(jnp.mean(x * x, axis=-1, keepdims=True) + RMS_EPS)
        u_ref[...] = (x * rstd * g_ref[...]).astype(BF16)

    return pl.pallas_call(
        body, name=name, grid=(s_len // ts,),
        in_specs=[pl.BlockSpec((ts, d), lambda i: (i, 0)), pl.BlockSpec((1, d), lambda i: (0, 0))],
        out_specs=pl.BlockSpec((ts, d), lambda i: (i, 0)),
        out_shape=jax.ShapeDtypeStruct((s_len, d), BF16),
        compiler_params=_cparams(("parallel",)),
    )(h, g)


def _rms_bwd(name, h, du, dres, g):
    s_len, d = h.shape
    ts = _tile(s_len, 256, 8)

    def body(h_ref, du_ref, dres_ref, g_ref, dh_ref, dhb_ref, dg_ref):
        x = h_ref[...]
        rstd = lax.rsqrt(jnp.mean(x * x, axis=-1, keepdims=True) + RMS_EPS)
        n = x * rstd
        dn = du_ref[...] * g_ref[...]
        dh = dres_ref[...] + rstd * (dn - n * jnp.mean(dn * n, axis=-1, keepdims=True))
        dh_ref[...] = dh
        dhb_ref[...] = dh.astype(BF16)

        @pl.when(pl.program_id(0) == 0)
        def _():
            dg_ref[...] = jnp.zeros_like(dg_ref)

        dg_ref[...] += jnp.sum(du_ref[...] * n, axis=0, keepdims=True)

    row = pl.BlockSpec((ts, d), lambda i: (i, 0))
    vec = pl.BlockSpec((1, d), lambda i: (0, 0))
    return pl.pallas_call(
        body, name=name, grid=(s_len // ts,),
        in_specs=[row, row, row, vec], out_specs=[row, row, vec],
        out_shape=[jax.ShapeDtypeStruct((s_len, d), F32), jax.ShapeDtypeStruct((s_len, d), BF16),
                   jax.ShapeDtypeStruct((1, d), F32)],
        compiler_params=_cparams(("arbitrary",)),
    )(h, du, dres, g)


def _loss_head(name, h, g, target):
    s_len, d = h.shape
    ts = _tile(s_len, 256, 8)

    def body(h_ref, g_ref, t_ref, loss_ref, dh_ref, dhb_ref, dg_ref):
        x = h_ref[...]
        rstd = lax.rsqrt(jnp.mean(x * x, axis=-1, keepdims=True) + RMS_EPS)
        n = x * rstd
        err = n * g_ref[...] - t_ref[...]
        dy = err * (1.0 / d)
        dn = dy * g_ref[...]
        dh = rstd * (dn - n * jnp.mean(dn * n, axis=-1, keepdims=True))
        dh_ref[...] = dh
        dhb_ref[...] = dh.astype(BF16)

        @pl.when(pl.program_id(0) == 0)
        def _():
            dg_ref[...] = jnp.zeros_like(dg_ref)
            loss_ref[...] = jnp.zeros_like(loss_ref)

        dg_ref[...] += jnp.sum(dy * n, axis=0, keepdims=True)
        part = jnp.sum(jnp.sum(err * err, axis=1, keepdims=True), axis=0, keepdims=True) * (0.5 / d)
        loss_ref[...] += jnp.broadcast_to(part, loss_ref.shape)

    row = pl.BlockSpec((ts, d), lambda i: (i, 0))
    vec = pl.BlockSpec((1, d), lambda i: (0, 0))
    lspec = pl.BlockSpec((1, LANES), lambda i: (0, 0))
    return pl.pallas_call(
        body, name=name, grid=(s_len // ts,),
        in_specs=[row, vec, row], out_specs=[lspec, row, row, vec],
        out_shape=[jax.ShapeDtypeStruct((1, LANES), F32), jax.ShapeDtypeStruct((s_len, d), F32),
                   jax.ShapeDtypeStruct((s_len, d), BF16), jax.ShapeDtypeStruct((1, d), F32)],
        compiler_params=_cparams(("arbitrary",)),
    )(h, g, target)


def _no_hooks(_):
    return []


def _ffn_fwd(tag, h, g, wg, wu, wd, hooks=_no_hooks):
    u = _rms_fwd(tag + "_norm", h, g)

    def up(accs, _):
        a, b = accs
        return a, b, _silu(a) * b

    a, b, hm = _mm(tag + "_up", [u, u], [wg, wu], nt=False, acc_of=[0, 1], epilogue=up, out_dtypes=[BF16, BF16, BF16],
                   hosted=hooks("up"))
    (h2,) = _mm(tag + "_down", [hm], [wd], nt=False, extras=[h], epilogue=lambda accs, ex: [ex[0] + 0.5 * accs[0]],
                out_dtypes=[F32], hosted=hooks("down"), tm=1024, tn=1024, tk=1408)
    return h2, (h, u, a, b, hm)


def _ffn_bwd(tag, saved, g, wg, wu, wd, dh2, dh2_bf, hooks=_no_hooks, group=None):
    h, u, a, b, hm = saved

    def dact(accs, ex):
        af, bf = ex[0].astype(F32), ex[1].astype(F32)
        dhm = 0.5 * accs[0]
        return dhm * bf * _dsilu(af), dhm * _silu(af)

    da, db = _mm(tag + "_dhm", [dh2_bf], [wd], nt=True, extras=[a, b], epilogue=dact, out_dtypes=[BF16, BF16],
                 hosted=hooks("dhm"), tk=2048)
    (dwd,) = _mm(tag + "_dwd", [hm], [dh2_bf], nt=False, ta=True, epilogue=lambda accs, _: [0.5 * accs[0]],
                 out_dtypes=[F32], hosted=hooks("dwd"), tm=1408, tn=1024, tk=512)
    dwg, dwu = _mm(tag + "_dwgu", [u, u], [da, db], nt=False, ta=True, acc_of=[0, 1], epilogue=lambda accs, _: accs,
                   out_dtypes=[F32, F32], hosted=hooks("dwgu"), tm=2048, tn=512, tk=1024)
    mine = group(dwg, dwu, dwd) if group else (dwg, dwu, dwd)
    (du,) = _mm(tag + "_du", [da, db], [wg, wu], nt=True, epilogue=lambda accs, _: accs, out_dtypes=[F32],
                hosted=[mine.swap()] if group else [], tm=1024, tn=1024, tk=1408)
    dh, dh_bf, dg = _rms_bwd(tag + "_dnorm", h, du, dh2, g)
    return dh, dh_bf, dg, mine


def _softplus(x):
    e = jnp.exp(-jnp.abs(x))
    u = 1.0 + e
    log1p_e = jnp.where(u == 1.0, e, jnp.log(u) * (e / jnp.where(u == 1.0, 1.0, u - 1.0)))
    return jnp.maximum(x, 0.0) + log1p_e


def _row_spec(ts, width, colblock):
    return pl.BlockSpec((ts, width), lambda i: (i, colblock))


def _halo_before_spec(ts, width, colblock):
    r = ts // HALO
    return pl.BlockSpec((HALO, width), lambda i: (jnp.maximum(i * r - 1, 0), colblock))


def _halo_after_spec(ts, width, colblock, s_len):
    r = ts // HALO
    return pl.BlockSpec((HALO, width), lambda i: (jnp.minimum((i + 1) * r, s_len // HALO - 1), colblock))


def _const_spec(shape):
    nd = len(shape)
    return pl.BlockSpec(shape, lambda *_: (0,) * nd)


def _window_sum(e, win, roll):
    s, sh = e, 1
    while sh < win:
        s = s + roll(s, sh)
        sh *= 2
    return s


def _pool_center(ext, x, t, gi, win):
    sl = slice(gi * LANES, (gi + 1) * LANES)
    s = _window_sum(ext[:, sl], win, _roll_down)
    cnt = jnp.minimum(t + 1, win).astype(F32)
    return s[HALO:] / cnt - x[:, sl]


def _pool_fwd(name, proj, pw, scale):
    s_len = proj.shape[0]
    ts = _tile(s_len, 512, 8)
    width = len(POOL_WINDOWS) * LANES

    def body(x_ref, hb_ref, pw_ref, sc_ref, o_ref):
        i = pl.program_id(0)
        x = x_ref[...]
        ext = jnp.concatenate([jnp.where(i == 0, 0.0, hb_ref[...]), x], axis=0)
        t = i * ts + _iota((ts, 1), 0)
        for gi, win in enumerate(POOL_WINDOWS):
            sl = slice(gi * LANES, (gi + 1) * LANES)
            c = _pool_center(ext, x, t, gi, win)
            o_ref[:, sl] = (_dot(c, pw_ref[gi]) * sc_ref[:, sl]).astype(o_ref.dtype)

    return pl.pallas_call(
        body, name=name, grid=(s_len // ts,),
        in_specs=[_row_spec(ts, width, 0), _halo_before_spec(ts, width, 0), _const_spec(pw.shape), _const_spec(scale.shape)],
        out_specs=_row_spec(ts, width, 0),
        out_shape=jax.ShapeDtypeStruct((s_len, width), BF16),
        compiler_params=_cparams(("parallel",)),
    )(proj, proj, pw, scale)


def _pool_bwd(name, proj, dmixed, pw, scale):
    s_len = proj.shape[0]
    ts = _tile(s_len, 512, 8)
    n_tiles = s_len // ts
    width = len(POOL_WINDOWS) * LANES

    def body(x_ref, hb_ref, d_ref, da_ref, pw_ref, sc_ref, dx_ref, dpw_ref, dsc_ref):
        i = pl.program_id(0)
        x = x_ref[...]
        ext = jnp.concatenate([jnp.where(i == 0, 0.0, hb_ref[...]), x], axis=0)
        dout = d_ref[...]
        dext = jnp.concatenate([dout, jnp.where(i == n_tiles - 1, 0.0, da_ref[...])], axis=0)
        t = i * ts + _iota((ts, 1), 0)
        te = i * ts + _iota((ts + HALO, 1), 0)

        @pl.when(i == 0)
        def _():
            dpw_ref[...] = jnp.zeros_like(dpw_ref)
            dsc_ref[...] = jnp.zeros_like(dsc_ref)

        for gi, win in enumerate(POOL_WINDOWS):
            sl = slice(gi * LANES, (gi + 1) * LANES)
            c = _pool_center(ext, x, t, gi, win)
            o = _dot(c, pw_ref[gi])
            dsc_ref[:, sl] += jnp.sum(dout[:, sl] * o, axis=0, keepdims=True)
            do_ext = dext[:, sl] * sc_ref[:, sl]
            dc = _dot_nt(do_ext, pw_ref[gi])
            e = dc / jnp.minimum(te + 1, win).astype(F32)
            back = _window_sum(e, win, _roll_up)
            dx_ref[:, sl] = (back[:ts] - dc[:ts]).astype(dx_ref.dtype)
            dpw_ref[gi] += _dot(c.T, do_ext[:ts])

    return pl.pallas_call(
        body, name=name, grid=(n_tiles,),
        in_specs=[_row_spec(ts, width, 0), _halo_before_spec(ts, width, 0), _row_spec(ts, width, 0),
                  _halo_after_spec(ts, width, 0, s_len), _const_spec(pw.shape), _const_spec(scale.shape)],
        out_specs=[_row_spec(ts, width, 0), _const_spec(pw.shape), _const_spec(scale.shape)],
        out_shape=[jax.ShapeDtypeStruct((s_len, width), BF16), jax.ShapeDtypeStruct(pw.shape, F32),
                   jax.ShapeDtypeStruct(scale.shape, F32)],
        compiler_params=_cparams(("arbitrary",)),
    )(proj, proj, dmixed, dmixed, pw, scale)


def _conv_pre(ext, w_ref, b_ref):
    k_len = w_ref.shape[0]
    y = _roll_down(ext, k_len - 1) * w_ref[0:1, :]
    for k in range(1, k_len):
        y = y + _roll_down(ext, k_len - 1 - k) * w_ref[k:k + 1, :]
    return y + b_ref[...]


def _conv_fwd(name, proj, w, b):
    s_len = proj.shape[0]
    width = w.shape[1]
    ts = _tile(s_len, 512, 8)
    cb = C_XBC // width

    def body(x_ref, hb_ref, w_ref, b_ref, o_ref):
        i = pl.program_id(0)
        ext = jnp.concatenate([jnp.where(i == 0, 0.0, hb_ref[...]), x_ref[...]], axis=0)
        o_ref[...] = _silu(_conv_pre(ext, w_ref, b_ref)[HALO:])

    return pl.pallas_call(
        body, name=name, grid=(s_len // ts,),
        in_specs=[_row_spec(ts, width, cb), _halo_before_spec(ts, width, cb), _const_spec(w.shape), _const_spec(b.shape)],
        out_specs=_row_spec(ts, width, 0),
        out_shape=jax.ShapeDtypeStruct((s_len, width), F32),
        compiler_params=_cparams(("parallel",)),
    )(proj, proj, w, b)


def _conv_bwd(name, proj, dact, w, b):
    s_len = proj.shape[0]
    width = w.shape[1]
    k_len = w.shape[0]
    ts = _tile(s_len, 512, 8)
    n_tiles = s_len // ts
    cb = C_XBC // width

    def body(x_ref, hb_ref, ha_ref, d_ref, da_ref, w_ref, b_ref, dx_ref, dw_ref, db_ref):
        i = pl.program_id(0)
        last = i == n_tiles - 1
        ext = jnp.concatenate([jnp.where(i == 0, 0.0, hb_ref[...]), x_ref[...], jnp.where(last, 0.0, ha_ref[...])], axis=0)
        pre = _conv_pre(ext, w_ref, b_ref)[HALO:]
        dpre = jnp.concatenate([d_ref[...], jnp.where(last, 0.0, da_ref[...])], axis=0) * _dsilu(pre)

        @pl.when(i == 0)
        def _():
            dw_ref[...] = jnp.zeros_like(dw_ref)
            db_ref[...] = jnp.zeros_like(db_ref)

        dx = _roll_up(dpre, k_len - 1) * w_ref[0:1, :]
        for k in range(1, k_len):
            dx = dx + _roll_up(dpre, k_len - 1 - k) * w_ref[k:k + 1, :]
        dx_ref[...] = dx[:ts].astype(dx_ref.dtype)
        dtile = dpre[:ts]
        for k in range(k_len):
            xk = _roll_down(ext, k_len - 1 - k)[HALO:HALO + ts]
            dw_ref[k:k + 1, :] += jnp.sum(dtile * xk, axis=0, keepdims=True)
        db_ref[...] += jnp.sum(dtile, axis=0, keepdims=True)

    return pl.pallas_call(
        body, name=name, grid=(n_tiles,),
        in_specs=[_row_spec(ts, width, cb), _halo_before_spec(ts, width, cb), _halo_after_spec(ts, width, cb, s_len),
                  _row_spec(ts, width, 0), _halo_after_spec(ts, width, 0, s_len), _const_spec(w.shape), _const_spec(b.shape)],
        out_specs=[_row_spec(ts, width, 0), _const_spec(w.shape), _const_spec(b.shape)],
        out_shape=[jax.ShapeDtypeStruct((s_len, width), BF16), jax.ShapeDtypeStruct(w.shape, F32),
                   jax.ShapeDtypeStruct(b.shape, F32)],
        compiler_params=_cparams(("arbitrary",)),
    )(proj, proj, proj, dact, dact, w, b)


def _pair_cols(c0, c1, lo_half):
    return jnp.where(lo_half, c0, c1)


def _ssd_common(dtr_ref, bias_ref, alog_ref, acs_t_ref):
    chunk = dtr_ref.shape[0]
    xpre = dtr_ref[...] + bias_ref[...]
    dt = _softplus(xpre)
    a_neg = -jnp.exp(alog_ref[...])
    tri = _iota((chunk, chunk), 1) <= _iota((chunk, chunk), 0)
    a_cs = _dot3_left(tri.astype(BF16), dt * a_neg)
    acs_t_ref[...] = a_cs.T
    a_last = jnp.sum(jnp.where(_iota(a_cs.shape, 0) == chunk - 1, a_cs, 0.0), axis=0, keepdims=True)
    return xpre, dt, a_neg, tri, a_cs, a_last


def _ssd_specs(chunk, order):
    xs = pl.BlockSpec((chunk, 1024), lambda c: (order(c), 0))
    bm = pl.BlockSpec((chunk, 256), lambda c: (order(c), 4))
    cm = pl.BlockSpec((chunk, 256), lambda c: (order(c), 5))
    lanes = pl.BlockSpec((chunk, LANES), lambda c: (order(c), 0))
    return xs, bm, cm, lanes


def _ssd_fwd(name, xbc, dtr, dt_bias, a_log, d_skip):
    s_len = xbc.shape[0]
    chunk = SSD_CHUNK
    nc = s_len // chunk
    n_pairs = SSD_HEADS // 2

    def body(xs_ref, b_ref, c_ref, dtr_ref, bias_ref, alog_ref, dsk_ref, y_ref, st_ref, state_ref, acs_t_ref):
        @pl.when(pl.program_id(0) == 0)
        def _():
            state_ref[...] = jnp.zeros_like(state_ref)

        _, dt, _, tri, a_cs, a_last = _ssd_common(dtr_ref, bias_ref, alog_ref, acs_t_ref)
        lo_half = _iota((chunk, LANES), 1) < HEAD_DIM
        lo_lane = _iota((1, LANES), 1) < HEAD_DIM
        lo_row = _iota((LANES, 1), 0) < HEAD_DIM
        dsk = dsk_ref[...]
        for g in range(2):
            gsl = slice(g * LANES, (g + 1) * LANES)
            bg, cg = b_ref[:, gsl], c_ref[:, gsl]
            gmat = _dot_nt(cg, bg)
            for pr in range(n_pairs // 2):
                pair = g * (n_pairs // 2) + pr
                h0, h1 = 2 * pair, 2 * pair + 1
                psl = slice(pair * LANES, (pair + 1) * LANES)
                x2 = xs_ref[:, psl]
                acs0, acs1 = _col(a_cs, h0), _col(a_cs, h1)
                xdt = x2 * _pair_cols(_col(dt, h0), _col(dt, h1), lo_half)
                y2 = jnp.zeros((chunk, LANES), F32)
                for h, acs_c, hmask in ((h0, acs0, lo_half), (h1, acs1, ~lo_half)):
                    lam = jnp.where(tri, jnp.exp(jnp.minimum(acs_c - acs_t_ref[h:h + 1, :], 0.0)), 0.0)
                    y2 = y2 + _dot(gmat * lam, jnp.where(hmask, xdt, 0.0))
                s2 = state_ref[pair]
                st_ref[0, pair] = s2
                y2 = y2 + _pair_cols(jnp.exp(acs0), jnp.exp(acs1), lo_half) * _dot_nt(cg, s2)
                y_ref[:, psl] = y2 + _pair_cols(_col(dsk, h0), _col(dsk, h1), lo_lane) * x2
                al0, al1 = _col(a_last, h0), _col(a_last, h1)
                wl2 = _pair_cols(jnp.exp(al0 - acs0), jnp.exp(al1 - acs1), lo_half)
                state_ref[pair] = _pair_cols(jnp.exp(al0), jnp.exp(al1), lo_row) * s2 + _dot((xdt * wl2).T, bg)

    xs, bm, cm, lanes = _ssd_specs(chunk, lambda c: c)
    vec = _const_spec((1, LANES))
    return pl.pallas_call(
        body, name=name, grid=(nc,),
        in_specs=[xs, bm, cm, lanes, vec, vec, vec],
        out_specs=[xs, pl.BlockSpec((1, n_pairs, LANES, LANES), lambda c: (c, 0, 0, 0))],
        out_shape=[jax.ShapeDtypeStruct((s_len, 1024), F32), jax.ShapeDtypeStruct((nc, n_pairs, LANES, LANES), F32)],
        scratch_shapes=[pltpu.VMEM((n_pairs, LANES, LANES), F32), pltpu.VMEM((LANES, chunk), F32)],
        compiler_params=_cparams(("arbitrary",)),
    )(xbc, xbc, xbc, dtr, dt_bias, a_log, d_skip)


def _ssd_bwd(name, xbc, dtr, states, dy, dt_bias, a_log, d_skip):
    s_len = xbc.shape[0]
    chunk = SSD_CHUNK
    nc = s_len // chunk
    n_pairs = SSD_HEADS // 2
    rev = lambda c: nc - 1 - c

    def body(xs_ref, b_ref, c_ref, dtr_ref, dy_ref, sin_ref, bias_ref, alog_ref, dsk_ref,
             dxs_ref, db_ref, dc_ref, ddtr_ref, dbias_ref, dalog_ref, ddsk_ref, dstate_ref, acs_t_ref):
        @pl.when(pl.program_id(0) == 0)
        def _():
            dstate_ref[...] = jnp.zeros_like(dstate_ref)
            dbias_ref[...] = jnp.zeros_like(dbias_ref)
            dalog_ref[...] = jnp.zeros_like(dalog_ref)
            ddsk_ref[...] = jnp.zeros_like(ddsk_ref)

        xpre, dt, a_neg, tri, a_cs, a_last = _ssd_common(dtr_ref, bias_ref, alog_ref, acs_t_ref)
        lane = _iota((chunk, LANES), 1)
        lo_half = lane < HEAD_DIM
        lane1 = _iota((1, LANES), 1)
        lo_lane = lane1 < HEAD_DIM
        lo_row = _iota((LANES, 1), 0) < HEAD_DIM
        head_row = _iota((LANES, chunk), 0)
        sq_row, sq_col = _iota((chunk, chunk), 0), _iota((chunk, chunk), 1)
        before = (sq_row < sq_col).astype(BF16)
        dsk = dsk_ref[...]
        da_rows = jnp.zeros((LANES, chunk), F32)
        yo = jnp.zeros((chunk, LANES), F32)
        to = jnp.zeros((chunk, LANES), F32)
        vs = jnp.zeros((1, LANES), F32)
        ddt = jnp.zeros((chunk, LANES), F32)
        ddsk = jnp.zeros((1, LANES), F32)

        def half_sums(v):
            lo = jnp.sum(jnp.where(lo_half, v, 0.0), axis=1, keepdims=True)
            return lo, jnp.sum(v, axis=1, keepdims=True) - lo

        for g in range(2):
            gsl = slice(g * LANES, (g + 1) * LANES)
            bg, cg = b_ref[:, gsl], c_ref[:, gsl]
            gmat = _dot_nt(cg, bg)
            dgm = jnp.zeros((chunk, chunk), F32)
            dbg = jnp.zeros((chunk, LANES), F32)
            dcg = jnp.zeros((chunk, LANES), F32)
            for pr in range(n_pairs // 2):
                pair = g * (n_pairs // 2) + pr
                h0, h1 = 2 * pair, 2 * pair + 1
                psl = slice(pair * LANES, (pair + 1) * LANES)
                x2, dy2 = xs_ref[:, psl], dy_ref[:, psl]
                acs0, acs1 = _col(a_cs, h0), _col(a_cs, h1)
                dt2 = _pair_cols(_col(dt, h0), _col(dt, h1), lo_half)
                xdt = x2 * dt2
                al0, al1 = _col(a_last, h0), _col(a_last, h1)
                v2 = _pair_cols(jnp.exp(acs0), jnp.exp(acs1), lo_half)
                wl2 = _pair_cols(jnp.exp(al0 - acs0), jnp.exp(al1 - acs1), lo_half)
                s_in, ds2 = sin_ref[0, pair], dstate_ref[pair]
                y_off = v2 * _dot_nt(cg, s_in)
                dx_state = wl2 * _dot_nt(bg, ds2)
                dx2 = dx_state
                for h, acs_c, hmask in ((h0, acs0, lo_half), (h1, acs1, ~lo_half)):
                    lam = jnp.where(tri, jnp.exp(jnp.minimum(acs_c - acs_t_ref[h:h + 1, :], 0.0)), 0.0)
                    m = gmat * lam
                    dyh = jnp.where(hmask, dy2, 0.0)
                    dx2 = dx2 + _dot(m.T, dyh)
                    dml = _dot_nt(dyh, xdt) * lam
                    dgm = dgm + dml
                    crossed = jnp.where(sq_row >= sq_col, _dot3_stacked(dml * gmat, before), 0.0)
                    da_rows = jnp.where(head_row == h, jnp.sum(crossed, axis=0, keepdims=True), da_rows)
                vdy = v2 * dy2
                dcg = dcg + _dot(vdy, s_in)
                dbg = dbg + _dot(wl2 * xdt, ds2)
                yo0, yo1 = half_sums(dy2 * y_off)
                yo = jnp.where(lane == h0, yo0, jnp.where(lane == h1, yo1, yo))
                to0, to1 = half_sums(dx_state * xdt)
                to = jnp.where(lane == h0, to0, jnp.where(lane == h1, to1, to))
                prod = jnp.sum(ds2 * s_in, axis=1, keepdims=True)
                e0 = jnp.sum(jnp.where(lo_row, prod, 0.0), axis=0, keepdims=True)
                e1 = jnp.sum(prod, axis=0, keepdims=True) - e0
                vs = jnp.where(lane1 == h0, jnp.exp(al0) * e0, jnp.where(lane1 == h1, jnp.exp(al1) * e1, vs))
                q0, q1 = half_sums(dx2 * x2)
                ddt = jnp.where(lane == h0, q0, jnp.where(lane == h1, q1, ddt))
                dxs_ref[:, psl] = dx2 * dt2 + _pair_cols(_col(dsk, h0), _col(dsk, h1), lo_lane) * dy2
                s0, s1 = half_sums(dy2 * x2)
                ddsk = jnp.where(lane1 == h0, jnp.sum(s0, axis=0, keepdims=True),
                                 jnp.where(lane1 == h1, jnp.sum(s1, axis=0, keepdims=True), ddsk))
                dstate_ref[pair] = _pair_cols(jnp.exp(al0), jnp.exp(al1), lo_row) * ds2 + _dot(vdy.T, cg)
            dc_ref[:, gsl] = dcg + _dot(dgm, bg)
            db_ref[:, gsl] = dbg + _dot(dgm.T, cg)

        da = (da_rows.T + _dot3_left((sq_col >= sq_row).astype(BF16), yo)
              + _dot3_left((sq_col < sq_row).astype(BF16), to) + vs)
        ddt = ddt + da * a_neg
        dalog_ref[...] += jnp.sum(da * dt, axis=0, keepdims=True) * a_neg
        ddtr = jnp.where(lane < SSD_HEADS, ddt * jax.nn.sigmoid(xpre), 0.0)
        ddtr_ref[...] = ddtr
        dbias_ref[...] += jnp.sum(ddtr, axis=0, keepdims=True)
        ddsk_ref[...] += ddsk

    xs, bm, cm, lanes = _ssd_specs(chunk, rev)
    vec = _const_spec((1, LANES))
    st_in = pl.BlockSpec((1, n_pairs, LANES, LANES), lambda c: (rev(c), 0, 0, 0))
    bc_out = pl.BlockSpec((chunk, 256), lambda c: (rev(c), 0))
    return pl.pallas_call(
        body, name=name, grid=(nc,),
        in_specs=[xs, bm, cm, lanes, xs, st_in, vec, vec, vec],
        out_specs=[xs, bc_out, bc_out, lanes, vec, vec, vec],
        out_shape=[jax.ShapeDtypeStruct((s_len, 1024), F32), jax.ShapeDtypeStruct((s_len, 256), F32),
                   jax.ShapeDtypeStruct((s_len, 256), F32), jax.ShapeDtypeStruct((s_len, LANES), F32),
                   jax.ShapeDtypeStruct((1, LANES), F32), jax.ShapeDtypeStruct((1, LANES), F32),
                   jax.ShapeDtypeStruct((1, LANES), F32)],
        scratch_shapes=[pltpu.VMEM((n_pairs, LANES, LANES), F32), pltpu.VMEM((LANES, chunk), F32)],
        compiler_params=_cparams(("arbitrary",)),
    )(xbc, xbc, xbc, dtr, dy, states, dt_bias, a_log, d_skip)


def _gatenorm_fwd(name, y, proj, g):
    s_len = y.shape[0]
    ts = _tile(s_len, 512, 8)
    gw = 512

    def body(y_ref, z_ref, g_ref, o_ref):
        yg = y_ref[...] * _silu(z_ref[...])
        rstd = lax.rsqrt(jnp.mean(yg * yg, axis=-1, keepdims=True) + RMS_EPS)
        o_ref[...] = (yg * rstd * g_ref[...]).astype(o_ref.dtype)

    return pl.pallas_call(
        body, name=name, grid=(2, s_len // ts),
        in_specs=[pl.BlockSpec((ts, gw), lambda gi, i: (i, gi)), pl.BlockSpec((ts, gw), lambda gi, i: (i, C_Z // gw + gi)),
                  pl.BlockSpec((1, gw), lambda gi, i: (0, gi))],
        out_specs=pl.BlockSpec((ts, gw), lambda gi, i: (i, gi)),
        out_shape=jax.ShapeDtypeStruct((s_len, 2 * gw), BF16),
        compiler_params=_cparams(("parallel", "parallel")),
    )(y, proj, g)


def _gatenorm_bwd(name, y, proj, g, dmixed):
    s_len = y.shape[0]
    ts = _tile(s_len, 512, 8)
    gw = 512

    def body(y_ref, z_ref, g_ref, d_ref, dy_ref, dz_ref, dg_ref):
        yv, z = y_ref[...], z_ref[...]
        sz = _silu(z)
        yg = yv * sz
        rstd = lax.rsqrt(jnp.mean(yg * yg, axis=-1, keepdims=True) + RMS_EPS)
        n = yg * rstd
        dn = d_ref[...] * g_ref[...]
        dyg = rstd * (dn - n * jnp.mean(dn * n, axis=-1, keepdims=True))
        dy_ref[...] = dyg * sz
        dz_ref[...] = (dyg * yv * _dsilu(z)).astype(dz_ref.dtype)

        @pl.when(pl.program_id(1) == 0)
        def _():
            dg_ref[...] = jnp.zeros_like(dg_ref)

        dg_ref[...] += jnp.sum(d_ref[...] * n, axis=0, keepdims=True)

    grp = pl.BlockSpec((ts, gw), lambda gi, i: (i, gi))
    vec = pl.BlockSpec((1, gw), lambda gi, i: (0, gi))
    return pl.pallas_call(
        body, name=name, grid=(2, s_len // ts),
        in_specs=[grp, pl.BlockSpec((ts, gw), lambda gi, i: (i, C_Z // gw + gi)), vec,
                  pl.BlockSpec((ts, gw), lambda gi, i: (i, 1 + gi))],
        out_specs=[grp, grp, vec],
        out_shape=[jax.ShapeDtypeStruct((s_len, 2 * gw), F32), jax.ShapeDtypeStruct((s_len, 2 * gw), BF16),
                   jax.ShapeDtypeStruct((1, 2 * gw), F32)],
        compiler_params=_cparams(("parallel", "arbitrary")),
    )(y, proj, g, dmixed)


def _stack_heads(x2, lo):
    return jnp.concatenate([jnp.where(lo, x2, 0.0), jnp.where(lo, 0.0, x2)], axis=0)


def _attn_scores(q_st, kblk, mask, ustrict, r):
    z = _dot_nt(q_st, kblk)
    sp = jnp.log1p(jnp.exp(-jnp.abs(z)))
    ls = jnp.minimum(z, 0.0) - sp
    lm_raw = jnp.minimum(-z, 0.0) - sp
    lm = jnp.where(mask, lm_raw, 0.0)
    w = jnp.where(mask, jnp.exp(ls + _dot3_stacked(lm, ustrict) + r), 0.0)
    return ls, lm_raw, lm, w


def _attn_masks(qi, tq, tk):
    row = _iota((2 * tq, tk), 0)
    return qi * tq + jnp.where(row < tq, row, row - tq), _iota((2 * tq, tk), 1)


def _attn_tiles(s_len):
    tk = ATTN_BLOCK
    return next(m * tk for m in (2, 1) if s_len % (m * tk) == 0), tk


def _attn_kv(proj):
    return proj[:, C_K:C_END].astype(BF16)


def _attn_specs(tq, s_len):
    qcol, vcol = C_Q // LANES, (C_V - C_K) // LANES
    q = pl.BlockSpec((tq, LANES), lambda p, i: (i, qcol + p))
    k = pl.BlockSpec((s_len, LANES), lambda p, i: (0, p))
    v = pl.BlockSpec((s_len, LANES), lambda p, i: (0, vcol + p))
    return q, k, v


def _attn_fwd(name, proj, kv):
    s_len = proj.shape[0]
    tq, tk = _attn_tiles(s_len)
    n_slabs = 4

    def body(q_ref, k_ref, v_ref, o_ref):
        qi = pl.program_id(1)
        q2 = q_ref[...] * (HEAD_DIM ** -0.5)
        lo = _iota((tq, LANES), 1) < HEAD_DIM
        lo_k = _iota((tk, LANES), 1) < HEAD_DIM
        q_st = _stack_heads(q2, lo).astype(BF16)
        ustrict = (_iota((tk, tk), 0) > _iota((tk, tk), 1)).astype(BF16)
        q_pos, k_off = _attn_masks(qi, tq, tk)

        def step(carry):
            kb, _, r, acc = carry
            rows = pl.ds(pl.multiple_of(kb * tk, tk), tk)
            kblk, vblk = k_ref[rows, :], v_ref[rows, :]
            _, _, lm, w = _attn_scores(q_st, kblk, kb * tk + k_off < q_pos, ustrict, r)
            wb = w.astype(BF16)
            acc = acc + _dot(jnp.concatenate([wb[:tq], wb[tq:]], axis=1), _stack_heads(vblk, lo_k))
            r = r + jnp.sum(lm, axis=1, keepdims=True)
            return kb - 1, (jnp.max(r) > EXP_UNDERFLOW).astype(jnp.int32), r, acc

        init = ((qi + 1) * (tq // tk) - 1, jnp.int32(1), jnp.zeros((2 * tq, 1), F32), jnp.zeros((tq, LANES), F32))
        o_ref[...] = lax.while_loop(lambda c: (c[0] >= 0) & (c[1] > 0), step, init)[3]

    q, k, v = _attn_specs(tq, s_len)
    return pl.pallas_call(
        body, name=name, grid=(n_slabs, s_len // tq),
        in_specs=[q, k, v], out_specs=pl.BlockSpec((tq, LANES), lambda p, i: (i, p)),
        out_shape=jax.ShapeDtypeStruct((s_len, n_slabs * LANES), F32),
        compiler_params=_cparams(("parallel", "arbitrary")),
    )(proj, kv, kv)


def _attn_bwd(name, proj, kv, dmixed):
    s_len = proj.shape[0]
    tq, tk = _attn_tiles(s_len)
    n_slabs = 4
    scale = HEAD_DIM ** -0.5

    def body(q_ref, k_ref, v_ref, do_ref, dq_ref, dk_ref, dv_ref, dk_acc, dv_acc, r_hist):
        qi = pl.program_id(1)

        @pl.when(qi == 0)
        def _():
            dk_acc[...] = jnp.zeros_like(dk_acc)
            dv_acc[...] = jnp.zeros_like(dv_acc)

        q2 = q_ref[...] * scale
        do2 = do_ref[...]
        lo = _iota((tq, LANES), 1) < HEAD_DIM
        q_st = _stack_heads(q2, lo).astype(BF16)
        do_st = _stack_heads(do2, lo).astype(BF16)
        row, col = _iota((tk, tk), 0), _iota((tk, tk), 1)
        ustrict = (row > col).astype(BF16)
        earlier = (row < col).astype(BF16)
        q_pos, k_off = _attn_masks(qi, tq, tk)
        top = (qi + 1) * (tq // tk) - 1
        zero = jnp.zeros((2 * tq, 1), F32)

        def scan(carry):
            kb, _, r = carry
            kblk = k_ref[pl.ds(pl.multiple_of(kb * tk, tk), tk), :]
            r_hist[kb] = jnp.where(lo, r[:tq], r[tq:])
            z = _dot_nt(q_st, kblk)
            lm = jnp.where(kb * tk + k_off < q_pos, jnp.minimum(-z, 0.0) - jnp.log1p(jnp.exp(-jnp.abs(z))), 0.0)
            r = r + jnp.sum(lm, axis=1, keepdims=True)
            return kb - 1, (jnp.max(r) > EXP_UNDERFLOW).astype(jnp.int32), r

        first = lax.while_loop(lambda c: (c[0] >= 0) & (c[1] > 0), scan, (top, jnp.int32(1), zero))[0] + 1

        def step(carry):
            kb, p, dq = carry
            rows = pl.ds(pl.multiple_of(kb * tk, tk), tk)
            kblk, vblk = k_ref[rows, :], v_ref[rows, :]
            mask = kb * tk + k_off < q_pos
            rr = r_hist[kb]
            r = jnp.concatenate([_col(rr, 0), _col(rr, HEAD_DIM)], axis=0)
            ls, lm_raw, _, w = _attn_scores(q_st, kblk, mask, ustrict, r)
            ew = _dot_nt(do_st, vblk) * w
            dz = jnp.where(mask, ew * jnp.exp(lm_raw) - jnp.exp(ls) * (p + _dot3_stacked(ew, earlier)), 0.0).astype(BF16)
            both = _dot(dz, kblk)
            dq = dq + jnp.where(lo, both[:tq], both[tq:])
            dk_acc[rows, :] += _dot_tn(dz, q_st)
            dv_acc[rows, :] += _dot_tn(w, do_st)
            return kb + 1, p + jnp.sum(ew, axis=1, keepdims=True), dq

        dq = lax.while_loop(lambda c: c[0] <= top, step, (first, zero, jnp.zeros((tq, LANES), F32)))[2]
        dq_ref[...] = (dq * scale).astype(dq_ref.dtype)

        @pl.when(qi == pl.num_programs(1) - 1)
        def _():
            dk_ref[...] = dk_acc[...].astype(dk_ref.dtype)
            dv_ref[...] = dv_acc[...].astype(dv_ref.dtype)

    q, k, v = _attn_specs(tq, s_len)
    blk = pl.BlockSpec((tq, LANES), lambda p, i: (i, p))
    full = pl.BlockSpec((s_len, LANES), lambda p, i: (0, p))
    shape = jax.ShapeDtypeStruct((s_len, n_slabs * LANES), BF16)
    return pl.pallas_call(
        body, name=name, grid=(n_slabs, s_len // tq),
        in_specs=[q, k, v, pl.BlockSpec((tq, LANES), lambda p, i: (i, 1536 // LANES + p))],
        out_specs=[blk, full, full], out_shape=[shape, shape, shape],
        scratch_shapes=[pltpu.VMEM((s_len, LANES), F32), pltpu.VMEM((s_len, LANES), F32),
                        pltpu.VMEM((s_len // tk, tq, LANES), F32)],
        compiler_params=_cparams(("parallel", "arbitrary")),
    )(proj, kv, kv, dmixed)


def _ident(accs, _):
    return accs


def _mixer_fwd(tag, h, p, hooks=_no_hooks):
    u = _rms_fwd(tag + "_norm", h, p["mix_norm"])
    (proj,) = _mm(tag + "_in", [u], [p["w_main"]], nt=False, epilogue=_ident, out_dtypes=[F32], hosted=hooks("in"),
                  tk=2048)
    (dtr,) = _mm(tag + "_indt", [u], [p["w_dt"]], nt=False, epilogue=_ident, out_dtypes=[F32], tk=2048)
    pool_out = _pool_fwd(tag + "_pool", proj, p["pool_w"], p["pool_scale"])
    xbc = _conv_fwd(tag + "_conv", proj, p["conv_w"], p["conv_b"])
    y, states = _ssd_fwd(tag + "_ssd", xbc, dtr, p["dt_bias"], p["a_log"], p["d_skip"])
    ssd_out = _gatenorm_fwd(tag + "_gate", y, proj, p["ssd_norm"])
    kv = _attn_kv(proj)
    attn = _attn_fwd(tag + "_attn", proj, kv)
    mixed = jnp.concatenate([pool_out, ssd_out, attn.astype(BF16)], axis=1)
    (h2,) = _mm(tag + "_out", [mixed], [p["w_out"]], nt=False, extras=[h], epilogue=lambda accs, ex: [ex[0] + accs[0]],
                out_dtypes=[F32], hosted=hooks("out"), tk=2048)
    return h2, (h, u, proj, dtr, xbc, y, states, mixed, kv)


def _mixer_bwd(tag, saved, p, dh2, dh2_bf, hooks=_no_hooks, group=None):
    h, u, proj, dtr, xbc, y, states, mixed, kv = saved
    (dmixed,) = _mm(tag + "_dmix", [dh2_bf], [p["w_out"]], nt=True, epilogue=_ident, out_dtypes=[F32],
                    hosted=hooks("dmix"), tk=2048)
    (dw_out,) = _mm(tag + "_dwout", [mixed], [dh2_bf], nt=False, ta=True, epilogue=_ident, out_dtypes=[F32],
                    hosted=hooks("dwout"), tm=2048, tn=1024, tk=1024)
    dpool_in, dpool_w, dpool_scale = _pool_bwd(tag + "_dpool", proj, dmixed, p["pool_w"], p["pool_scale"])
    dy, dz, dssd_norm = _gatenorm_bwd(tag + "_dgate", y, proj, p["ssd_norm"], dmixed)
    dxs, dbm, dcm, ddtr, ddt_bias, da_log, dd_skip = _ssd_bwd(tag + "_dssd", xbc, dtr, states, dy, p["dt_bias"],
                                                             p["a_log"], p["d_skip"])
    dxbc, dconv_w, dconv_b = _conv_bwd(tag + "_dconv", proj, jnp.concatenate([dxs, dbm, dcm], axis=1), p["conv_w"],
                                       p["conv_b"])
    dq, dk, dv = _attn_bwd(tag + "_dattn", proj, kv, dmixed)
    dproj = jnp.concatenate([dpool_in, dz, dxbc, dq, dk, dv], axis=1)
    ddtr_bf = ddtr.astype(BF16)
    (dw_main,) = _mm(tag + "_dwin", [u], [dproj], nt=False, ta=True, epilogue=_ident, out_dtypes=[F32],
                     hosted=hooks("dwin"), tm=2048, tn=512, tk=1024)
    (dw_dt,) = _mm(tag + "_dwdt", [u], [ddtr_bf], nt=False, ta=True, epilogue=_ident, out_dtypes=[F32])
    dw_in = jnp.concatenate([dw_main[:, :REF_DT], dw_dt[:, :SSD_HEADS], dw_main[:, REF_DT:]], axis=1)
    dw_in = dw_in.reshape(dw_in.shape[0], N_CHIPS, dw_in.shape[1] // N_CHIPS).transpose(1, 0, 2)
    mine = group(dw_in, dw_out) if group else (dw_in, dw_out)
    (du_dt,) = _mm(tag + "_dudt", [ddtr_bf], [p["w_dt"]], nt=True, epilogue=_ident, out_dtypes=[F32], tn=1024)
    (du,) = _mm(tag + "_du", [dproj], [p["w_main"]], nt=True, extras=[du_dt], epilogue=lambda accs, ex: [accs[0] + ex[0]],
                out_dtypes=[F32], hosted=hooks("du") + ([mine.swap()] if group else []), tm=1024, tn=1024, tk=1536)
    dh, dh_bf, dg = _rms_bwd(tag + "_dnorm", h, du, dh2, p["mix_norm"])
    grads = dict(mix_norm=dg, pool_w=dpool_w, pool_scale=dpool_scale, conv_w=dconv_w, conv_b=dconv_b, dt_bias=ddt_bias,
                 a_log=da_log, d_skip=dd_skip, ssd_norm=dssd_norm)
    return dh, dh_bf, grads, mine


def _axes():
    return lax.axis_index("x"), lax.axis_index("y"), lax.axis_index("c")


def _any_specs(n):
    return [pl.BlockSpec(memory_space=pl.ANY) for _ in range(n)]


def _remote(src, dst, send, recv, k, dev):
    return pltpu.make_async_remote_copy(src_ref=src, dst_ref=dst, send_sem=send.at[k], recv_sem=recv.at[k],
                                        device_id=dev, device_id_type=MESH_ID)


def _chip_peers(x, y):
    return [(1 - x, y), (x, 1 - y), (1 - x, 1 - y)]


def _gather_comm(shards, done):
    n = len(shards)

    def make(in_refs, out_refs, send, recv, loc, r0, l0):
        x, y, c = _axes()
        me = 2 * x + y
        local = [pltpu.make_async_copy(in_refs[a], out_refs[a].at[me], loc.at[l0 + a]) for a in range(n)]
        sent = [_remote(in_refs[a], out_refs[a].at[me], send, recv, r0 + 3 * a + k, (px, py, c))
                for a in range(n) for k, (px, py) in enumerate(_chip_peers(x, y))]
        got = [_remote(in_refs[a], out_refs[a].at[2 * px + py], send, recv, r0 + 3 * a + k, (px, py, c))
               for a in range(n) for k, (px, py) in enumerate(_chip_peers(x, y))]
        return local + sent, [g.wait_recv for g in got] + [s.wait_send for s in sent] + [l.wait for l in local]

    return _Comm(shards, [jax.ShapeDtypeStruct((N_CHIPS,) + s.shape, s.dtype) for s in shards], 3 * n, n, make, done)


def _swap_comm(arrs, kinds, outs, done):
    n = len(arrs)

    def make(in_refs, out_refs, send, recv, loc, r0, l0):
        x, y, c = _axes()
        cps = [_remote(in_refs[a] if kinds[a] is None else _half(kinds[a], in_refs[a], 1 - c), out_refs[a], send, recv,
                       r0 + a, (x, y, 1 - c)) for a in range(n)]
        return cps, [cp.wait for cp in cps]

    return _Comm(arrs, outs, n, 0, make, done)


def _chips_comm(wires, kinds, done):
    n = len(wires)

    def make(in_refs, out_refs, send, recv, loc, r0, l0):
        x, y, c = _axes()
        cps = [_remote(_wire_shard(kinds[a], in_refs[a], 2 * px + py), out_refs[a].at[k], send, recv, r0 + 3 * a + k,
                       (px, py, c)) for a in range(n) for k, (px, py) in enumerate(_chip_peers(x, y))]
        return cps, [cp.wait for cp in cps]

    outs = [jax.ShapeDtypeStruct((3,) + _wire_shard_shape(k, w.shape), w.dtype) for k, w in zip(kinds, wires)]
    return _Comm(wires, outs, 3 * n, 0, make, done)


def _run_comm(name, hosted):
    ins, outs, sems, build, deliver = _comm_plan(hosted)

    def body(*refs):
        starts, waits = build(refs[:len(ins)], refs[len(ins):len(ins) + len(outs)], *refs[len(ins) + len(outs):])
        for cp in starts:
            cp.start()
        for wait in waits:
            wait()

    deliver(pl.pallas_call(body, name=name, in_specs=_any_specs(len(ins)), out_specs=_any_specs(len(outs)),
                           out_shape=outs, scratch_shapes=sems)(*ins))


def _half(kind, ref, hc):
    if kind == "col":
        r = ref.shape[0] // 2
        return ref.at[pl.ds(pl.multiple_of(hc * r, 16), r), :]
    if kind == "row":
        w = ref.shape[1] // 2
        return ref.at[:, pl.ds(pl.multiple_of(hc * w, LANES), w)]
    r = ref.shape[1] // 2
    return ref.at[:, pl.ds(pl.multiple_of(hc * r, 16), r), :]


def _half_shape(kind, s):
    return {"col": (s[0] // 2, s[1]), "row": (s[0], s[1] // 2), "win": (s[0], s[1] // 2) + tuple(s[2:])}[kind]


def _wire_shape(kind, hs):
    return (N_CHIPS, hs[0], hs[1] // N_CHIPS) if kind == "col" else tuple(hs)


def _wire_shard(kind, ref, j):
    if kind == "row":
        r = ref.shape[0] // N_CHIPS
        return ref.at[pl.ds(pl.multiple_of(j * r, 16), r), :]
    return ref.at[j]


def _wire_shard_shape(kind, ws):
    return (ws[0] // N_CHIPS, ws[1]) if kind == "row" else tuple(ws[1:])


class _GradGroup:
    def __init__(self, tag, names, kinds, arrs, where, south):
        self.tag, self.names, self.kinds, self.arrs, self.where, self.south = tag, list(names), list(kinds), list(arrs), where, south
        self.h32, self.wire, self.final = None, None, None
        self.from_chips = [None] * len(arrs)

    def swap(self):
        outs = [jax.ShapeDtypeStruct(_half_shape(k, g.shape), g.dtype) for k, g in zip(self.kinds, self.arrs)]

        def done(from_sibling):
            sums = [_sum_pair(f"{self.tag}_{n}_pair", k, g, r, self.where)
                    for n, k, g, r in zip(self.names, self.kinds, self.arrs, from_sibling)]
            self.h32, self.wire = [s[0] for s in sums], [s[1] for s in sums]

        return _swap_comm(self.arrs, self.kinds, outs, done)

    def chips(self, which):
        def done(got):
            for a, r in zip(which, got):
                self.from_chips[a] = r

        return _chips_comm([self.wire[a] for a in which], [self.kinds[a] for a in which], done)

    def share(self):
        parts = [_sum_chips(f"{self.tag}_{n}_chips", k, h, r, self.where)
                 for n, k, h, r in zip(self.names, self.kinds, self.h32, self.from_chips)]

        def done(theirs):
            axis = lambda k: 1 if k == "row" else 0
            self.final = [jnp.concatenate([jnp.where(self.south, mine, other), jnp.where(self.south, other, mine)], axis=axis(k))
                          for k, mine, other in zip(self.kinds, parts, theirs)]

        return _swap_comm(parts, [None] * len(parts), [jax.ShapeDtypeStruct(p.shape, p.dtype) for p in parts], done)

    def grads(self):
        return dict(zip(self.names, self.final))


def _esum(name, grid, block, ins, outs, where):
    n_in = len(ins)

    def body(s_ref, *refs):
        tot = refs[0][...].astype(F32)
        for r in refs[1:n_in]:
            tot = tot + r[...].astype(F32)
        for o in refs[n_in:]:
            o[...] = tot.astype(o.dtype)

    spec = lambda nd, imap: pl.BlockSpec((None,) * (nd - 2) + tuple(block), imap)
    return pl.pallas_call(
        body, name=name,
        grid_spec=pltpu.PrefetchScalarGridSpec(
            num_scalar_prefetch=1, grid=grid,
            in_specs=[spec(a.ndim, m) for a, m in ins], out_specs=[spec(len(s), m) for s, _, m in outs]),
        out_shape=[jax.ShapeDtypeStruct(s, dt) for s, dt, _ in outs],
        compiler_params=_cparams(("parallel",) * len(grid)),
    )(where, *[a for a, _ in ins])


def _sum_pair(name, kind, g, r1, where):
    hshape = _half_shape(kind, g.shape)
    wshape = _wire_shape(kind, hshape)
    if kind == "col":
        block = (_tile(hshape[0], 512, 16), hshape[1] // N_CHIPS)
        nb = hshape[0] // block[0]
        grid = (nb, N_CHIPS)
        gmap = lambda i, j, s: (s[0] * nb + i, j)
        hmap = lambda i, j, s: (i, j)
        wmap = lambda i, j, s: (j, i, 0)
    elif kind == "row":
        block = (_tile(hshape[0], 512, 16), hshape[1])
        grid = (hshape[0] // block[0],)
        gmap = lambda i, s: (i, s[0])
        hmap = wmap = lambda i, s: (i, 0)
    else:
        block = (_tile(hshape[1], 512, 16), hshape[2])
        nb = hshape[1] // block[0]
        grid = (hshape[0], nb)
        gmap = lambda q, i, s: (q, s[0] * nb + i, 0)
        hmap = wmap = lambda q, i, s: (q, i, 0)
    return _esum(name, grid, block, [(g, gmap), (r1, hmap)], [(hshape, F32, hmap), (wshape, BF16, wmap)], where)


def _sum_chips(name, kind, h32, r2, where):
    tshape = tuple(r2.shape[1:])
    block = (_tile(tshape[0], 512, 16), tshape[1])
    nb = tshape[0] // block[0]
    if kind == "col":
        hmap = lambda i, s: (i, s[1])
    elif kind == "row":
        hmap = lambda i, s: (s[1] * nb + i, 0)
    else:
        hmap = lambda i, s: (s[1], i, 0)
    rmap = lambda k: (lambda i, s: (k, i, 0))
    return _esum(name, (nb,), block, [(h32, hmap)] + [(r2, rmap(k)) for k in range(3)],
                 [(tshape, F32, lambda i, s: (i, 0))], where)[0]


def _allreduce_small(name, vec):
    rows_n = vec.shape[0]

    def body(x_ref, sum_ref, all_ref, send, recv, local_sem):
        x, y, c = _axes()
        me, sibling = (x, y, c), (x, y, 1 - c)
        chips = [(1 - x, y), (x, 1 - y), (1 - x, 1 - y)]

        def rows(px, py, pc):
            return all_ref.at[pl.ds(pl.multiple_of((4 * px + 2 * py + pc) * rows_n, 8), rows_n), :]

        def copy(k, block, to, src=None):
            return _remote(rows(*block) if src is None else src, rows(*block), send, recv, k, to)

        mine = pltpu.make_async_copy(x_ref, rows(*me), local_sem)
        mine.start()
        first = [copy(0, me, sibling, src=x_ref)] + [copy(1 + j, me, (*chip, c), src=x_ref) for j, chip in enumerate(chips)]
        for cp in first:
            cp.start()
        passed = [copy(4 + j, (*chip, c), sibling) for j, chip in enumerate(chips)]
        for j, chip in enumerate(chips):
            copy(1 + j, (*chip, c), me).wait_recv()
            passed[j].start()
        copy(0, sibling, me).wait_recv()
        for j, chip in enumerate(chips):
            copy(4 + j, (*chip, 1 - c), me).wait_recv()
        for cp in first + passed:
            cp.wait_send()
        mine.wait()
        tot = all_ref[0:rows_n, :]
        for d in range(1, 8):
            tot = tot + all_ref[d * rows_n:(d + 1) * rows_n, :]
        sum_ref[...] = tot

    vm = pl.BlockSpec(memory_space=pltpu.VMEM)
    return pl.pallas_call(
        body, name=name, in_specs=[vm], out_specs=[vm, vm],
        out_shape=[jax.ShapeDtypeStruct(vec.shape, F32), jax.ShapeDtypeStruct((8 * rows_n, LANES), F32)],
        scratch_shapes=[pltpu.SemaphoreType.DMA((7,)), pltpu.SemaphoreType.DMA((7,)), pltpu.SemaphoreType.DMA],
        compiler_params=pltpu.CompilerParams(vmem_limit_bytes=VMEM_LIMIT),
    )(vec)[0]


def _adamw(name, w, g, m, v):
    shape = w.shape
    rows_n, cols = shape[-2], shape[-1]
    lead = math.prod(shape[:-2])
    tr = _tile(rows_n, 256, 8)

    def body(w_ref, g_ref, m_ref, v_ref, d_ref, m2_ref, v2_ref):
        gv = g_ref[...]
        m2 = ADAM_B1 * m_ref[...] + (1.0 - ADAM_B1) * gv
        v2 = ADAM_B2 * v_ref[...] + (1.0 - ADAM_B2) * jnp.square(gv)
        m_hat = m2 / (1.0 - ADAM_B1 ** ADAM_STEP)
        v_hat = v2 / (1.0 - ADAM_B2 ** ADAM_STEP)
        d_ref[...] = -ADAM_LR * (m_hat / (jnp.sqrt(v_hat) + ADAM_EPS) + ADAM_WD * w_ref[...])
        m2_ref[...] = m2
        v2_ref[...] = v2

    spec = pl.BlockSpec((None, tr, cols), lambda l, i: (l, i, 0))
    flat = (lead, rows_n, cols)
    outs = pl.pallas_call(
        body, name=name, grid=(lead, rows_n // tr), in_specs=[spec] * 4, out_specs=[spec] * 3,
        out_shape=[jax.ShapeDtypeStruct(flat, F32)] * 3,
        compiler_params=_cparams(("parallel", "parallel")),
    )(*[t.reshape(flat) for t in (w, g, m, v)])
    return [o.reshape(shape) for o in outs]


WEIGHTS = ("ffn1_norm", "ffn1_w_gate", "ffn1_w_up", "ffn1_w_down", "mix_norm", "w_in", "pool_w", "pool_scale", "conv_w",
           "conv_b", "dt_bias", "a_log", "d_skip", "ssd_norm", "w_out", "ffn2_norm", "ffn2_w_gate", "ffn2_w_up",
           "ffn2_w_down", "final_norm")
BIG = {"ffn1_w_gate": "col", "ffn1_w_up": "col", "ffn1_w_down": "row", "w_in": "win", "w_out": "row",
       "ffn2_w_gate": "col", "ffn2_w_up": "col", "ffn2_w_down": "row"}
SMALL = tuple(n for n in WEIGHTS if n not in BIG and n != "conv_w")
REF_DT = 3072


def _pack(parts):
    flat = jnp.concatenate([p.reshape(-1) for p in parts])
    rows_n = -(-flat.shape[0] // (8 * LANES)) * 8
    return jnp.pad(flat, (0, rows_n * LANES - flat.shape[0])).reshape(rows_n, LANES)


def _unpack(block, shapes):
    flat, out, at = block.reshape(-1), [], 0
    for s in shapes:
        n = math.prod(s)
        out.append(flat[at:at + n].reshape(s))
        at += n
    return out


def _train_step(a):
    depth = a["ffn1_norm"].shape[0]
    x_id, y_id, c_id = _axes()
    chip = 2 * x_id + y_id
    where = jnp.stack([c_id, chip]).astype(jnp.int32)

    big = list(BIG)
    south = c_id == 0
    full = [dict() for _ in range(depth)]
    conv_full = []

    def fetch(l, names):
        def done(blocks):
            for n, g in zip(names, blocks):
                full[l][n] = jnp.concatenate([g[s] for s in range(N_CHIPS)], axis=0 if BIG[n] == "row" else 1)

        return [_gather_comm([a[n][l].astype(BF16) for n in names], done)] if l < depth else []

    ffn1, mix, ffn2 = ["ffn1_w_gate", "ffn1_w_up", "ffn1_w_down"], ["w_in", "w_out"], ["ffn2_w_gate", "ffn2_w_up", "ffn2_w_down"]
    conv_done = lambda blocks: conv_full.append(jnp.concatenate([blocks[0][s] for s in range(N_CHIPS)], axis=2))
    _run_comm("gather_first", fetch(0, ffn1) + [_gather_comm([a["conv_w"]], conv_done)])
    conv_w = conv_full[0]
    heads128 = lambda v: jnp.pad(v, ((0, 0), (0, LANES - SSD_HEADS)))
    dt_bias, a_log, d_skip = heads128(a["dt_bias"]), heads128(a["a_log"]), heads128(a["d_skip"])

    def mixer_params(l):
        w_in = full[l]["w_in"]
        w_main = jnp.concatenate([w_in[:, :REF_DT], w_in[:, REF_DT + SSD_HEADS:]], axis=1)
        w_dt = jnp.pad(w_in[:, REF_DT:REF_DT + SSD_HEADS], ((0, 0), (0, LANES - SSD_HEADS)))
        return dict(mix_norm=a["mix_norm"][l][None], w_main=w_main, w_dt=w_dt, pool_w=a["pool_w"][l],
                    pool_scale=a["pool_scale"][l][None], conv_w=conv_w[l], conv_b=a["conv_b"][l][None],
                    dt_bias=dt_bias[l][None], a_log=a_log[l][None], d_skip=d_skip[l][None],
                    ssd_norm=a["ssd_norm"][l][None], w_out=full[l]["w_out"])

    def ffn_params(l, which):
        return (a[which + "_norm"][l][None], full[l][which + "_w_gate"], full[l][which + "_w_up"], full[l][which + "_w_down"])

    h = a["x"][0]
    saved, mixer_p = [], []
    for l in range(depth):
        carry = lambda plan: (lambda call: fetch(*plan[call]) if call in plan else [])
        h, s1 = _ffn_fwd(f"l{l}_ffn1", h, *ffn_params(l, "ffn1"),
                         hooks=carry({"up": (l, mix + ffn2[:1]), "down": (l, ffn2[1:2])}))
        mixer_p.append(mixer_params(l))
        h, sm = _mixer_fwd(f"l{l}_mix", h, mixer_p[l], hooks=carry({"in": (l, ffn2[2:])}))
        h, s2 = _ffn_fwd(f"l{l}_ffn2", h, *ffn_params(l, "ffn2"),
                         hooks=carry({"up": (l + 1, ffn1[:2]), "down": (l + 1, ffn1[2:])}))
        saved.append((s1, sm, s2))
    loss_part, dh, dh_bf, dfinal = _loss_head("loss_head", h, a["final_norm"][None], a["loss_target"][0])

    small = {n: [None] * depth for n in SMALL if n != "final_norm"}
    small["conv_w"] = [None] * depth
    groups = []

    def grouper(tag, names):
        def make(*arrs):
            groups.append(_GradGroup(tag, names, [BIG[n] for n in names], arrs, where, south))
            return groups[-1]
        return make

    above = None
    for l in reversed(range(depth)):
        s1, sm, s2 = saved[l]
        hooks = _no_hooks if above is None else (lambda call, g=above: {"dhm": lambda: [g.chips([0, 1])],
                                                                        "dwd": lambda: [g.chips([2])],
                                                                        "dwgu": lambda: [g.share()]}.get(call, list)())
        dh, dh_bf, small["ffn2_norm"][l], f2 = _ffn_bwd(f"l{l}_ffn2", s2, *ffn_params(l, "ffn2"), dh, dh_bf, hooks=hooks,
                                                        group=grouper(f"l{l}_ffn2", ffn2))
        hooks = lambda call, g=f2: {"dmix": lambda: [g.chips([1])], "dwout": lambda: [g.chips([0])],
                                    "dwin": lambda: [g.chips([2])], "du": lambda: [g.share()]}.get(call, list)()
        dh, dh_bf, g, mx = _mixer_bwd(f"l{l}_mix", sm, mixer_p[l], dh, dh_bf, hooks=hooks, group=grouper(f"l{l}_mix", mix))
        for n in ("mix_norm", "pool_w", "pool_scale", "conv_w", "conv_b", "ssd_norm"):
            small[n][l] = g[n]
        for n in ("dt_bias", "a_log", "d_skip"):
            small[n][l] = g[n][:, :SSD_HEADS]
        hooks = lambda call, g=mx: {"dhm": lambda: [g.chips([0, 1])], "dwd": lambda: [g.share()]}.get(call, list)()
        dh, dh_bf, small["ffn1_norm"][l], above = _ffn_bwd(f"l{l}_ffn1", s1, *ffn_params(l, "ffn1"), dh, dh_bf, hooks=hooks,
                                                           group=grouper(f"l{l}_ffn1", ffn1))
    grad_x = dh[None]
    _run_comm("tail_chips", [above.chips([0, 1, 2])])
    _run_comm("tail_share", [above.share()])
    per_layer = {}
    for grp in groups:
        per_layer.update({(grp.tag, n): g for n, g in grp.grads().items()})
    grad = {n: jnp.stack([per_layer[(f"l{l}_{'mix' if n in mix else n[:4]}", n)] for l in range(depth)]) for n in big}

    small_full = {n: jnp.stack([t.reshape(a[n].shape[1:]) for t in small[n]]) for n in SMALL if n != "final_norm"}
    small_full["final_norm"] = dfinal.reshape(a["final_norm"].shape)
    conv_full = jnp.stack(small["conv_w"])
    shapes = [a[n].shape for n in SMALL] + [conv_full.shape]
    reduced = _unpack(_allreduce_small("allreduce_small", _pack([small_full[n] for n in SMALL] + [conv_full])), shapes)
    grad.update(zip(SMALL, reduced[:-1]))
    shard = a["conv_w"].shape[2]
    grad["conv_w"] = lax.dynamic_slice_in_dim(reduced[-1], chip * shard, shard, axis=2)

    delta, new_m, new_v = {}, {}, {}
    for n in big + ["conv_w"]:
        delta[n], new_m[n], new_v[n] = _adamw(f"adamw_{n}", a[n], grad[n], a["m_" + n], a["v_" + n])
    packed = [_pack([a[pre + n] for n in SMALL]) for pre in ("", "m_", "v_")]
    outs = _adamw("adamw_small", packed[0], _pack([grad[n] for n in SMALL]), packed[1], packed[2])
    for store, block in zip((delta, new_m, new_v), outs):
        store.update(zip(SMALL, _unpack(block, [a[n].shape for n in SMALL])))

    loss = lax.psum(loss_part[0, 0], ("x", "y", "c"))
    return (loss, grad_x, *[grad[n] for n in WEIGHTS], *[delta[n] for n in WEIGHTS], *[new_m[n] for n in WEIGHTS],
            *[new_v[n] for n in WEIGHTS])


def kernel(x, ffn1_norm, ffn1_w_gate, ffn1_w_up, ffn1_w_down, mix_norm, w_in, pool_w, pool_scale, conv_w, conv_b, dt_bias, a_log, d_skip, ssd_norm, w_out, ffn2_norm, ffn2_w_gate, ffn2_w_up, ffn2_w_down, final_norm, loss_target, m_ffn1_norm, m_ffn1_w_gate, m_ffn1_w_up, m_ffn1_w_down, m_mix_norm, m_w_in, m_pool_w, m_pool_scale, m_conv_w, m_conv_b, m_dt_bias, m_a_log, m_d_skip, m_ssd_norm, m_w_out, m_ffn2_norm, m_ffn2_w_gate, m_ffn2_w_up, m_ffn2_w_down, m_final_norm, v_ffn1_norm, v_ffn1_w_gate, v_ffn1_w_up, v_ffn1_w_down, v_mix_norm, v_w_in, v_pool_w, v_pool_scale, v_conv_w, v_conv_b, v_dt_bias, v_a_log, v_d_skip, v_ssd_norm, v_w_out, v_ffn2_norm, v_ffn2_w_gate, v_ffn2_w_up, v_ffn2_w_down, v_final_norm):
    return _train_step(dict(locals()))
```
